```python
import jax, jax.numpy as jnp
from jax import lax
import numpy as np

D_MODEL = 2048
BATCH = 2
SEQ = 4096
DEPTH = 1
DEC_BATCH = 32
DEC_SEQ = 1
PAST_LEN = 8192
PAGE_SIZE = 128

HEAD_DIM = 128
N_HEADS_A = 8
N_HEADS_B = 4
DK_B = 128
DV_B = 256
GATE_RANK = 16
GATE_TEMP = 16.0
MOBA_BLOCK = 256
MOBA_TOPK = 3
Q_CHUNK = 32
GLA_CHUNK = 16
D_FF = 4 * D_MODEL
ROPE_THETA = 10000.0
EPS = 1e-6

W_A = N_HEADS_A * HEAD_DIM
W_QB = N_HEADS_B * DK_B
W_VB = N_HEADS_B * DV_B
D_MIX = W_A + W_VB
D_IN = 3 * W_A + 2 * W_QB + 2 * W_VB + GATE_RANK
IN_SPLITS = (W_A, 2 * W_A, 3 * W_A, 3 * W_A + W_QB, 3 * W_A + 2 * W_QB,
             3 * W_A + 2 * W_QB + W_VB, 3 * W_A + 2 * W_QB + 2 * W_VB)

kernel_name = 'hymba_moba_gla_decoder_step'


def rms_norm(x, w):
    xf = x.astype(jnp.float32)
    y = xf * lax.rsqrt(jnp.mean(xf * xf, axis=-1, keepdims=True) + EPS)
    return (y * w.astype(jnp.float32)).astype(x.dtype)


def rotary(x, pos):
    half = HEAD_DIM // 2
    inv_freq = ROPE_THETA ** (-jnp.arange(half, dtype=jnp.float32) / half)
    ang = pos.astype(jnp.float32)[:, None] * inv_freq[None, :]
    cos = jnp.cos(ang)[None, :, None, :]
    sin = jnp.sin(ang)[None, :, None, :]
    xf = x.astype(jnp.float32)
    x1, x2 = xf[..., :half], xf[..., half:]
    return jnp.concatenate([x1 * cos - x2 * sin, x2 * cos + x1 * sin], axis=-1).astype(x.dtype)


def to_blocks(t):
    n, l, h, d = t.shape
    lp = -(-l // MOBA_BLOCK) * MOBA_BLOCK
    t = jnp.pad(t, ((0, 0), (0, lp - l), (0, 0), (0, 0)))
    return t.reshape(n, lp // MOBA_BLOCK, MOBA_BLOCK, h, d).transpose(0, 3, 1, 2, 4)


def moba_attend(q, pos, kblk, vblk, kmean):
    nb = kblk.shape[2]
    kk = min(MOBA_TOPK, nb)
    qf = q.astype(jnp.float32)
    q_blk = pos // MOBA_BLOCK
    gate = jnp.einsum('nqhd,nhbd->nhqb', qf, kmean)
    fully_past = jnp.arange(nb)[None, :] < q_blk[:, None]
    gate = jnp.where(fully_past[None, None], gate, -jnp.inf)
    top_val, top_idx = lax.top_k(gate, kk)
    top_ok = jnp.isfinite(top_val)
    own = jnp.broadcast_to(q_blk[None, None, :, None].astype(top_idx.dtype), top_idx.shape[:-1] + (1,))
    idx = jnp.concatenate([top_idx, own], axis=-1)
    ok = jnp.concatenate([top_ok, jnp.ones(own.shape, dtype=bool)], axis=-1)
    gather = jax.vmap(jax.vmap(lambda tbl, i: tbl[i]))
    k_sel = gather(kblk, idx).astype(jnp.float32)
    v_sel = gather(vblk, idx).astype(jnp.float32)
    key_pos = idx[..., None] * MOBA_BLOCK + jnp.arange(MOBA_BLOCK, dtype=idx.dtype)
    mask = ok[..., None] & (key_pos <= pos[None, None, :, None, None])
    logits = jnp.einsum('nqhd,nhqtkd->nhqtk', qf, k_sel) * (HEAD_DIM ** -0.5)
    logits = jnp.where(mask, logits, -jnp.inf)
    n, h, sq, t, blk = logits.shape
    p = jax.nn.softmax(logits.reshape(n, h, sq, t * blk), axis=-1).reshape(n, h, sq, t, blk)
    out = jnp.einsum('nhqtk,nhqtkd->nqhd', p, v_sel)
    return out.astype(q.dtype)


def moba_prompt(q, pos, kblk, vblk, kmean):
    n, s, h, d = q.shape
    nc = s // Q_CHUNK
    qc = q.reshape(n, nc, Q_CHUNK, h, d).transpose(1, 0, 2, 3, 4)
    pc = pos.reshape(nc, Q_CHUNK)
    out = lax.map(lambda a: moba_attend(a[0], a[1], kblk, vblk, kmean), (qc, pc))
    return out.transpose(1, 0, 2, 3, 4).reshape(n, s, h, d)


def gla_recurrent(q, k, v, log_a, s0):
    n, l = q.shape[:2]
    c = min(GLA_CHUNK, l)
    lp = -(-l // c) * c
    nc = lp // c

    def to_chunks(t):
        t = jnp.pad(t.astype(jnp.float32), ((0, 0), (0, lp - l), (0, 0), (0, 0)))
        return t.reshape(n, nc, c, t.shape[2], t.shape[3]).transpose(0, 3, 1, 2, 4)

    qc, kc, vc, ac = to_chunks(q), to_chunks(k), to_chunks(v), to_chunks(log_a)
    b = jnp.cumsum(ac, axis=3)
    causal = jnp.tril(jnp.ones((c, c), dtype=bool))
    diff = b[..., :, None, :] - b[..., None, :, :]
    decay = jnp.exp(jnp.where(causal[:, :, None], diff, -jnp.inf))
    attn = jnp.einsum('nhctd,nhcsd,nhctsd->nhcts', qc, kc, decay)
    o_intra = jnp.einsum('nhcts,nhcsv->nhctv', attn, vc)
    b_last = b[..., -1, :]
    kv = jnp.einsum('nhcsd,nhcsv->nhcdv', kc * jnp.exp(b_last[..., None, :] - b), vc)

    def step(state, xs):
        dl, kv_c = xs
        return jnp.exp(dl)[..., None] * state + kv_c, state

    s_fin, s_start = lax.scan(step, s0.astype(jnp.float32),
                              (b_last.transpose(2, 0, 1, 3), kv.transpose(2, 0, 1, 3, 4)))
    s_start = s_start.transpose(1, 2, 0, 3, 4)
    o_inter = jnp.einsum('nhctd,nhcdv->nhctv', qc * jnp.exp(b), s_start)
    o = (o_intra + o_inter).transpose(0, 2, 3, 1, 4).reshape(n, lp, q.shape[2], v.shape[-1])[:, :l]
    return o, s_fin


def trunk_layer(x, c, pos, k_past, v_past, s0, w_ada, b_ada, norm_mix_w, w_in, q_norm_w, k_norm_w,
                w_gate_up, b_gate, gla_norm_w, w_out, norm_ffn_w, w_up, w_down):
    n, s, _ = x.shape
    ada = jax.nn.silu(c) @ w_ada + b_ada
    sh1, sc1, g1, sh2, sc2, g2 = [a[:, None, :] for a in jnp.split(ada, 6, axis=-1)]

    h = rms_norm(x, norm_mix_w) * (1 + sc1) + sh1
    qa, ka, va, qb, kb, vb, rb, ab = jnp.split(h @ w_in, IN_SPLITS, axis=-1)

    qa = rotary(rms_norm(qa.reshape(n, s, N_HEADS_A, HEAD_DIM), q_norm_w), pos)
    ka = rotary(rms_norm(ka.reshape(n, s, N_HEADS_A, HEAD_DIM), k_norm_w), pos)
    va = va.reshape(n, s, N_HEADS_A, HEAD_DIM)
    if k_past is None:
        k_all, v_all = ka, va
    else:
        k_all = jnp.concatenate([k_past.astype(ka.dtype), ka], axis=1)
        v_all = jnp.concatenate([v_past.astype(va.dtype), va], axis=1)
    kblk, vblk = to_blocks(k_all), to_blocks(v_all)
    kmean = jnp.mean(kblk.astype(jnp.float32), axis=3)
    if k_past is None:
        o_a = moba_prompt(qa, pos, kblk, vblk, kmean)
    else:
        o_a = moba_attend(qa, pos, kblk, vblk, kmean)

    log_a = jax.nn.log_sigmoid((ab @ w_gate_up + b_gate).astype(jnp.float32)) / GATE_TEMP
    o_b, s_fin = gla_recurrent(qb.reshape(n, s, N_HEADS_B, DK_B) * (DK_B ** -0.5),
                               kb.reshape(n, s, N_HEADS_B, DK_B),
                               vb.reshape(n, s, N_HEADS_B, DV_B),
                               log_a.reshape(n, s, N_HEADS_B, DK_B), s0)
    o_b = rms_norm(o_b.astype(x.dtype), gla_norm_w) * jax.nn.silu(rb.reshape(n, s, N_HEADS_B, DV_B))

    mix = jnp.concatenate([o_a.reshape(n, s, W_A), o_b.reshape(n, s, W_VB)], axis=-1) @ w_out
    x = x + g1 * mix

    h2 = rms_norm(x, norm_ffn_w) * (1 + sc2) + sh2
    x = x + g2 * (jnp.square(jax.nn.relu(h2 @ w_up)) @ w_down)
    return x, ka, va, s_fin


def setup_inputs(seed: int = 0) -> dict:
    key = jax.random.key(seed)
    ks = jax.random.split(key, 24)
    n_pages = PAST_LEN // PAGE_SIZE
    n_used = DEC_BATCH * n_pages
    n_phys = n_used + (n_used + 3) // 4

    def nrm(k, shape, scale):
        return jax.random.normal(k, shape, jnp.float32) * scale

    def gain(k, shape):
        return 1.0 + nrm(k, shape, 0.02)

    page_table = jax.random.permutation(ks[0], n_phys)[:n_used].reshape(DEC_BATCH, n_pages).astype(jnp.int32)
    return {
        'x_prompt': nrm(ks[1], (BATCH, SEQ, D_MODEL), 1.0),
        'x_sample': nrm(ks[2], (DEC_BATCH, DEC_SEQ, D_MODEL), 1.0),
        'c_prompt': nrm(ks[3], (BATCH, D_MODEL), 1.0),
        'c_sample': nrm(ks[4], (DEC_BATCH, D_MODEL), 1.0),
        'cache_k': nrm(ks[5], (DEPTH, n_phys, PAGE_SIZE, N_HEADS_A, HEAD_DIM), 1.0),
        'cache_v': nrm(ks[6], (DEPTH, n_phys, PAGE_SIZE, N_HEADS_A, HEAD_DIM), 1.0),
        'state_gla': nrm(ks[7], (DEPTH, DEC_BATCH, N_HEADS_B, DK_B, DV_B), 0.3),
        'page_table': page_table,
        'w_ada': nrm(ks[8], (DEPTH, D_MODEL, 6 * D_MODEL), 0.5 * D_MODEL ** -0.5),
        'b_ada': nrm(ks[9], (DEPTH, 6 * D_MODEL), 0.01),
        'norm_mix_w': gain(ks[10], (DEPTH, D_MODEL)),
        'w_in': nrm(ks[11], (DEPTH, D_MODEL, D_IN), D_MODEL ** -0.5),
        'q_norm_w': gain(ks[12], (DEPTH, HEAD_DIM)),
        'k_norm_w': gain(ks[13], (DEPTH, HEAD_DIM)),
        'w_gate_up': nrm(ks[14], (DEPTH, GATE_RANK, W_QB), GATE_RANK ** -0.5),
        'b_gate': nrm(ks[15], (DEPTH, W_QB), 0.1),
        'gla_norm_w': gain(ks[16], (DEPTH, DV_B)),
        'w_out': nrm(ks[17], (DEPTH, D_MIX, D_MODEL), D_MIX ** -0.5),
        'norm_ffn_w': gain(ks[18], (DEPTH, D_MODEL)),
        'w_up': nrm(ks[19], (DEPTH, D_MODEL, D_FF), D_MODEL ** -0.5),
        'w_down': nrm(ks[20], (DEPTH, D_FF, D_MODEL), D_FF ** -0.5),
    }


def reference(x_prompt, x_sample, c_prompt, c_sample, cache_k, cache_v, state_gla, page_table,
              w_ada, b_ada, norm_mix_w, w_in, q_norm_w, k_norm_w, w_gate_up, b_gate, gla_norm_w,
              w_out, norm_ffn_w, w_up, w_down):
    n_p, s_p = x_prompt.shape[:2]
    n_s, s_s = x_sample.shape[:2]
    past_len = page_table.shape[1] * PAGE_SIZE
    pos_prompt = jnp.arange(s_p, dtype=jnp.int32)
    pos_sample = past_len + jnp.arange(s_s, dtype=jnp.int32)
    yp, ys = x_prompt, x_sample
    kp, vp, sp, kn, vn, sn = [], [], [], [], [], []
    for l in range(DEPTH):
        w = (w_ada[l], b_ada[l], norm_mix_w[l], w_in[l], q_norm_w[l], k_norm_w[l], w_gate_up[l],
             b_gate[l], gla_norm_w[l], w_out[l], norm_ffn_w[l], w_up[l], w_down[l])
        s0_prompt = jnp.zeros((n_p, N_HEADS_B, DK_B, DV_B), jnp.float32)
        yp, k_new, v_new, s_new = trunk_layer(yp, c_prompt, pos_prompt, None, None, s0_prompt, *w)
        kp.append(k_new); vp.append(v_new); sp.append(s_new)
        k_past = cache_k[l][page_table].reshape(n_s, past_len, N_HEADS_A, HEAD_DIM)
        v_past = cache_v[l][page_table].reshape(n_s, past_len, N_HEADS_A, HEAD_DIM)
        ys, k_new, v_new, s_new = trunk_layer(ys, c_sample, pos_sample, k_past, v_past, state_gla[l], *w)
        kn.append(k_new); vn.append(v_new); sn.append(s_new)
    k_prompt, v_prompt, gla_prompt = jnp.stack(kp), jnp.stack(vp), jnp.stack(sp)
    k_sample, v_sample, gla_sample = jnp.stack(kn), jnp.stack(vn), jnp.stack(sn)
    return (yp, ys, k_prompt, v_prompt, gla_prompt, k_sample, v_sample, gla_sample)
```

```python
import functools

import numpy as np
import jax
import jax.numpy as jnp
from jax import lax
from jax.experimental import pallas as pl
from jax.experimental.pallas import tpu as pltpu

D_MODEL = 2048
PAGE_SIZE = 128
HEAD_DIM = 128
N_HEADS_A = 8
N_HEADS_B = 4
DK_B = 128
DV_B = 256
GATE_RANK = 16
GATE_TEMP = 16.0
MOBA_BLOCK = 256
MOBA_TOPK = 3
GLA_SUB = 16
GLA_CHUNK = 128
D_FF = 4 * D_MODEL
ROPE_THETA = 10000.0
EPS = 1e-6

W_A = N_HEADS_A * HEAD_DIM
W_QB = N_HEADS_B * DK_B
W_VB = N_HEADS_B * DV_B
W_GLA = 2 * W_QB + 2 * W_VB + GATE_RANK
LANES = 128
W_GLA_PAD = -(-W_GLA // LANES) * LANES

F32 = jnp.float32
BF16 = jnp.bfloat16
NEG_INF = float("-inf")
VMEM_LIMIT = 56 * 1024 * 1024

_NT = (((1,), (1,)), ((), ()))


def _params(*sem):
    return pltpu.CompilerParams(dimension_semantics=sem, vmem_limit_bytes=VMEM_LIMIT)


def _ada_kernel(c_ref, w_ref, b_ref, o_ref):
    c = c_ref[...]
    a = (c / (1.0 + jnp.exp(-c))).astype(BF16)
    o_ref[...] = jnp.dot(a, w_ref[...].astype(BF16), preferred_element_type=F32) + b_ref[...]


def _ada(c, w, b, tn=512):
    m, d = c.shape
    n = w.shape[1]
    return pl.pallas_call(
        _ada_kernel,
        grid=(n // tn,),
        in_specs=[pl.BlockSpec((m, d), lambda j: (0, 0)),
                  pl.BlockSpec((d, tn), lambda j: (0, j)),
                  pl.BlockSpec((1, tn), lambda j: (0, j))],
        out_specs=pl.BlockSpec((m, tn), lambda j: (0, j)),
        out_shape=jax.ShapeDtypeStruct((m, n), F32),
        compiler_params=_params("arbitrary"),
        name="ada_proj",
    )(c, w, b.reshape(1, n))


def _mod_spec(mod, tiles_per_group, width=None):
    _, r, w = mod.shape
    w = width or w
    return pl.BlockSpec((None, r, w), lambda i, *_: (i // tiles_per_group, 0, 0))


def _rms(x, w):
    return x * lax.rsqrt(jnp.mean(x * x, axis=-1, keepdims=True) + EPS) * w


def _norm_mod_kernel(x_ref, sc_ref, sh_ref, w_ref, o_ref):
    y = _rms(x_ref[...], w_ref[...])
    o_ref[...] = (y * (1.0 + sc_ref[...]) + sh_ref[...]).astype(o_ref.dtype)


def _norm_mod(x, sc, sh, w, tm, tiles_per_group):
    m, d = x.shape
    return pl.pallas_call(
        _norm_mod_kernel,
        grid=(m // tm,),
        in_specs=[pl.BlockSpec((tm, d), lambda i: (i, 0)),
                  _mod_spec(sc, tiles_per_group), _mod_spec(sh, tiles_per_group),
                  pl.BlockSpec((1, d), lambda i: (0, 0))],
        out_specs=pl.BlockSpec((tm, d), lambda i: (i, 0)),
        out_shape=jax.ShapeDtypeStruct((m, d), BF16),
        compiler_params=_params("arbitrary"),
        name="norm_mod",
    )(x, sc, sh, w.reshape(1, d))


def _mm_kernel(a_ref, b_ref, o_ref):
    o_ref[...] = jnp.dot(a_ref[...], b_ref[...], preferred_element_type=F32).astype(o_ref.dtype)


def _matmul(a, b, tm, tn, out_dtype=F32):
    m, k = a.shape
    n = b.shape[1]
    return pl.pallas_call(
        _mm_kernel,
        grid=(m // tm, n // tn),
        in_specs=[pl.BlockSpec((tm, k), lambda i, j: (i, 0)),
                  pl.BlockSpec((k, tn), lambda i, j: (0, j))],
        out_specs=pl.BlockSpec((tm, tn), lambda i, j: (i, j)),
        out_shape=jax.ShapeDtypeStruct((m, n), out_dtype),
        compiler_params=_params("arbitrary", "arbitrary"),
        name="proj",
    )(a, b)


def _qk_prep_kernel(x_ref, cos_ref, sin_ref, w_ref, o_ref, *, scale):
    cos = cos_ref[...]
    sin = sin_ref[...]
    w = w_ref[...]
    for h in range(N_HEADS_A):
        sl = slice(h * HEAD_DIM, (h + 1) * HEAD_DIM)
        y = _rms(x_ref[:, sl], w)
        r = y * cos + pltpu.roll(y, HEAD_DIM // 2, 1) * sin
        if scale != 1.0:
            r = r * scale
        o_ref[:, sl] = r.astype(o_ref.dtype)


def _qk_prep(qk, col_block, cos, sin, w, tm, pos_tiles, scale, out_dtype):
    m = qk.shape[0]
    return pl.pallas_call(
        functools.partial(_qk_prep_kernel, scale=scale),
        grid=(m // tm,),
        in_specs=[pl.BlockSpec((tm, W_A), lambda i: (i, col_block)),
                  pl.BlockSpec((tm, HEAD_DIM), lambda i: (i % pos_tiles, 0)),
                  pl.BlockSpec((tm, HEAD_DIM), lambda i: (i % pos_tiles, 0)),
                  pl.BlockSpec((1, HEAD_DIM), lambda i: (0, 0))],
        out_specs=pl.BlockSpec((tm, W_A), lambda i: (i, 0)),
        out_shape=jax.ShapeDtypeStruct((m, W_A), out_dtype),
        compiler_params=_params("arbitrary"),
        name="qk_prep",
    )(qk, cos, sin, w.reshape(1, HEAD_DIM))


def _rope_tables(pos):
    half = HEAD_DIM // 2
    inv_freq = ROPE_THETA ** (-jnp.arange(half, dtype=F32) / half)
    ang = pos.astype(F32)[:, None] * inv_freq[None, :]
    cos, sin = jnp.cos(ang), jnp.sin(ang)
    return jnp.concatenate([cos, cos], axis=-1), jnp.concatenate([-sin, sin], axis=-1)


def _topk_select(g, valid, axis):
    n = g.shape[axis]
    idx = lax.broadcasted_iota(jnp.int32, g.shape, axis)
    gm = jnp.where(valid, g, NEG_INF)
    rank = jnp.zeros(g.shape, F32)
    for c in range(n):
        gc = lax.slice_in_dim(gm, c, c + 1, axis=axis)
        beats = (gc > gm) | ((gc == gm) & (idx > c))
        rank = rank + beats.astype(F32)
    return valid & (rank < MOBA_TOPK)


def _moba_prompt_kernel(q_ref, k_ref, v_ref, o_ref, kb_scr, vt_scr, km_scr, sel_scr, *, nb):
    qb = pl.program_id(2)
    blk = MOBA_BLOCK

    @pl.when(qb == 0)
    def _prep():
        for j in range(nb):
            kj = k_ref[j * blk:(j + 1) * blk, :]
            kb_scr[j] = kj.astype(BF16)
            km_scr[j:j + 1, :] = jnp.sum(kj, axis=0, keepdims=True) * (1.0 / blk)
            vt_scr[j] = v_ref[j * blk:(j + 1) * blk, :].T.astype(BF16)

    q = q_ref[...]
    gate = lax.dot_general(km_scr[...].astype(BF16), q, _NT, preferred_element_type=F32)
    cand = lax.broadcasted_iota(jnp.int32, gate.shape, 0)
    sel_scr[...] = _topk_select(gate, cand < qb, 0).astype(F32)

    key_i = lax.broadcasted_iota(jnp.int32, (blk, blk), 0)
    qry_i = lax.broadcasted_iota(jnp.int32, (blk, blk), 1)
    s = lax.dot_general(kb_scr[qb], q, _NT, preferred_element_type=F32)
    s = jnp.where(key_i <= qry_i, s, NEG_INF)
    m = jnp.max(s, axis=0, keepdims=True)
    p = jnp.exp(s - m)
    l = jnp.sum(p, axis=0, keepdims=True)
    acc = jnp.dot(vt_scr[qb], p.astype(BF16), preferred_element_type=F32)

    def body(j, carry):
        m, l, acc = carry
        s = lax.dot_general(kb_scr[j], q, _NT, preferred_element_type=F32)
        s = jnp.where(sel_scr[pl.ds(j, 1), :] > 0.0, s, NEG_INF)
        m_new = jnp.maximum(m, jnp.max(s, axis=0, keepdims=True))
        alpha = jnp.exp(m - m_new)
        p = jnp.exp(s - m_new)
        l = alpha * l + jnp.sum(p, axis=0, keepdims=True)
        acc = alpha * acc + jnp.dot(vt_scr[j], p.astype(BF16), preferred_element_type=F32)
        return m_new, l, acc

    m, l, acc = lax.fori_loop(0, qb, body, (m, l, acc))
    o_ref[...] = (acc * (1.0 / l)).T.astype(o_ref.dtype)


def _moba_prompt(q, k, v, n_seq, seq):
    nb = seq // MOBA_BLOCK
    return pl.pallas_call(
        functools.partial(_moba_prompt_kernel, nb=nb),
        grid=(n_seq, N_HEADS_A, nb),
        in_specs=[pl.BlockSpec((MOBA_BLOCK, HEAD_DIM), lambda n, h, b: (n * nb + b, h)),
                  pl.BlockSpec((seq, HEAD_DIM), lambda n, h, b: (n, h)),
                  pl.BlockSpec((seq, HEAD_DIM), lambda n, h, b: (n, h))],
        out_specs=pl.BlockSpec((MOBA_BLOCK, HEAD_DIM), lambda n, h, b: (n * nb + b, h)),
        out_shape=jax.ShapeDtypeStruct((n_seq * seq, W_A), BF16),
        scratch_shapes=[pltpu.VMEM((nb, MOBA_BLOCK, HEAD_DIM), BF16),
                        pltpu.VMEM((nb, HEAD_DIM, MOBA_BLOCK), BF16),
                        pltpu.VMEM((nb, HEAD_DIM), F32),
                        pltpu.VMEM((nb, MOBA_BLOCK), F32)],
        compiler_params=_params("arbitrary", "arbitrary", "arbitrary"),
        name="moba_prompt",
    )(q, k, v)


def _gla_cum_matrices():
    c = GLA_CHUNK
    t = np.arange(c)
    tri = (t[None, :] <= t[:, None]).astype(np.float32)
    mats = [tri]
    m = GLA_SUB
    while m < c:
        ref_tok = (t // (2 * m)) * (2 * m) + m - 1
        mats.append(tri[ref_tok])
        m *= 2
    return np.concatenate(mats, axis=0)


def _log_sigmoid(x):
    return jnp.minimum(x, 0.0) - jnp.log1p(jnp.exp(-jnp.abs(x)))


def _silu(x):
    return x / (1.0 + jnp.exp(-x))


def _gla_prompt_kernel(g_ref, cm_ref, wg_ref, bg_ref, nw_ref, o_ref, s_ref):
    c = GLA_CHUNK

    @pl.when(pl.program_id(1) == 0)
    def _init():
        s_ref[...] = jnp.zeros(s_ref.shape, F32)

    tok = lax.broadcasted_iota(jnp.int32, (c, c), 0)
    col = lax.broadcasted_iota(jnp.int32, (c, c), 1)
    tok_d = lax.broadcasted_iota(jnp.int32, (c, DK_B), 0)
    ab = g_ref[:, 2 * W_QB + 2 * W_VB:W_GLA]
    for h in range(N_HEADS_B):
        pre = jnp.dot(ab, wg_ref[:, h * DK_B:(h + 1) * DK_B], preferred_element_type=F32,
                      precision=lax.Precision.HIGHEST) + bg_ref[:, h * DK_B:(h + 1) * DK_B]
        la = _log_sigmoid(pre) * (1.0 / GATE_TEMP)
        cums = jnp.dot(cm_ref[...], la, preferred_element_type=F32, precision=lax.Precision.HIGHEST)
        b = cums[0:c]
        q = g_ref[:, h * DK_B:(h + 1) * DK_B] * (DK_B ** -0.5)
        k = g_ref[:, W_QB + h * DK_B:W_QB + (h + 1) * DK_B]
        v = g_ref[:, 2 * W_QB + h * DV_B:2 * W_QB + (h + 1) * DV_B]
        rb = g_ref[:, 2 * W_QB + W_VB + h * DV_B:2 * W_QB + W_VB + (h + 1) * DV_B]
        vb = v.astype(BF16)

        attn = jnp.zeros((c, c), F32)
        for d in range(GLA_SUB):
            ks = pltpu.roll(k, d, 0) if d else k
            bs = pltpu.roll(b, d, 0) if d else b
            ok = (tok_d % GLA_SUB) >= d
            term = q * ks * jnp.exp(jnp.where(ok, b - bs, NEG_INF))
            a_d = jnp.sum(term, axis=-1, keepdims=True)
            attn = attn + jnp.where(col == tok - d, a_d, 0.0)
        m = GLA_SUB
        lvl = 1
        while m < c:
            r = cums[lvl * c:(lvl + 1) * c]
            upper = ((tok_d // m) % 2) == 1
            qt = q * jnp.exp(jnp.where(upper, b - r, NEG_INF))
            kt = k * jnp.exp(jnp.where(upper, NEG_INF, r - b))
            a_m = lax.dot_general(qt.astype(BF16), kt.astype(BF16), _NT, preferred_element_type=F32)
            attn = attn + jnp.where((tok // (2 * m)) == (col // (2 * m)), a_m, 0.0)
            m *= 2
            lvl += 1

        state = s_ref[0, h]
        o = jnp.dot(attn.astype(BF16), vb, preferred_element_type=F32)
        o = o + jnp.dot((q * jnp.exp(b)).astype(BF16), state.astype(BF16), preferred_element_type=F32)
        b_last = b[c - 1:c, :]
        kd = k * jnp.exp(b_last - b)
        kv = jnp.dot(kd.T.astype(BF16), vb, preferred_element_type=F32)
        dcol = b.T[:, c - 1:c]
        s_ref[0, h] = jnp.exp(dcol) * state + kv

        y = _rms(o, nw_ref[...]) * _silu(rb)
        o_ref[:, h * DV_B:(h + 1) * DV_B] = y.astype(o_ref.dtype)


def _gla_prompt(g, wg, bg, nw, n_seq, seq):
    nc = seq // GLA_CHUNK
    cm = jnp.asarray(_gla_cum_matrices())
    return pl.pallas_call(
        _gla_prompt_kernel,
        grid=(n_seq, nc),
        in_specs=[pl.BlockSpec((GLA_CHUNK, W_GLA_PAD), lambda n, t: (n * nc + t, 0)),
                  pl.BlockSpec(cm.shape, lambda n, t: (0, 0)),
                  pl.BlockSpec((GATE_RANK, W_QB), lambda n, t: (0, 0)),
                  pl.BlockSpec((1, W_QB), lambda n, t: (0, 0)),
                  pl.BlockSpec((1, DV_B), lambda n, t: (0, 0))],
        out_specs=[pl.BlockSpec((GLA_CHUNK, W_VB), lambda n, t: (n * nc + t, 0)),
                   pl.BlockSpec((1, N_HEADS_B, DK_B, DV_B), lambda n, t: (n, 0, 0, 0))],
        out_shape=[jax.ShapeDtypeStruct((n_seq * seq, W_VB), BF16),
                   jax.ShapeDtypeStruct((n_seq, N_HEADS_B, DK_B, DV_B), F32)],
        compiler_params=_params("arbitrary", "arbitrary"),
        name="gla_prompt",
    )(g, cm, wg, bg.reshape(1, W_QB), nw.reshape(1, DV_B))


def _gla_step_kernel(qc_ref, kc_ref, ab_ref, v_ref, rb_ref, wgt_ref, bgc_ref, nw_ref, s0_ref, o_ref, s_ref):
    ab = ab_ref[0]
    for h in range(N_HEADS_B):
        rows = slice(h * DK_B, (h + 1) * DK_B)
        pre = jnp.sum(wgt_ref[rows, :] * ab, axis=-1, keepdims=True) + bgc_ref[rows, :]
        a = jnp.exp(_log_sigmoid(pre) * (1.0 / GATE_TEMP))
        v = v_ref[0, :, h * DV_B:(h + 1) * DV_B]
        s_new = a * s0_ref[0, h] + kc_ref[0, rows, :] * v
        s_ref[0, h] = s_new
        o = jnp.sum((qc_ref[0, rows, :] * (DK_B ** -0.5)) * s_new, axis=0, keepdims=True)
        y = _rms(o, nw_ref[...]) * _silu(rb_ref[0, :, h * DV_B:(h + 1) * DV_B])
        o_ref[0, :, h * DV_B:(h + 1) * DV_B] = y.astype(o_ref.dtype)


def _gla_step(g, wg, bg, nw, s0):
    n = g.shape[0]
    qc = g[:, 0:W_QB].reshape(n, W_QB, 1)
    kc = g[:, W_QB:2 * W_QB].reshape(n, W_QB, 1)
    v = g[:, 2 * W_QB:2 * W_QB + W_VB].reshape(n, 1, W_VB)
    rb = g[:, 2 * W_QB + W_VB:2 * W_QB + 2 * W_VB].reshape(n, 1, W_VB)
    ab = g[:, 2 * W_QB + 2 * W_VB:W_GLA].reshape(n, 1, GATE_RANK)
    per_n3 = lambda i: (i, 0, 0)
    fixed = lambda i: (0, 0)
    o, s = pl.pallas_call(
        _gla_step_kernel,
        grid=(n,),
        in_specs=[pl.BlockSpec((1, W_QB, 1), per_n3), pl.BlockSpec((1, W_QB, 1), per_n3),
                  pl.BlockSpec((1, 1, GATE_RANK), per_n3),
                  pl.BlockSpec((1, 1, W_VB), per_n3), pl.BlockSpec((1, 1, W_VB), per_n3),
                  pl.BlockSpec((W_QB, GATE_RANK), fixed), pl.BlockSpec((W_QB, 1), fixed),
                  pl.BlockSpec((1, DV_B), fixed),
                  pl.BlockSpec((1, N_HEADS_B, DK_B, DV_B), lambda i: (i, 0, 0, 0))],
        out_specs=[pl.BlockSpec((1, 1, W_VB), per_n3),
                   pl.BlockSpec((1, N_HEADS_B, DK_B, DV_B), lambda i: (i, 0, 0, 0))],
        out_shape=[jax.ShapeDtypeStruct((n, 1, W_VB), BF16),
                   jax.ShapeDtypeStruct(s0.shape, F32)],
        compiler_params=_params("arbitrary"),
        name="gla_step",
    )(qc, kc, ab, v, rb, wg.T, bg.reshape(W_QB, 1), nw.reshape(1, DV_B), s0)
    return o.reshape(n, W_VB), s


PAGES_PER_STEP = 8


def _moba_scores_kernel(pt_ref, q_ref, *refs):
    k_refs, o_ref = refs[:PAGES_PER_STEP], refs[PAGES_PER_STEP]
    q = q_ref[0].astype(F32)
    ones = jnp.ones((HEAD_DIM, LANES), BF16)
    shape3 = (PAGE_SIZE, N_HEADS_A, LANES)
    diag = lax.broadcasted_iota(jnp.int32, shape3, 0) == lax.broadcasted_iota(jnp.int32, shape3, 2)
    for p, k_ref in enumerate(k_refs):
        prod = (k_ref[0] * q[None]).reshape(PAGE_SIZE * N_HEADS_A, HEAD_DIM)
        r = jnp.dot(prod.astype(BF16), ones, preferred_element_type=F32).reshape(shape3)
        o_ref[0, :, p * PAGE_SIZE:(p + 1) * PAGE_SIZE] = jnp.sum(jnp.where(diag, r, 0.0), axis=0)


def _moba_scores(cache_k, page_table, q):
    n, n_pages = page_table.shape
    steps = n_pages // PAGES_PER_STEP

    def page_spec(p):
        return pl.BlockSpec((1, PAGE_SIZE, N_HEADS_A, HEAD_DIM),
                            lambda i, g, pt: (pt[i, g * PAGES_PER_STEP + p], 0, 0, 0))

    return pl.pallas_call(
        _moba_scores_kernel,
        grid_spec=pltpu.PrefetchScalarGridSpec(
            num_scalar_prefetch=1,
            grid=(n, steps),
            in_specs=[pl.BlockSpec((1, N_HEADS_A, HEAD_DIM), lambda i, g, pt: (i, 0, 0))]
                     + [page_spec(p) for p in range(PAGES_PER_STEP)],
            out_specs=pl.BlockSpec((1, N_HEADS_A, PAGES_PER_STEP * PAGE_SIZE), lambda i, g, pt: (i, 0, g)),
        ),
        out_shape=jax.ShapeDtypeStruct((n, N_HEADS_A, n_pages * PAGE_SIZE), F32),
        compiler_params=_params("arbitrary", "arbitrary"),
        name="moba_scores",
    )(page_table, q.reshape(n, N_HEADS_A, HEAD_DIM), *([cache_k] * PAGES_PER_STEP))


def _moba_pick_kernel(s_ref, o_ref, *, nb):
    rows = s_ref.shape[0]
    lane = lax.broadcasted_iota(jnp.int32, (rows, LANES), 1)
    gate = jnp.zeros((rows, LANES), F32)
    for j in range(nb):
        gj = jnp.sum(s_ref[:, j * MOBA_BLOCK:(j + 1) * MOBA_BLOCK], axis=-1, keepdims=True) * (1.0 / MOBA_BLOCK)
        gate = jnp.where(lane == j, gj, gate)
    valid = lane < nb
    gm = jnp.where(valid, gate, NEG_INF)
    rank = jnp.zeros((rows, LANES), F32)
    for c in range(nb):
        gc = gm[:, c:c + 1]
        beats = (gc > gm) | ((gc == gm) & (lane > c))
        rank = rank + beats.astype(F32)
    out = jnp.zeros((rows, LANES), jnp.int32)
    for t in range(MOBA_TOPK):
        pick = jnp.sum(jnp.where(valid & (rank == float(t)), lane, 0), axis=-1, keepdims=True)
        out = jnp.where(lane == t, pick, out)
    o_ref[...] = out


def _moba_pick(scores, nb):
    rows = scores.shape[0]
    return pl.pallas_call(
        functools.partial(_moba_pick_kernel, nb=nb),
        grid=(1,),
        in_specs=[pl.BlockSpec(scores.shape, lambda i: (0, 0))],
        out_specs=pl.BlockSpec((rows, LANES), lambda i: (0, 0)),
        out_shape=jax.ShapeDtypeStruct((rows, LANES), jnp.int32),
        compiler_params=_params("arbitrary"),
        name="moba_pick",
    )(scores)


def _moba_gather_kernel(pt_ref, pick_ref, q_ref, kn_ref, vn_ref, e_ref, *refs):
    n_sel = MOBA_TOPK
    s_refs = refs[:n_sel]
    v_refs = refs[n_sel:n_sel + 2 * n_sel]
    o_ref = refs[n_sel + 2 * n_sel]
    s_self = jnp.sum(q_ref[0, 0].astype(F32) * kn_ref[0, 0].astype(BF16).astype(F32), axis=-1, keepdims=True)
    s = [r[0, 0, 0] for r in s_refs]
    m = s_self
    for st in s:
        m = jnp.maximum(m, jnp.max(st, axis=-1, keepdims=True))
    p_self = jnp.exp(s_self - m)
    l = p_self
    acc = p_self * vn_ref[0, 0].astype(BF16).astype(F32)
    spread = e_ref[0]
    for t, st in enumerate(s):
        p = jnp.exp(st - m)
        l = l + jnp.sum(p, axis=-1, keepdims=True)
        pb = jnp.broadcast_to(p, (8, MOBA_BLOCK)).astype(BF16)
        for half in range(2):
            rows = jnp.dot(pb[:, half * PAGE_SIZE:(half + 1) * PAGE_SIZE], spread, preferred_element_type=F32)
            vpage = v_refs[2 * t + half][0].reshape(PAGE_SIZE * N_HEADS_A, HEAD_DIM).astype(BF16)
            acc = acc + jnp.dot(rows.astype(BF16), vpage, preferred_element_type=F32)[0:1]
    o_ref[0, 0] = (acc * (1.0 / l)).astype(o_ref.dtype)


def _moba_gather(cache_v, page_table, picks, scores, q, k_new, v_new, nb):
    n = page_table.shape[0]
    pages_per_block = MOBA_BLOCK // PAGE_SIZE
    assert pages_per_block == 2
    scores5 = scores.reshape(n, N_HEADS_A, nb, 1, MOBA_BLOCK)
    per_nh = lambda i, h, pt, pk: (i, h, 0)

    def score_spec(t):
        return pl.BlockSpec((1, 1, 1, 1, MOBA_BLOCK),
                            lambda i, h, pt, pk: (i, h, pk[i * N_HEADS_A + h, t], 0, 0))

    def v_spec(t, half):
        return pl.BlockSpec((1, PAGE_SIZE, N_HEADS_A, HEAD_DIM),
                            lambda i, h, pt, pk: (pt[i, pk[i * N_HEADS_A + h, t] * pages_per_block + half], 0, 0, 0))

    tok = np.arange(PAGE_SIZE)
    spread = np.zeros((N_HEADS_A, PAGE_SIZE, PAGE_SIZE * N_HEADS_A), np.float32)
    for h in range(N_HEADS_A):
        spread[h, tok, tok * N_HEADS_A + h] = 1.0

    return pl.pallas_call(
        _moba_gather_kernel,
        grid_spec=pltpu.PrefetchScalarGridSpec(
            num_scalar_prefetch=2,
            grid=(n, N_HEADS_A),
            in_specs=[pl.BlockSpec((1, 1, 1, HEAD_DIM), lambda i, h, pt, pk: (i, h, 0, 0))] * 3
                     + [pl.BlockSpec((1, PAGE_SIZE, PAGE_SIZE * N_HEADS_A), lambda i, h, pt, pk: (h, 0, 0))]
                     + [score_spec(t) for t in range(MOBA_TOPK)]
                     + [v_spec(t, half) for t in range(MOBA_TOPK) for half in range(2)],
            out_specs=pl.BlockSpec((1, 1, 1, HEAD_DIM), lambda i, h, pt, pk: (i, h, 0, 0)),
        ),
        out_shape=jax.ShapeDtypeStruct((n, N_HEADS_A, 1, HEAD_DIM), BF16),
        compiler_params=_params("arbitrary", "arbitrary"),
        name="moba_gather",
    )(page_table, picks,
      q.reshape(n, N_HEADS_A, 1, HEAD_DIM), k_new.reshape(n, N_HEADS_A, 1, HEAD_DIM),
      v_new.reshape(n, N_HEADS_A, 1, HEAD_DIM), jnp.asarray(spread, BF16),
      *([scores5] * MOBA_TOPK), *([cache_v] * (2 * MOBA_TOPK)))


def _out_proj_kernel(oa_ref, ob_ref, w_ref, x_ref, g_ref, o_ref):
    mix = jnp.dot(oa_ref[...], w_ref[0:W_A, :], preferred_element_type=F32)
    mix = mix + jnp.dot(ob_ref[...], w_ref[W_A:W_A + W_VB, :], preferred_element_type=F32)
    o_ref[...] = x_ref[...] + g_ref[...] * mix


def _out_proj(oa, ob, w, x, gate, tm, tn, tiles_per_group):
    m, d = x.shape
    _, r, _ = gate.shape
    return pl.pallas_call(
        _out_proj_kernel,
        grid=(m // tm, d // tn),
        in_specs=[pl.BlockSpec((tm, W_A), lambda i, j: (i, 0)),
                  pl.BlockSpec((tm, W_VB), lambda i, j: (i, 0)),
                  pl.BlockSpec((W_A + W_VB, tn), lambda i, j: (0, j)),
                  pl.BlockSpec((tm, tn), lambda i, j: (i, j)),
                  pl.BlockSpec((None, r, tn), lambda i, j: (i // tiles_per_group, 0, j))],
        out_specs=pl.BlockSpec((tm, tn), lambda i, j: (i, j)),
        out_shape=jax.ShapeDtypeStruct((m, d), F32),
        compiler_params=_params("arbitrary", "arbitrary"),
        name="out_proj",
    )(oa, ob, w, x, gate)


def _mlp_kernel(x_ref, sc_ref, sh_ref, g_ref, nw_ref, wu_ref, wd_ref, o_ref, h_scr, acc_scr):
    f = pl.program_id(1)

    @pl.when(f == 0)
    def _first():
        y = _rms(x_ref[...], nw_ref[...])
        h_scr[...] = (y * (1.0 + sc_ref[...]) + sh_ref[...]).astype(BF16)
        acc_scr[...] = jnp.zeros(acc_scr.shape, F32)

    u = jnp.dot(h_scr[...], wu_ref[...], preferred_element_type=F32)
    r = jnp.maximum(u, 0.0)
    acc_scr[...] += jnp.dot((r * r).astype(BF16), wd_ref[...], preferred_element_type=F32)

    @pl.when(f == pl.num_programs(1) - 1)
    def _last():
        o_ref[...] = x_ref[...] + g_ref[...] * acc_scr[...]


def _mlp(x, sc, sh, gate, nw, wu, wd, tm, tf, tiles_per_group):
    m, d = x.shape
    ff = wu.shape[1]
    return pl.pallas_call(
        _mlp_kernel,
        grid=(m // tm, ff // tf),
        in_specs=[pl.BlockSpec((tm, d), lambda i, f: (i, 0)),
                  _mod_spec(sc, tiles_per_group), _mod_spec(sh, tiles_per_group), _mod_spec(gate, tiles_per_group),
                  pl.BlockSpec((1, d), lambda i, f: (0, 0)),
                  pl.BlockSpec((d, tf), lambda i, f: (0, f)),
                  pl.BlockSpec((tf, d), lambda i, f: (f, 0))],
        out_specs=pl.BlockSpec((tm, d), lambda i, f: (i, 0)),
        out_shape=jax.ShapeDtypeStruct((m, d), F32),
        scratch_shapes=[pltpu.VMEM((tm, d), BF16), pltpu.VMEM((tm, d), F32)],
        compiler_params=_params("arbitrary", "arbitrary"),
        name="mlp",
    )(x, sc, sh, gate, nw.reshape(1, d), wu, wd)


def _layer_dense_in(x2, mods, norm_w, w_qk, w_v, w_gla, tm, tiles_per_group):
    h = _norm_mod(x2, mods["sc1"], mods["sh1"], norm_w, tm, tiles_per_group)
    qk = _matmul(h, w_qk, tm, 512)
    v = _matmul(h, w_v, tm, 512)
    g = _matmul(h, w_gla, tm, W_GLA_PAD // 5)
    return qk, v, g


def kernel(x_prompt, x_sample, c_prompt, c_sample, cache_k, cache_v, state_gla, page_table, w_ada, b_ada, norm_mix_w, w_in, q_norm_w, k_norm_w, w_gate_up, b_gate, gla_norm_w, w_out, norm_ffn_w, w_up, w_down):
    n_p, s_p, d = x_prompt.shape
    n_s, s_s, _ = x_sample.shape
    depth = w_ada.shape[0]
    assert depth == 1 and s_s == 1 and d == D_MODEL
    n_pages = page_table.shape[1]
    past_len = n_pages * PAGE_SIZE
    nb_past = past_len // MOBA_BLOCK
    assert past_len % MOBA_BLOCK == 0 and nb_past >= MOBA_TOPK
    n_phys = cache_k.shape[1]
    l = 0
    tm_p = 512

    rows = n_p + n_s
    rows_pad = -(-rows // 8) * 8
    c_all = jnp.concatenate([c_prompt, c_sample, jnp.zeros((rows_pad - rows, d), F32)], axis=0)
    ada = _ada(c_all, w_ada[l], b_ada[l])
    names = ("sh1", "sc1", "g1", "sh2", "sc2", "g2")
    mods_p = {nm: ada[0:n_p, i * d:(i + 1) * d].reshape(n_p, 1, d) for i, nm in enumerate(names)}
    mods_s = {nm: ada[n_p:rows, i * d:(i + 1) * d].reshape(1, n_s, d) for i, nm in enumerate(names)}

    w_in_l = w_in[l]
    w_qk = w_in_l[:, 0:2 * W_A].astype(BF16)
    w_v = w_in_l[:, 2 * W_A:3 * W_A].astype(BF16)
    w_gla = jnp.pad(w_in_l[:, 3 * W_A:], ((0, 0), (0, W_GLA_PAD - W_GLA))).astype(BF16)
    w_out_b = w_out[l].astype(BF16)
    w_up_b = w_up[l].astype(BF16)
    w_down_b = w_down[l].astype(BF16)
    scale = HEAD_DIM ** -0.5

    xp = x_prompt.reshape(n_p * s_p, d)
    tpg = s_p // tm_p
    qk, v_p, g = _layer_dense_in(xp, mods_p, norm_mix_w[l], w_qk, w_v, w_gla, tm_p, tpg)
    cos_p, sin_p = _rope_tables(jnp.arange(s_p, dtype=jnp.int32))
    q_p = _qk_prep(qk, 0, cos_p, sin_p, q_norm_w[l], tm_p, tpg, scale, BF16)
    k_p = _qk_prep(qk, 1, cos_p, sin_p, k_norm_w[l], tm_p, tpg, 1.0, F32)
    oa_p = _moba_prompt(q_p, k_p, v_p, n_p, s_p)
    ob_p, gla_p = _gla_prompt(g, w_gate_up[l], b_gate[l], gla_norm_w[l], n_p, s_p)
    x1_p = _out_proj(oa_p, ob_p, w_out_b, xp, mods_p["g1"], tm_p, 512, tpg)
    y_p = _mlp(x1_p, mods_p["sc2"], mods_p["sh2"], mods_p["g2"], norm_ffn_w[l], w_up_b, w_down_b, tm_p, 512, tpg)

    xs = x_sample.reshape(n_s, d)
    qk, v_s, g = _layer_dense_in(xs, mods_s, norm_mix_w[l], w_qk, w_v, w_gla, n_s, 1)
    cos_s, sin_s = _rope_tables(jnp.full((n_s,), past_len, jnp.int32))
    q_s = _qk_prep(qk, 0, cos_s, sin_s, q_norm_w[l], n_s, 1, scale, BF16)
    k_s = _qk_prep(qk, 1, cos_s, sin_s, k_norm_w[l], n_s, 1, 1.0, F32)
    scores = _moba_scores(cache_k[l], page_table, q_s)
    picks = _moba_pick(scores.reshape(n_s * N_HEADS_A, past_len), nb_past)
    oa_s = _moba_gather(cache_v[l], page_table, picks, scores, q_s, k_s, v_s, nb_past).reshape(n_s, W_A)
    ob_s, gla_s = _gla_step(g, w_gate_up[l], b_gate[l], gla_norm_w[l], state_gla[l])
    x1_s = _out_proj(oa_s, ob_s, w_out_b, xs, mods_s["g1"], n_s, 512, 1)
    y_s = _mlp(x1_s, mods_s["sc2"], mods_s["sh2"], mods_s["g2"], norm_ffn_w[l], w_up_b, w_down_b, n_s, 512, 1)

    return (y_p.reshape(n_p, s_p, d), y_s.reshape(n_s, s_s, d),
            k_p.reshape(1, n_p, s_p, N_HEADS_A, HEAD_DIM), v_p.reshape(1, n_p, s_p, N_HEADS_A, HEAD_DIM),
            gla_p[None],
            k_s.reshape(1, n_s, s_s, N_HEADS_A, HEAD_DIM), v_s.reshape(1, n_s, s_s, N_HEADS_A, HEAD_DIM),
            gla_s[None])
```

```python
import functools

import numpy as np
import jax
import jax.numpy as jnp
from jax import lax
from jax.experimental import pallas as pl
from jax.experimental.pallas import tpu as pltpu

D_MODEL = 2048
PAGE_SIZE = 128
HEAD_DIM = 128
N_HEADS_A = 8
N_HEADS_B = 4
DK_B = 128
DV_B = 256
GATE_RANK = 16
GATE_TEMP = 16.0
MOBA_BLOCK = 256
MOBA_TOPK = 3
GLA_CHUNK = 128
D_FF = 4 * D_MODEL
ROPE_THETA = 10000.0
EPS = 1e-6

W_A = N_HEADS_A * HEAD_DIM
W_QB = N_HEADS_B * DK_B
W_VB = N_HEADS_B * DV_B
W_GLA = 2 * W_QB + 2 * W_VB
LANES = 128

F32 = jnp.float32
BF16 = jnp.bfloat16
NEG_INF = float("-inf")
VMEM_LIMIT = 56 * 1024 * 1024

TM_ROWWISE = 512
TM_DENSE = 1024
TN_DENSE = 1024
TF_MLP = 512
MOBA_HEADS_PER_STEP = 2
MOBA_QBLOCK_GROUP = 4
ONES_ROWS = 16
LOG2E = 1.4426950408889634

_NT = (((1,), (1,)), ((), ()))


def _params(*sem):
    return pltpu.CompilerParams(dimension_semantics=sem, vmem_limit_bytes=VMEM_LIMIT)


def _rms(x, w):
    return x * lax.rsqrt(jnp.mean(x * x, axis=-1, keepdims=True) + EPS) * w


def _log_sigmoid(x):
    return jnp.minimum(x, 0.0) - jnp.log(1.0 + jnp.exp(-jnp.abs(x)))


def _silu(x):
    return x / (1.0 + jnp.exp(-x))


def _ada_kernel(c_ref, w_ref, b_ref, o_ref):
    a = _silu(c_ref[...]).astype(BF16)
    o_ref[...] = jnp.dot(a, w_ref[...].astype(BF16), preferred_element_type=F32) + b_ref[...]


def _ada(c, w, b, tn=512):
    m, d = c.shape
    n = w.shape[1]
    return pl.pallas_call(
        _ada_kernel,
        grid=(n // tn,),
        in_specs=[pl.BlockSpec((m, d), lambda j: (0, 0)),
                  pl.BlockSpec((d, tn), lambda j: (0, j)),
                  pl.BlockSpec((1, tn), lambda j: (0, j))],
        out_specs=pl.BlockSpec((m, tn), lambda j: (0, j)),
        out_shape=jax.ShapeDtypeStruct((m, n), F32),
        compiler_params=_params("arbitrary"),
        name="ada_proj",
    )(c, w, b.reshape(1, n))


def _mod_spec(mod, tiles_per_group):
    _, r, w = mod.shape
    return pl.BlockSpec((None, r, w), lambda i, *_: (i // tiles_per_group, 0, 0))


def _norm_mod_kernel(x_ref, sc_ref, sh_ref, w_ref, o_ref):
    y = _rms(x_ref[...], w_ref[...])
    o_ref[...] = (y * (1.0 + sc_ref[...]) + sh_ref[...]).astype(o_ref.dtype)


def _norm_mod(x, sc, sh, w, tm, tiles_per_group):
    m, d = x.shape
    return pl.pallas_call(
        _norm_mod_kernel,
        grid=(m // tm,),
        in_specs=[pl.BlockSpec((tm, d), lambda i: (i, 0)),
                  _mod_spec(sc, tiles_per_group), _mod_spec(sh, tiles_per_group),
                  pl.BlockSpec((1, d), lambda i: (0, 0))],
        out_specs=pl.BlockSpec((tm, d), lambda i: (i, 0)),
        out_shape=jax.ShapeDtypeStruct((m, d), BF16),
        compiler_params=_params("arbitrary"),
        name="norm_mod",
    )(x, sc, sh, w.reshape(1, d))


def _proj_kernel(a_ref, w_ref, o_ref, wb_scr):
    @pl.when(pl.program_id(1) == 0)
    def _cast():
        wb_scr[...] = w_ref[...].astype(BF16)

    o_ref[...] = jnp.dot(a_ref[...], wb_scr[...], preferred_element_type=F32).astype(o_ref.dtype)


def _proj(a, w, col_block0, n_out, tm, tn, out_dtype=F32):
    m, k = a.shape
    return pl.pallas_call(
        _proj_kernel,
        grid=(n_out // tn, m // tm),
        in_specs=[pl.BlockSpec((tm, k), lambda j, i: (i, 0)),
                  pl.BlockSpec((k, tn), lambda j, i: (0, j + col_block0))],
        out_specs=pl.BlockSpec((tm, tn), lambda j, i: (i, j)),
        out_shape=jax.ShapeDtypeStruct((m, n_out), out_dtype),
        scratch_shapes=[pltpu.VMEM((k, tn), BF16)],
        compiler_params=_params("arbitrary", "arbitrary"),
        name="proj",
    )(a, w)


def _gate_weight_kernel(a_ref, b_ref, o_ref):
    o_ref[...] = jnp.dot(a_ref[...], b_ref[...], preferred_element_type=F32, precision=lax.Precision.HIGHEST)


def _gate_weight(w_ab, w_gate_up):
    d, r = w_ab.shape
    n = w_gate_up.shape[1]
    return pl.pallas_call(
        _gate_weight_kernel,
        grid=(1,),
        in_specs=[pl.BlockSpec((d, r), lambda i: (0, 0)), pl.BlockSpec((r, n), lambda i: (0, 0))],
        out_specs=pl.BlockSpec((d, n), lambda i: (0, 0)),
        out_shape=jax.ShapeDtypeStruct((d, n), F32),
        compiler_params=_params("arbitrary"),
        name="gate_weight",
    )(w_ab, w_gate_up)


def _qk_prep_kernel(x_ref, cos_ref, sin_ref, w_ref, o_ref, *, scale):
    cos = cos_ref[...]
    sin = sin_ref[...]
    w = w_ref[...]
    for h in range(N_HEADS_A):
        sl = slice(h * HEAD_DIM, (h + 1) * HEAD_DIM)
        y = _rms(x_ref[:, sl], w)
        r = y * cos + pltpu.roll(y, HEAD_DIM // 2, 1) * sin
        if scale != 1.0:
            r = r * scale
        o_ref[:, sl] = r.astype(o_ref.dtype)


def _qk_prep(qk, col_block, cos, sin, w, tm, pos_tiles, scale, out_dtype):
    m = qk.shape[0]
    return pl.pallas_call(
        functools.partial(_qk_prep_kernel, scale=scale),
        grid=(m // tm,),
        in_specs=[pl.BlockSpec((tm, W_A), lambda i: (i, col_block)),
                  pl.BlockSpec((tm, HEAD_DIM), lambda i: (i % pos_tiles, 0)),
                  pl.BlockSpec((tm, HEAD_DIM), lambda i: (i % pos_tiles, 0)),
                  pl.BlockSpec((1, HEAD_DIM), lambda i: (0, 0))],
        out_specs=pl.BlockSpec((tm, W_A), lambda i: (i, 0)),
        out_shape=jax.ShapeDtypeStruct((m, W_A), out_dtype),
        compiler_params=_params("arbitrary"),
        name="qk_prep",
    )(qk, cos, sin, w.reshape(1, HEAD_DIM))


def _rope_tables(pos):
    half = HEAD_DIM // 2
    inv_freq = ROPE_THETA ** (-jnp.arange(half, dtype=F32) / half)
    ang = pos.astype(F32)[:, None] * inv_freq[None, :]
    cos, sin = jnp.cos(ang), jnp.sin(ang)
    return jnp.concatenate([cos, cos], axis=-1), jnp.concatenate([-sin, sin], axis=-1)


def _topk_select(g, valid, axis):
    n = g.shape[axis]
    idx = lax.broadcasted_iota(jnp.int32, g.shape, axis)
    gm = jnp.where(valid, g, NEG_INF)
    rank = jnp.zeros(g.shape, F32)
    for c in range(n):
        gc = lax.slice_in_dim(gm, c, c + 1, axis=axis)
        beats = (gc > gm) | ((gc == gm) & (idx > c))
        rank = rank + beats.astype(F32)
    return valid & (rank < MOBA_TOPK)


def _moba_prompt_kernel(q_ref, k_ref, v_ref, o_ref, kb_scr, vt_scr, km_scr, sel_scr, s_scr, *, nb, hp, group):
    qb = pl.program_id(2)
    blk = MOBA_BLOCK
    heads = [slice(hh * HEAD_DIM, (hh + 1) * HEAD_DIM) for hh in range(hp)]

    @pl.when(qb == 0)
    def _prep():
        def per_block(j, carry):
            rows = pl.ds(pl.multiple_of(j * blk, blk), blk)
            for hh, cols in enumerate(heads):
                kj = k_ref[rows, cols]
                kb_scr[hh, j] = kj.astype(BF16)
                km_scr[hh, pl.ds(j, 1), :] = jnp.sum(kj, axis=0, keepdims=True) * (1.0 / blk)
                vt_scr[hh, j, 0:HEAD_DIM, :] = v_ref[rows, cols].T.astype(BF16)
                vt_scr[hh, j, HEAD_DIM:HEAD_DIM + ONES_ROWS, :] = jnp.ones((ONES_ROWS, blk), BF16)
            return carry

        lax.fori_loop(0, nb, per_block, 0)

    causal = (lax.broadcasted_iota(jnp.int32, (blk, blk), 0) <= lax.broadcasted_iota(jnp.int32, (blk, blk), 1))
    for hh, cols in enumerate(heads):
        gate = lax.dot_general(km_scr[hh].astype(BF16), q_ref[:, cols], _NT, preferred_element_type=F32)
        cand = lax.broadcasted_iota(jnp.int32, gate.shape, 0)
        sel_scr[hh] = (_topk_select(gate, cand < qb, 0) | (cand == qb)).astype(F32)

    def attend(n_keys):
        first_own = n_keys - group
        for hh, cols in enumerate(heads):
            q = q_ref[:, cols]
            m = jnp.full((1, blk), NEG_INF, F32)
            for j in range(n_keys):
                s = lax.dot_general(kb_scr[hh, j], q, _NT, preferred_element_type=F32)
                s = jnp.where(sel_scr[hh, j:j + 1, :] > 0.0, s, NEG_INF)
                s_scr[hh, j] = s
                bmax = jnp.max(s, axis=0, keepdims=True)
                if j >= first_own:
                    bmax = jnp.where(qb == j, NEG_INF, bmax)
                m = jnp.maximum(m, bmax)
            s_own = jnp.where(causal, s_scr[hh, qb], NEG_INF)
            s_scr[hh, qb] = s_own
            m = jnp.maximum(m, jnp.max(s_own, axis=0, keepdims=True))
            acc = jnp.zeros((HEAD_DIM + ONES_ROWS, blk), F32)
            for j in range(n_keys):
                p = jnp.exp2(s_scr[hh, j] - m)
                acc = acc + jnp.dot(vt_scr[hh, j], p.astype(BF16), preferred_element_type=F32)
            l = acc[HEAD_DIM:HEAD_DIM + 1, :]
            o_ref[:, cols] = (acc[0:HEAD_DIM, :] * (1.0 / l)).T.astype(o_ref.dtype)

    for g in range(nb // group):
        pl.when(qb // group == g)(functools.partial(attend, (g + 1) * group))


def _moba_prompt(q, k, v, n_seq, seq):
    nb = seq // MOBA_BLOCK
    hp = MOBA_HEADS_PER_STEP
    group = min(MOBA_QBLOCK_GROUP, nb)
    assert nb % group == 0
    w = hp * HEAD_DIM
    return pl.pallas_call(
        functools.partial(_moba_prompt_kernel, nb=nb, hp=hp, group=group),
        grid=(n_seq, N_HEADS_A // hp, nb),
        in_specs=[pl.BlockSpec((MOBA_BLOCK, w), lambda n, h, b: (n * nb + b, h)),
                  pl.BlockSpec((seq, w), lambda n, h, b: (n, h)),
                  pl.BlockSpec((seq, w), lambda n, h, b: (n, h))],
        out_specs=pl.BlockSpec((MOBA_BLOCK, w), lambda n, h, b: (n * nb + b, h)),
        out_shape=jax.ShapeDtypeStruct((n_seq * seq, W_A), BF16),
        scratch_shapes=[pltpu.VMEM((hp, nb, MOBA_BLOCK, HEAD_DIM), BF16),
                        pltpu.VMEM((hp, nb, HEAD_DIM + ONES_ROWS, MOBA_BLOCK), BF16),
                        pltpu.VMEM((hp, nb, HEAD_DIM), F32),
                        pltpu.VMEM((hp, nb, MOBA_BLOCK), F32),
                        pltpu.VMEM((hp, nb, MOBA_BLOCK, MOBA_BLOCK), F32)],
        compiler_params=_params("arbitrary", "arbitrary", "arbitrary"),
        name="moba_prompt",
    )(q, k, v)


def _gla_levels():
    m, out = 1, []
    while m < GLA_CHUNK:
        out.append(m)
        m *= 2
    return out


def _gla_cum_matrices():
    c = GLA_CHUNK
    t = np.arange(c)
    tri = (t[None, :] <= t[:, None]).astype(np.float32)
    mats = [tri]
    for m in _gla_levels():
        mats.append(tri[(t // (2 * m)) * (2 * m) + m - 1])
    return np.concatenate(mats, axis=0)


def _split_bf16(x):
    hi = x.astype(BF16)
    return hi, (x - hi.astype(F32)).astype(BF16)


def _gla_prompt_kernel(g_ref, pre_ref, cm_ref, bg_ref, nw_ref, o_ref, s_ref):
    c = GLA_CHUNK

    @pl.when(pl.program_id(1) == 0)
    def _init():
        s_ref[...] = jnp.zeros(s_ref.shape, F32)

    tok = lax.broadcasted_iota(jnp.int32, (c, c), 0)
    col = lax.broadcasted_iota(jnp.int32, (c, c), 1)
    tok_d = lax.broadcasted_iota(jnp.int32, (c, DK_B), 0)
    cm = cm_ref[...]
    for h in range(N_HEADS_B):
        dk = slice(h * DK_B, (h + 1) * DK_B)
        la = _log_sigmoid(pre_ref[:, dk] + bg_ref[:, dk]) * (LOG2E / GATE_TEMP)
        cums = sum(jnp.dot(cm, piece, preferred_element_type=F32) for piece in _split_bf16(la))
        b = cums[0:c]
        q = g_ref[:, dk] * (DK_B ** -0.5)
        k = g_ref[:, W_QB + h * DK_B:W_QB + (h + 1) * DK_B]
        v = g_ref[:, 2 * W_QB + h * DV_B:2 * W_QB + (h + 1) * DV_B]
        rb = g_ref[:, 2 * W_QB + W_VB + h * DV_B:2 * W_QB + W_VB + (h + 1) * DV_B]
        vb = v.astype(BF16)

        attn = lax.dot_general(q.astype(BF16), k.astype(BF16), _NT, preferred_element_type=F32)
        attn = jnp.where(col == tok, attn, 0.0)
        for lvl, m in enumerate(_gla_levels()):
            r = cums[(lvl + 1) * c:(lvl + 2) * c]
            upper = ((tok_d // m) % 2) == 1
            x = b - r
            e = jnp.exp2(jnp.where(upper, x, -x))
            e_up = jnp.where(upper, e, 0.0)
            a_m = lax.dot_general((q * e_up).astype(BF16), (k * (e - e_up)).astype(BF16), _NT,
                                  preferred_element_type=F32)
            attn = attn + jnp.where((tok // (2 * m)) == (col // (2 * m)), a_m, 0.0)

        state = s_ref[0, h]
        o = jnp.dot(attn.astype(BF16), vb, preferred_element_type=F32)
        o = o + jnp.dot((q * jnp.exp2(b)).astype(BF16), state.astype(BF16), preferred_element_type=F32)
        b_last = b[c - 1:c, :]
        kd = k * jnp.exp2(b_last - b)
        kv = jnp.dot(kd.T.astype(BF16), vb, preferred_element_type=F32)
        dcol = b.T[:, c - 1:c]
        s_ref[0, h] = jnp.exp2(dcol) * state + kv

        y = _rms(o, nw_ref[...]) * _silu(rb)
        o_ref[:, h * DV_B:(h + 1) * DV_B] = y.astype(o_ref.dtype)


def _gla_prompt(g, pre, bg, nw, n_seq, seq):
    nc = seq // GLA_CHUNK
    cm = jnp.asarray(_gla_cum_matrices(), BF16)
    return pl.pallas_call(
        _gla_prompt_kernel,
        grid=(n_seq, nc),
        in_specs=[pl.BlockSpec((GLA_CHUNK, W_GLA), lambda n, t: (n * nc + t, 0)),
                  pl.BlockSpec((GLA_CHUNK, W_QB), lambda n, t: (n * nc + t, 0)),
                  pl.BlockSpec(cm.shape, lambda n, t: (0, 0)),
                  pl.BlockSpec((1, W_QB), lambda n, t: (0, 0)),
                  pl.BlockSpec((1, DV_B), lambda n, t: (0, 0))],
        out_specs=[pl.BlockSpec((GLA_CHUNK, W_VB), lambda n, t: (n * nc + t, 0)),
                   pl.BlockSpec((1, N_HEADS_B, DK_B, DV_B), lambda n, t: (n, 0, 0, 0))],
        out_shape=[jax.ShapeDtypeStruct((n_seq * seq, W_VB), BF16),
                   jax.ShapeDtypeStruct((n_seq, N_HEADS_B, DK_B, DV_B), F32)],
        compiler_params=_params("arbitrary", "arbitrary"),
        name="gla_prompt",
    )(g, pre, cm, bg.reshape(1, W_QB), nw.reshape(1, DV_B))


def _gla_step_kernel(qc_ref, kc_ref, prec_ref, v_ref, rb_ref, bgc_ref, nw_ref, s0_ref, o_ref, s_ref):
    for h in range(N_HEADS_B):
        rows = slice(h * DK_B, (h + 1) * DK_B)
        a = jnp.exp(_log_sigmoid(prec_ref[0, rows, :] + bgc_ref[rows, :]) * (1.0 / GATE_TEMP))
        v = v_ref[0, :, h * DV_B:(h + 1) * DV_B]
        s_new = a * s0_ref[0, h] + kc_ref[0, rows, :] * v
        s_ref[0, h] = s_new
        o = jnp.sum((qc_ref[0, rows, :] * (DK_B ** -0.5)) * s_new, axis=0, keepdims=True)
        y = _rms(o, nw_ref[...]) * _silu(rb_ref[0, :, h * DV_B:(h + 1) * DV_B])
        o_ref[0, :, h * DV_B:(h + 1) * DV_B] = y.astype(o_ref.dtype)


def _gla_step(g, pre, bg, nw, s0):
    n = g.shape[0]
    qc = g[:, 0:W_QB].reshape(n, W_QB, 1)
    kc = g[:, W_QB:2 * W_QB].reshape(n, W_QB, 1)
    v = g[:, 2 * W_QB:2 * W_QB + W_VB].reshape(n, 1, W_VB)
    rb = g[:, 2 * W_QB + W_VB:W_GLA].reshape(n, 1, W_VB)
    per_n3 = lambda i: (i, 0, 0)
    fixed = lambda i: (0, 0)
    o, s = pl.pallas_call(
        _gla_step_kernel,
        grid=(n,),
        in_specs=[pl.BlockSpec((1, W_QB, 1), per_n3), pl.BlockSpec((1, W_QB, 1), per_n3),
                  pl.BlockSpec((1, W_QB, 1), per_n3),
                  pl.BlockSpec((1, 1, W_VB), per_n3), pl.BlockSpec((1, 1, W_VB), per_n3),
                  pl.BlockSpec((W_QB, 1), fixed), pl.BlockSpec((1, DV_B), fixed),
                  pl.BlockSpec((1, N_HEADS_B, DK_B, DV_B), lambda i: (i, 0, 0, 0))],
        out_specs=[pl.BlockSpec((1, 1, W_VB), per_n3),
                   pl.BlockSpec((1, N_HEADS_B, DK_B, DV_B), lambda i: (i, 0, 0, 0))],
        out_shape=[jax.ShapeDtypeStruct((n, 1, W_VB), BF16),
                   jax.ShapeDtypeStruct(s0.shape, F32)],
        compiler_params=_params("arbitrary"),
        name="gla_step",
    )(qc, kc, pre.reshape(n, W_QB, 1), v, rb, bg.reshape(W_QB, 1), nw.reshape(1, DV_B), s0)
    return o.reshape(n, W_VB), s


PAGES_PER_STEP = 8


def _moba_scores_kernel(pt_ref, q_ref, *refs):
    k_refs, o_ref = refs[:PAGES_PER_STEP], refs[PAGES_PER_STEP]
    q = q_ref[0].astype(F32)
    ones = jnp.ones((HEAD_DIM, LANES), BF16)
    shape3 = (PAGE_SIZE, N_HEADS_A, LANES)
    diag = lax.broadcasted_iota(jnp.int32, shape3, 0) == lax.broadcasted_iota(jnp.int32, shape3, 2)
    for p, k_ref in enumerate(k_refs):
        prod = (k_ref[0] * q[None]).reshape(PAGE_SIZE * N_HEADS_A, HEAD_DIM)
        r = jnp.dot(prod.astype(BF16), ones, preferred_element_type=F32).reshape(shape3)
        o_ref[0, :, p * PAGE_SIZE:(p + 1) * PAGE_SIZE] = jnp.sum(jnp.where(diag, r, 0.0), axis=0)


def _moba_scores(cache_k, page_table, q):
    n, n_pages = page_table.shape
    steps = n_pages // PAGES_PER_STEP

    def page_spec(p):
        return pl.BlockSpec((1, PAGE_SIZE, N_HEADS_A, HEAD_DIM),
                            lambda i, g, pt: (pt[i, g * PAGES_PER_STEP + p], 0, 0, 0))

    return pl.pallas_call(
        _moba_scores_kernel,
        grid_spec=pltpu.PrefetchScalarGridSpec(
            num_scalar_prefetch=1,
            grid=(n, steps),
            in_specs=[pl.BlockSpec((1, N_HEADS_A, HEAD_DIM), lambda i, g, pt: (i, 0, 0))]
                     + [page_spec(p) for p in range(PAGES_PER_STEP)],
            out_specs=pl.BlockSpec((1, N_HEADS_A, PAGES_PER_STEP * PAGE_SIZE), lambda i, g, pt: (i, 0, g)),
        ),
        out_shape=jax.ShapeDtypeStruct((n, N_HEADS_A, n_pages * PAGE_SIZE), F32),
        compiler_params=_params("arbitrary", "arbitrary"),
        name="moba_scores",
    )(page_table, q.reshape(n, N_HEADS_A, HEAD_DIM), *([cache_k] * PAGES_PER_STEP))


def _moba_pick_kernel(s_ref, o_ref, *, nb):
    rows = s_ref.shape[0]
    lane = lax.broadcasted_iota(jnp.int32, (rows, LANES), 1)
    gate = jnp.zeros((rows, LANES), F32)
    for j in range(nb):
        gj = jnp.sum(s_ref[:, j * MOBA_BLOCK:(j + 1) * MOBA_BLOCK], axis=-1, keepdims=True) * (1.0 / MOBA_BLOCK)
        gate = jnp.where(lane == j, gj, gate)
    valid = lane < nb
    gm = jnp.where(valid, gate, NEG_INF)
    rank = jnp.zeros((rows, LANES), F32)
    for c in range(nb):
        gc = gm[:, c:c + 1]
        beats = (gc > gm) | ((gc == gm) & (lane > c))
        rank = rank + beats.astype(F32)
    out = jnp.zeros((rows, LANES), jnp.int32)
    for t in range(MOBA_TOPK):
        pick = jnp.sum(jnp.where(valid & (rank == float(t)), lane, 0), axis=-1, keepdims=True)
        out = jnp.where(lane == t, pick, out)
    o_ref[...] = out


def _moba_pick(scores, nb):
    rows = scores.shape[0]
    return pl.pallas_call(
        functools.partial(_moba_pick_kernel, nb=nb),
        grid=(1,),
        in_specs=[pl.BlockSpec(scores.shape, lambda i: (0, 0))],
        out_specs=pl.BlockSpec((rows, LANES), lambda i: (0, 0)),
        out_shape=jax.ShapeDtypeStruct((rows, LANES), jnp.int32),
        compiler_params=_params("arbitrary"),
        name="moba_pick",
    )(scores)


def _moba_gather_kernel(pt_ref, pick_ref, q_ref, kn_ref, vn_ref, e_ref, s_ref, *refs):
    v_refs, o_ref = refs[:2 * MOBA_TOPK], refs[2 * MOBA_TOPK]
    row = pl.program_id(0) * N_HEADS_A + pl.program_id(1)
    s_self = jnp.sum(q_ref[0, 0].astype(F32) * kn_ref[0, 0].astype(BF16).astype(F32), axis=-1, keepdims=True)
    s = [s_ref[0, 0, pl.ds(pick_ref[row, t], 1), :] for t in range(MOBA_TOPK)]
    m = s_self
    for st in s:
        m = jnp.maximum(m, jnp.max(st, axis=-1, keepdims=True))
    p_self = jnp.exp(s_self - m)
    l = p_self
    acc = p_self * vn_ref[0, 0].astype(BF16).astype(F32)
    spread = e_ref[0]
    for t, st in enumerate(s):
        p = jnp.exp(st - m)
        l = l + jnp.sum(p, axis=-1, keepdims=True)
        pb = jnp.broadcast_to(p, (8, MOBA_BLOCK)).astype(BF16)
        for half in range(2):
            rows = jnp.dot(pb[:, half * PAGE_SIZE:(half + 1) * PAGE_SIZE], spread, preferred_element_type=F32)
            vpage = v_refs[2 * t + half][0].reshape(PAGE_SIZE * N_HEADS_A, HEAD_DIM).astype(BF16)
            acc = acc + jnp.dot(rows.astype(BF16), vpage, preferred_element_type=F32)[0:1]
    o_ref[0, 0] = (acc * (1.0 / l)).astype(o_ref.dtype)


def _moba_gather(cache_v, page_table, picks, scores, q, k_new, v_new, nb):
    n = page_table.shape[0]
    pages_per_block = MOBA_BLOCK // PAGE_SIZE
    assert pages_per_block == 2
    scores4 = scores.reshape(n, N_HEADS_A, nb, MOBA_BLOCK)

    def v_spec(t, half):
        return pl.BlockSpec((1, PAGE_SIZE, N_HEADS_A, HEAD_DIM),
                            lambda i, h, pt, pk: (pt[i, pk[i * N_HEADS_A + h, t] * pages_per_block + half], 0, 0, 0))

    tok = np.arange(PAGE_SIZE)
    spread = np.zeros((N_HEADS_A, PAGE_SIZE, PAGE_SIZE * N_HEADS_A), np.float32)
    for h in range(N_HEADS_A):
        spread[h, tok, tok * N_HEADS_A + h] = 1.0

    return pl.pallas_call(
        _moba_gather_kernel,
        grid_spec=pltpu.PrefetchScalarGridSpec(
            num_scalar_prefetch=2,
            grid=(n, N_HEADS_A),
            in_specs=[pl.BlockSpec((1, 1, 1, HEAD_DIM), lambda i, h, pt, pk: (i, h, 0, 0))] * 3
                     + [pl.BlockSpec((1, PAGE_SIZE, PAGE_SIZE * N_HEADS_A), lambda i, h, pt, pk: (h, 0, 0)),
                        pl.BlockSpec((1, 1, nb, MOBA_BLOCK), lambda i, h, pt, pk: (i, h, 0, 0))]
                     + [v_spec(t, half) for t in range(MOBA_TOPK) for half in range(2)],
            out_specs=pl.BlockSpec((1, 1, 1, HEAD_DIM), lambda i, h, pt, pk: (i, h, 0, 0)),
        ),
        out_shape=jax.ShapeDtypeStruct((n, N_HEADS_A, 1, HEAD_DIM), BF16),
        compiler_params=_params("arbitrary", "arbitrary"),
        name="moba_gather",
    )(page_table, picks,
      q.reshape(n, N_HEADS_A, 1, HEAD_DIM), k_new.reshape(n, N_HEADS_A, 1, HEAD_DIM),
      v_new.reshape(n, N_HEADS_A, 1, HEAD_DIM), jnp.asarray(spread, BF16), scores4,
      *([cache_v] * (2 * MOBA_TOPK)))


def _out_proj_kernel(oa_ref, ob_ref, w_ref, x_ref, g_ref, o_ref, wb_scr):
    @pl.when(pl.program_id(1) == 0)
    def _cast():
        wb_scr[...] = w_ref[...].astype(BF16)

    mix = jnp.dot(oa_ref[...], wb_scr[0:W_A, :], preferred_element_type=F32)
    mix = mix + jnp.dot(ob_ref[...], wb_scr[W_A:W_A + W_VB, :], preferred_element_type=F32)
    o_ref[...] = x_ref[...] + g_ref[...] * mix


def _out_proj(oa, ob, w, x, gate, tm, tn, tiles_per_group):
    m, d = x.shape
    _, r, _ = gate.shape
    return pl.pallas_call(
        _out_proj_kernel,
        grid=(d // tn, m // tm),
        in_specs=[pl.BlockSpec((tm, W_A), lambda j, i: (i, 0)),
                  pl.BlockSpec((tm, W_VB), lambda j, i: (i, 0)),
                  pl.BlockSpec((W_A + W_VB, tn), lambda j, i: (0, j)),
                  pl.BlockSpec((tm, tn), lambda j, i: (i, j)),
                  pl.BlockSpec((None, r, tn), lambda j, i: (i // tiles_per_group, 0, j))],
        out_specs=pl.BlockSpec((tm, tn), lambda j, i: (i, j)),
        out_shape=jax.ShapeDtypeStruct((m, d), F32),
        scratch_shapes=[pltpu.VMEM((W_A + W_VB, tn), BF16)],
        compiler_params=_params("arbitrary", "arbitrary"),
        name="out_proj",
    )(oa, ob, w, x, gate)


def _mlp_kernel(x_ref, sc_ref, sh_ref, g_ref, nw_ref, wu_ref, wd_ref, o_ref, h_scr):
    f = pl.program_id(1)
    d = o_ref.shape[1]
    tn = min(d, TN_DENSE)

    @pl.when(f == 0)
    def _first():
        y = _rms(x_ref[...], nw_ref[...])
        h_scr[...] = (y * (1.0 + sc_ref[...]) + sh_ref[...]).astype(BF16)
        o_ref[...] = jnp.zeros(o_ref.shape, F32)

    u = jnp.dot(h_scr[...], wu_ref[...], preferred_element_type=F32)
    r = jnp.maximum(u, 0.0)
    r2 = (r * r).astype(BF16)
    for c0 in range(0, d, tn):
        o_ref[:, c0:c0 + tn] += jnp.dot(r2, wd_ref[:, c0:c0 + tn], preferred_element_type=F32)

    @pl.when(f == pl.num_programs(1) - 1)
    def _last():
        o_ref[...] = x_ref[...] + g_ref[...] * o_ref[...]


def _mlp(x, sc, sh, gate, nw, wu, wd, tm, tf, tiles_per_group):
    m, d = x.shape
    ff = wu.shape[1]
    return pl.pallas_call(
        _mlp_kernel,
        grid=(m // tm, ff // tf),
        in_specs=[pl.BlockSpec((tm, d), lambda i, f: (i, 0)),
                  _mod_spec(sc, tiles_per_group), _mod_spec(sh, tiles_per_group), _mod_spec(gate, tiles_per_group),
                  pl.BlockSpec((1, d), lambda i, f: (0, 0)),
                  pl.BlockSpec((d, tf), lambda i, f: (0, f)),
                  pl.BlockSpec((tf, d), lambda i, f: (f, 0))],
        out_specs=pl.BlockSpec((tm, d), lambda i, f: (i, 0)),
        out_shape=jax.ShapeDtypeStruct((m, d), F32),
        scratch_shapes=[pltpu.VMEM((tm, d), BF16)],
        compiler_params=_params("arbitrary", "arbitrary"),
        name="mlp",
    )(x, sc, sh, gate, nw.reshape(1, d), wu, wd)


def _dense_in(x2, mods, norm_w, w_in, w_gate, tm_row, tm, tiles_per_group):
    h = _norm_mod(x2, mods["sc1"], mods["sh1"], norm_w, tm_row, tiles_per_group)
    tn = TN_DENSE
    qk = _proj(h, w_in, 0, 2 * W_A, tm, tn)
    v = _proj(h, w_in, 2 * W_A // tn, W_A, tm, tn)
    g = _proj(h, w_in, 3 * W_A // tn, W_GLA, tm, tn)
    pre = _proj(h, w_gate, 0, W_QB, tm, W_QB)
    return qk, v, g, pre


def kernel(x_prompt, x_sample, c_prompt, c_sample, cache_k, cache_v, state_gla, page_table, w_ada, b_ada, norm_mix_w, w_in, q_norm_w, k_norm_w, w_gate_up, b_gate, gla_norm_w, w_out, norm_ffn_w, w_up, w_down):
    n_p, s_p, d = x_prompt.shape
    n_s, s_s, _ = x_sample.shape
    depth = w_ada.shape[0]
    assert depth == 1 and s_s == 1 and d == D_MODEL
    n_pages = page_table.shape[1]
    past_len = n_pages * PAGE_SIZE
    nb_past = past_len // MOBA_BLOCK
    assert past_len % MOBA_BLOCK == 0 and nb_past >= MOBA_TOPK
    l = 0
    tm_row = min(TM_ROWWISE, s_p)
    tm = min(TM_DENSE, s_p)

    rows = n_p + n_s
    rows_pad = -(-rows // 8) * 8
    c_all = jnp.concatenate([c_prompt, c_sample, jnp.zeros((rows_pad - rows, d), F32)], axis=0)
    ada = _ada(c_all, w_ada[l], b_ada[l])
    names = ("sh1", "sc1", "g1", "sh2", "sc2", "g2")
    mods_p = {nm: ada[0:n_p, i * d:(i + 1) * d].reshape(n_p, 1, d) for i, nm in enumerate(names)}
    mods_s = {nm: ada[n_p:rows, i * d:(i + 1) * d].reshape(1, n_s, d) for i, nm in enumerate(names)}

    w_in_l = w_in[l]
    w_gate = _gate_weight(w_in_l[:, 3 * W_A + W_GLA:], w_gate_up[l])
    w_up_b = w_up[l].astype(BF16)
    w_down_b = w_down[l].astype(BF16)
    scale = HEAD_DIM ** -0.5

    xp = x_prompt.reshape(n_p * s_p, d)
    qk, v_p, g, pre = _dense_in(xp, mods_p, norm_mix_w[l], w_in_l, w_gate, tm_row, tm, s_p // tm_row)
    cos_p, sin_p = _rope_tables(jnp.arange(s_p, dtype=jnp.int32))
    q_p = _qk_prep(qk, 0, cos_p, sin_p, q_norm_w[l], tm_row, s_p // tm_row, scale * LOG2E, BF16)
    k_p = _qk_prep(qk, 1, cos_p, sin_p, k_norm_w[l], tm_row, s_p // tm_row, 1.0, F32)
    oa_p = _moba_prompt(q_p, k_p, v_p, n_p, s_p)
    ob_p, gla_p = _gla_prompt(g, pre, b_gate[l], gla_norm_w[l], n_p, s_p)
    x1_p = _out_proj(oa_p, ob_p, w_out[l], xp, mods_p["g1"], tm, TN_DENSE, s_p // tm)
    y_p = _mlp(x1_p, mods_p["sc2"], mods_p["sh2"], mods_p["g2"], norm_ffn_w[l], w_up_b, w_down_b,
               tm, TF_MLP, s_p // tm)

    xs = x_sample.reshape(n_s, d)
    qk, v_s, g, pre = _dense_in(xs, mods_s, norm_mix_w[l], w_in_l, w_gate, n_s, n_s, 1)
    cos_s, sin_s = _rope_tables(jnp.full((n_s,), past_len, jnp.int32))
    q_s = _qk_prep(qk, 0, cos_s, sin_s, q_norm_w[l], n_s, 1, scale, BF16)
    k_s = _qk_prep(qk, 1, cos_s, sin_s, k_norm_w[l], n_s, 1, 1.0, F32)
    scores = _moba_scores(cache_k[l], page_table, q_s)
    picks = _moba_pick(scores.reshape(n_s * N_HEADS_A, past_len), nb_past)
    oa_s = _moba_gather(cache_v[l], page_table, picks, scores, q_s, k_s, v_s, nb_past).reshape(n_s, W_A)
    ob_s, gla_s = _gla_step(g, pre, b_gate[l], gla_norm_w[l], state_gla[l])
    x1_s = _out_proj(oa_s, ob_s, w_out[l], xs, mods_s["g1"], n_s, TN_DENSE, 1)
    y_s = _mlp(x1_s, mods_s["sc2"], mods_s["sh2"], mods_s["g2"], norm_ffn_w[l], w_up_b, w_down_b,
               n_s, TF_MLP, 1)

    return (y_p.reshape(n_p, s_p, d), y_s.reshape(n_s, s_s, d),
            k_p.reshape(1, n_p, s_p, N_HEADS_A, HEAD_DIM), v_p.reshape(1, n_p, s_p, N_HEADS_A, HEAD_DIM),
            gla_p[None],
            k_s.reshape(1, n_s, s_s, N_HEADS_A, HEAD_DIM), v_s.reshape(1, n_s, s_s, N_HEADS_A, HEAD_DIM),
            gla_s[None])
```

```python
import functools

import numpy as np
import jax
import jax.numpy as jnp
from jax import lax
from jax.experimental import pallas as pl
from jax.experimental.pallas import tpu as pltpu

D_MODEL = 2048
PAGE_SIZE = 128
HEAD_DIM = 128
N_HEADS_A = 8
N_HEADS_B = 4
DK_B = 128
DV_B = 256
GATE_RANK = 16
GATE_TEMP = 16.0
MOBA_BLOCK = 256
MOBA_TOPK = 3
GLA_CHUNK = 128
D_FF = 4 * D_MODEL
ROPE_THETA = 10000.0
EPS = 1e-6

W_A = N_HEADS_A * HEAD_DIM
W_QB = N_HEADS_B * DK_B
W_VB = N_HEADS_B * DV_B
W_GLA = 2 * W_QB + 2 * W_VB
LANES = 128

F32 = jnp.float32
BF16 = jnp.bfloat16
NEG_INF = float("-inf")
VMEM_LIMIT = 56 * 1024 * 1024

TM_ROWWISE = 512
TM_DENSE = 1024
TN_DENSE = 1024
TF_MLP = 512
MOBA_HEADS_PER_STEP = 2
MOBA_QBLOCK_GROUP = 4
ONES_ROWS = 16
LOG2E = 1.4426950408889634

_NT = (((1,), (1,)), ((), ()))


def _params(*sem):
    return pltpu.CompilerParams(dimension_semantics=sem, vmem_limit_bytes=VMEM_LIMIT)


def _rms(x, w):
    return x * lax.rsqrt(jnp.mean(x * x, axis=-1, keepdims=True) + EPS) * w


def _log_sigmoid(x):
    return jnp.minimum(x, 0.0) - jnp.log(1.0 + jnp.exp(-jnp.abs(x)))


def _silu(x):
    return x / (1.0 + jnp.exp(-x))


def _ada_kernel(c_ref, w_ref, b_ref, o_ref):
    a = _silu(c_ref[...]).astype(BF16)
    o_ref[...] = jnp.dot(a, w_ref[...].astype(BF16), preferred_element_type=F32) + b_ref[...]


def _ada(c, w, b, tn=512):
    m, d = c.shape
    n = w.shape[1]
    return pl.pallas_call(
        _ada_kernel,
        grid=(n // tn,),
        in_specs=[pl.BlockSpec((m, d), lambda j: (0, 0)),
                  pl.BlockSpec((d, tn), lambda j: (0, j)),
                  pl.BlockSpec((1, tn), lambda j: (0, j))],
        out_specs=pl.BlockSpec((m, tn), lambda j: (0, j)),
        out_shape=jax.ShapeDtypeStruct((m, n), F32),
        compiler_params=_params("arbitrary"),
        name="ada_proj",
    )(c, w, b.reshape(1, n))


def _mod_spec(mod, tiles_per_group):
    _, r, w = mod.shape
    return pl.BlockSpec((None, r, w), lambda i, *_: (i // tiles_per_group, 0, 0))


def _norm_mod_kernel(x_ref, sc_ref, sh_ref, w_ref, o_ref):
    y = _rms(x_ref[...], w_ref[...])
    o_ref[...] = (y * (1.0 + sc_ref[...]) + sh_ref[...]).astype(o_ref.dtype)


def _norm_mod(x, sc, sh, w, tm, tiles_per_group):
    m, d = x.shape
    return pl.pallas_call(
        _norm_mod_kernel,
        grid=(m // tm,),
        in_specs=[pl.BlockSpec((tm, d), lambda i: (i, 0)),
                  _mod_spec(sc, tiles_per_group), _mod_spec(sh, tiles_per_group),
                  pl.BlockSpec((1, d), lambda i: (0, 0))],
        out_specs=pl.BlockSpec((tm, d), lambda i: (i, 0)),
        out_shape=jax.ShapeDtypeStruct((m, d), BF16),
        compiler_params=_params("arbitrary"),
        name="norm_mod",
    )(x, sc, sh, w.reshape(1, d))


def _proj_kernel(a_ref, w_ref, o_ref, wb_scr):
    @pl.when(pl.program_id(1) == 0)
    def _cast():
        wb_scr[...] = w_ref[...].astype(BF16)

    o_ref[...] = jnp.dot(a_ref[...], wb_scr[...], preferred_element_type=F32).astype(o_ref.dtype)


def _proj(a, w, layer, col_block0, n_out, tm, tn, out_dtype=F32):
    m, k = a.shape
    return pl.pallas_call(
        _proj_kernel,
        grid=(n_out // tn, m // tm),
        in_specs=[pl.BlockSpec((tm, k), lambda j, i: (i, 0)),
                  pl.BlockSpec((None, k, tn), lambda j, i: (layer, 0, j + col_block0))],
        out_specs=pl.BlockSpec((tm, tn), lambda j, i: (i, j)),
        out_shape=jax.ShapeDtypeStruct((m, n_out), out_dtype),
        scratch_shapes=[pltpu.VMEM((k, tn), BF16)],
        compiler_params=_params("arbitrary", "arbitrary"),
        name="proj",
    )(a, w)


def _rope_heads(y, cos, sin, w, scale):
    out = []
    for h in range(y.shape[1] // HEAD_DIM):
        x = _rms(y[:, h * HEAD_DIM:(h + 1) * HEAD_DIM], w)
        r = x * cos + pltpu.roll(x, HEAD_DIM // 2, 1) * sin
        out.append(r * scale if scale != 1.0 else r)
    return out


def _proj_rope_kernel(a_ref, w_ref, cos_ref, sin_ref, nw_ref, o_ref, wb_scr, *, scale):
    @pl.when(pl.program_id(0) == 0)
    def _cast():
        wb_scr[...] = w_ref[...].astype(BF16)

    y = jnp.dot(a_ref[...], wb_scr[...], preferred_element_type=F32)
    for h, r in enumerate(_rope_heads(y, cos_ref[...], sin_ref[...], nw_ref[...], scale)):
        o_ref[:, h * HEAD_DIM:(h + 1) * HEAD_DIM] = r.astype(o_ref.dtype)


def _proj_rope(a, w, layer, col_block, cos, sin, norm_w, tm, pos_tiles, scale, out_dtype):
    m, k = a.shape
    return pl.pallas_call(
        functools.partial(_proj_rope_kernel, scale=scale),
        grid=(m // tm,),
        in_specs=[pl.BlockSpec((tm, k), lambda i: (i, 0)),
                  pl.BlockSpec((None, k, W_A), lambda i: (layer, 0, col_block)),
                  pl.BlockSpec((tm, HEAD_DIM), lambda i: (i % pos_tiles, 0)),
                  pl.BlockSpec((tm, HEAD_DIM), lambda i: (i % pos_tiles, 0)),
                  pl.BlockSpec((1, HEAD_DIM), lambda i: (0, 0))],
        out_specs=pl.BlockSpec((tm, W_A), lambda i: (i, 0)),
        out_shape=jax.ShapeDtypeStruct((m, W_A), out_dtype),
        scratch_shapes=[pltpu.VMEM((k, W_A), BF16)],
        compiler_params=_params("arbitrary"),
        name="proj_rope",
    )(a, w, cos, sin, norm_w.reshape(1, HEAD_DIM))


def _gate_weight_kernel(a_ref, b_ref, o_ref):
    o_ref[...] = jnp.dot(a_ref[...], b_ref[...], preferred_element_type=F32, precision=lax.Precision.HIGHEST)


def _gate_weight(w_ab, w_gate_up):
    d, r = w_ab.shape
    n = w_gate_up.shape[1]
    return pl.pallas_call(
        _gate_weight_kernel,
        grid=(1,),
        in_specs=[pl.BlockSpec((d, r), lambda i: (0, 0)), pl.BlockSpec((r, n), lambda i: (0, 0))],
        out_specs=pl.BlockSpec((d, n), lambda i: (0, 0)),
        out_shape=jax.ShapeDtypeStruct((d, n), F32),
        compiler_params=_params("arbitrary"),
        name="gate_weight",
    )(w_ab, w_gate_up)


def _rope_tables(pos):
    half = HEAD_DIM // 2
    inv_freq = ROPE_THETA ** (-jnp.arange(half, dtype=F32) / half)
    ang = pos.astype(F32)[:, None] * inv_freq[None, :]
    cos, sin = jnp.cos(ang), jnp.sin(ang)
    return jnp.concatenate([cos, cos], axis=-1), jnp.concatenate([-sin, sin], axis=-1)


def _topk_select(g, valid, axis):
    n = g.shape[axis]
    idx = lax.broadcasted_iota(jnp.int32, g.shape, axis)
    gm = jnp.where(valid, g, NEG_INF)
    rank = jnp.zeros(g.shape, F32)
    for c in range(n):
        gc = lax.slice_in_dim(gm, c, c + 1, axis=axis)
        beats = (gc > gm) | ((gc == gm) & (idx > c))
        rank = rank + beats.astype(F32)
    return valid & (rank < MOBA_TOPK)


def _moba_prompt_kernel(q_ref, k_ref, v_ref, o_ref, kb_scr, vt_scr, km_scr, sel_scr, s_scr, *, nb, hp, group):
    qb = pl.program_id(2)
    blk = MOBA_BLOCK
    heads = [slice(hh * HEAD_DIM, (hh + 1) * HEAD_DIM) for hh in range(hp)]

    @pl.when(qb == 0)
    def _prep():
        def per_block(j, carry):
            rows = pl.ds(pl.multiple_of(j * blk, blk), blk)
            for hh, cols in enumerate(heads):
                kj = k_ref[rows, cols]
                kb_scr[hh, j] = kj.astype(BF16)
                km_scr[hh, pl.ds(j, 1), :] = jnp.sum(kj, axis=0, keepdims=True) * (1.0 / blk)
                vt_scr[hh, j, 0:HEAD_DIM, :] = v_ref[rows, cols].T.astype(BF16)
                vt_scr[hh, j, HEAD_DIM:HEAD_DIM + ONES_ROWS, :] = jnp.ones((ONES_ROWS, blk), BF16)
            return carry

        lax.fori_loop(0, nb, per_block, 0)

    causal = (lax.broadcasted_iota(jnp.int32, (blk, blk), 0) <= lax.broadcasted_iota(jnp.int32, (blk, blk), 1))
    for hh, cols in enumerate(heads):
        gate = lax.dot_general(km_scr[hh].astype(BF16), q_ref[:, cols], _NT, preferred_element_type=F32)
        cand = lax.broadcasted_iota(jnp.int32, gate.shape, 0)
        sel_scr[hh] = (_topk_select(gate, cand < qb, 0) | (cand == qb)).astype(F32)

    def attend(n_keys):
        first_own = n_keys - group
        for hh, cols in enumerate(heads):
            q = q_ref[:, cols]
            m = jnp.full((1, blk), NEG_INF, F32)
            for j in range(n_keys):
                s = lax.dot_general(kb_scr[hh, j], q, _NT, preferred_element_type=F32)
                s = jnp.where(sel_scr[hh, j:j + 1, :] > 0.0, s, NEG_INF)
                s_scr[hh, j] = s
                bmax = jnp.max(s, axis=0, keepdims=True)
                if j >= first_own:
                    bmax = jnp.where(qb == j, NEG_INF, bmax)
                m = jnp.maximum(m, bmax)
            s_own = jnp.where(causal, s_scr[hh, qb], NEG_INF)
            s_scr[hh, qb] = s_own
            m = jnp.maximum(m, jnp.max(s_own, axis=0, keepdims=True))
            acc = jnp.zeros((HEAD_DIM + ONES_ROWS, blk), F32)
            for j in range(n_keys):
                p = jnp.exp2(s_scr[hh, j] - m)
                acc = acc + jnp.dot(vt_scr[hh, j], p.astype(BF16), preferred_element_type=F32)
            l = acc[HEAD_DIM:HEAD_DIM + 1, :]
            o_ref[:, cols] = (acc[0:HEAD_DIM, :] * (1.0 / l)).T.astype(o_ref.dtype)

    for g in range(nb // group):
        pl.when(qb // group == g)(functools.partial(attend, (g + 1) * group))


def _moba_prompt(q, k, v, n_seq, seq):
    nb = seq // MOBA_BLOCK
    hp = MOBA_HEADS_PER_STEP
    group = min(MOBA_QBLOCK_GROUP, nb)
    assert nb % group == 0
    w = hp * HEAD_DIM
    return pl.pallas_call(
        functools.partial(_moba_prompt_kernel, nb=nb, hp=hp, group=group),
        grid=(n_seq, N_HEADS_A // hp, nb),
        in_specs=[pl.BlockSpec((MOBA_BLOCK, w), lambda n, h, b: (n * nb + b, h)),
                  pl.BlockSpec((seq, w), lambda n, h, b: (n, h)),
                  pl.BlockSpec((seq, w), lambda n, h, b: (n, h))],
        out_specs=pl.BlockSpec((MOBA_BLOCK, w), lambda n, h, b: (n * nb + b, h)),
        out_shape=jax.ShapeDtypeStruct((n_seq * seq, W_A), BF16),
        scratch_shapes=[pltpu.VMEM((hp, nb, MOBA_BLOCK, HEAD_DIM), BF16),
                        pltpu.VMEM((hp, nb, HEAD_DIM + ONES_ROWS, MOBA_BLOCK), BF16),
                        pltpu.VMEM((hp, nb, HEAD_DIM), F32),
                        pltpu.VMEM((hp, nb, MOBA_BLOCK), F32),
                        pltpu.VMEM((hp, nb, MOBA_BLOCK, MOBA_BLOCK), F32)],
        compiler_params=_params("arbitrary", "arbitrary", "arbitrary"),
        name="moba_prompt",
    )(q, k, v)


def _gla_levels():
    m, out = 1, []
    while m < GLA_CHUNK:
        out.append(m)
        m *= 2
    return out


def _gla_cum_matrices():
    c = GLA_CHUNK
    t = np.arange(c)
    tri = (t[None, :] <= t[:, None]).astype(np.float32)
    mats = [tri]
    for m in _gla_levels():
        mats.append(tri[(t // (2 * m)) * (2 * m) + m - 1])
    return np.concatenate(mats, axis=0)


def _split_bf16(x):
    hi = x.astype(BF16)
    return hi, (x - hi.astype(F32)).astype(BF16)


def _gla_prompt_kernel(g_ref, pre_ref, cm_ref, bg_ref, nw_ref, o_ref, s_ref):
    c = GLA_CHUNK

    @pl.when(pl.program_id(1) == 0)
    def _init():
        s_ref[...] = jnp.zeros(s_ref.shape, F32)

    tok = lax.broadcasted_iota(jnp.int32, (c, c), 0)
    col = lax.broadcasted_iota(jnp.int32, (c, c), 1)
    tok_d = lax.broadcasted_iota(jnp.int32, (c, DK_B), 0)
    cm = cm_ref[...]
    for h in range(N_HEADS_B):
        dk = slice(h * DK_B, (h + 1) * DK_B)
        la = _log_sigmoid(pre_ref[:, dk] + bg_ref[:, dk]) * (LOG2E / GATE_TEMP)
        cums = sum(jnp.dot(cm, piece, preferred_element_type=F32) for piece in _split_bf16(la))
        b = cums[0:c]
        q = g_ref[:, dk] * (DK_B ** -0.5)
        k = g_ref[:, W_QB + h * DK_B:W_QB + (h + 1) * DK_B]
        v = g_ref[:, 2 * W_QB + h * DV_B:2 * W_QB + (h + 1) * DV_B]
        rb = g_ref[:, 2 * W_QB + W_VB + h * DV_B:2 * W_QB + W_VB + (h + 1) * DV_B]
        vb = v.astype(BF16)

        attn = lax.dot_general(q.astype(BF16), k.astype(BF16), _NT, preferred_element_type=F32)
        attn = jnp.where(col == tok, attn, 0.0)
        for lvl, m in enumerate(_gla_levels()):
            r = cums[(lvl + 1) * c:(lvl + 2) * c]
            upper = ((tok_d // m) % 2) == 1
            x = b - r
            e = jnp.exp2(jnp.where(upper, x, -x))
            e_up = jnp.where(upper, e, 0.0)
            a_m = lax.dot_general((q * e_up).astype(BF16), (k * (e - e_up)).astype(BF16), _NT,
                                  preferred_element_type=F32)
            attn = attn + jnp.where((tok // (2 * m)) == (col // (2 * m)), a_m, 0.0)

        state = s_ref[0, h]
        o = jnp.dot(attn.astype(BF16), vb, preferred_element_type=F32)
        o = o + jnp.dot((q * jnp.exp2(b)).astype(BF16), state.astype(BF16), preferred_element_type=F32)
        b_last = b[c - 1:c, :]
        kd = k * jnp.exp2(b_last - b)
        kv = jnp.dot(kd.T.astype(BF16), vb, preferred_element_type=F32)
        dcol = b.T[:, c - 1:c]
        s_ref[0, h] = jnp.exp2(dcol) * state + kv

        y = _rms(o, nw_ref[...]) * _silu(rb)
        o_ref[:, h * DV_B:(h + 1) * DV_B] = y.astype(o_ref.dtype)


def _gla_prompt(g, pre, bg, nw, n_seq, seq):
    nc = seq // GLA_CHUNK
    cm = jnp.asarray(_gla_cum_matrices(), BF16)
    return pl.pallas_call(
        _gla_prompt_kernel,
        grid=(n_seq, nc),
        in_specs=[pl.BlockSpec((GLA_CHUNK, W_GLA), lambda n, t: (n * nc + t, 0)),
                  pl.BlockSpec((GLA_CHUNK, W_QB), lambda n, t: (n * nc + t, 0)),
                  pl.BlockSpec(cm.shape, lambda n, t: (0, 0)),
                  pl.BlockSpec((1, W_QB), lambda n, t: (0, 0)),
                  pl.BlockSpec((1, DV_B), lambda n, t: (0, 0))],
        out_specs=[pl.BlockSpec((GLA_CHUNK, W_VB), lambda n, t: (n * nc + t, 0)),
                   pl.BlockSpec((1, N_HEADS_B, DK_B, DV_B), lambda n, t: (n, 0, 0, 0))],
        out_shape=[jax.ShapeDtypeStruct((n_seq * seq, W_VB), BF16),
                   jax.ShapeDtypeStruct((n_seq, N_HEADS_B, DK_B, DV_B), F32)],
        compiler_params=_params("arbitrary", "arbitrary"),
        name="gla_prompt",
    )(g, pre, cm, bg.reshape(1, W_QB), nw.reshape(1, DV_B))


def _gla_step_kernel(qc_ref, kc_ref, prec_ref, v_ref, rb_ref, bgc_ref, nw_ref, s0_ref, o_ref, s_ref):
    for h in range(N_HEADS_B):
        rows = slice(h * DK_B, (h + 1) * DK_B)
        a = jnp.exp(_log_sigmoid(prec_ref[0, rows, :] + bgc_ref[rows, :]) * (1.0 / GATE_TEMP))
        v = v_ref[0, :, h * DV_B:(h + 1) * DV_B]
        s_new = a * s0_ref[0, h] + kc_ref[0, rows, :] * v
        s_ref[0, h] = s_new
        o = jnp.sum((qc_ref[0, rows, :] * (DK_B ** -0.5)) * s_new, axis=0, keepdims=True)
        y = _rms(o, nw_ref[...]) * _silu(rb_ref[0, :, h * DV_B:(h + 1) * DV_B])
        o_ref[0, :, h * DV_B:(h + 1) * DV_B] = y.astype(o_ref.dtype)


def _gla_step(g, pre, bg, nw, s0):
    n = g.shape[0]
    qc = g[:, 0:W_QB].reshape(n, W_QB, 1)
    kc = g[:, W_QB:2 * W_QB].reshape(n, W_QB, 1)
    v = g[:, 2 * W_QB:2 * W_QB + W_VB].reshape(n, 1, W_VB)
    rb = g[:, 2 * W_QB + W_VB:W_GLA].reshape(n, 1, W_VB)
    per_n3 = lambda i: (i, 0, 0)
    fixed = lambda i: (0, 0)
    o, s = pl.pallas_call(
        _gla_step_kernel,
        grid=(n,),
        in_specs=[pl.BlockSpec((1, W_QB, 1), per_n3), pl.BlockSpec((1, W_QB, 1), per_n3),
                  pl.BlockSpec((1, W_QB, 1), per_n3),
                  pl.BlockSpec((1, 1, W_VB), per_n3), pl.BlockSpec((1, 1, W_VB), per_n3),
                  pl.BlockSpec((W_QB, 1), fixed), pl.BlockSpec((1, DV_B), fixed),
                  pl.BlockSpec((1, N_HEADS_B, DK_B, DV_B), lambda i: (i, 0, 0, 0))],
        out_specs=[pl.BlockSpec((1, 1, W_VB), per_n3),
                   pl.BlockSpec((1, N_HEADS_B, DK_B, DV_B), lambda i: (i, 0, 0, 0))],
        out_shape=[jax.ShapeDtypeStruct((n, 1, W_VB), BF16),
                   jax.ShapeDtypeStruct(s0.shape, F32)],
        compiler_params=_params("arbitrary"),
        name="gla_step",
    )(qc, kc, pre.reshape(n, W_QB, 1), v, rb, bg.reshape(W_QB, 1), nw.reshape(1, DV_B), s0)
    return o.reshape(n, W_VB), s


PAGES_PER_STEP = 8


def _moba_scores_kernel(pt_ref, q_ref, *refs):
    k_refs, o_ref = refs[:PAGES_PER_STEP], refs[PAGES_PER_STEP]
    q = q_ref[0].astype(F32)
    ones = jnp.ones((HEAD_DIM, LANES), BF16)
    shape3 = (PAGE_SIZE, N_HEADS_A, LANES)
    diag = lax.broadcasted_iota(jnp.int32, shape3, 0) == lax.broadcasted_iota(jnp.int32, shape3, 2)
    for p, k_ref in enumerate(k_refs):
        prod = (k_ref[0] * q[None]).reshape(PAGE_SIZE * N_HEADS_A, HEAD_DIM)
        r = jnp.dot(prod.astype(BF16), ones, preferred_element_type=F32).reshape(shape3)
        o_ref[0, :, p * PAGE_SIZE:(p + 1) * PAGE_SIZE] = jnp.sum(jnp.where(diag, r, 0.0), axis=0)


def _moba_scores(cache_k, page_table, q):
    n, n_pages = page_table.shape
    steps = n_pages // PAGES_PER_STEP

    def page_spec(p):
        return pl.BlockSpec((1, PAGE_SIZE, N_HEADS_A, HEAD_DIM),
                            lambda i, g, pt: (pt[i, g * PAGES_PER_STEP + p], 0, 0, 0))

    return pl.pallas_call(
        _moba_scores_kernel,
        grid_spec=pltpu.PrefetchScalarGridSpec(
            num_scalar_prefetch=1,
            grid=(n, steps),
            in_specs=[pl.BlockSpec((1, N_HEADS_A, HEAD_DIM), lambda i, g, pt: (i, 0, 0))]
                     + [page_spec(p) for p in range(PAGES_PER_STEP)],
            out_specs=pl.BlockSpec((1, N_HEADS_A, PAGES_PER_STEP * PAGE_SIZE), lambda i, g, pt: (i, 0, g)),
        ),
        out_shape=jax.ShapeDtypeStruct((n, N_HEADS_A, n_pages * PAGE_SIZE), F32),
        compiler_params=_params("arbitrary", "arbitrary"),
        name="moba_scores",
    )(page_table, q.reshape(n, N_HEADS_A, HEAD_DIM), *([cache_k] * PAGES_PER_STEP))


def _moba_pick_kernel(s_ref, o_ref, *, nb):
    rows = s_ref.shape[0]
    lane = lax.broadcasted_iota(jnp.int32, (rows, LANES), 1)
    gate = jnp.zeros((rows, LANES), F32)
    for j in range(nb):
        gj = jnp.sum(s_ref[:, j * MOBA_BLOCK:(j + 1) * MOBA_BLOCK], axis=-1, keepdims=True) * (1.0 / MOBA_BLOCK)
        gate = jnp.where(lane == j, gj, gate)
    valid = lane < nb
    gm = jnp.where(valid, gate, NEG_INF)
    rank = jnp.zeros((rows, LANES), F32)
    for c in range(nb):
        gc = gm[:, c:c + 1]
        beats = (gc > gm) | ((gc == gm) & (lane > c))
        rank = rank + beats.astype(F32)
    out = jnp.zeros((rows, LANES), jnp.int32)
    for t in range(MOBA_TOPK):
        pick = jnp.sum(jnp.where(valid & (rank == float(t)), lane, 0), axis=-1, keepdims=True)
        out = jnp.where(lane == t, pick, out)
    o_ref[...] = out


def _moba_pick(scores, nb):
    rows = scores.shape[0]
    return pl.pallas_call(
        functools.partial(_moba_pick_kernel, nb=nb),
        grid=(1,),
        in_specs=[pl.BlockSpec(scores.shape, lambda i: (0, 0))],
        out_specs=pl.BlockSpec((rows, LANES), lambda i: (0, 0)),
        out_shape=jax.ShapeDtypeStruct((rows, LANES), jnp.int32),
        compiler_params=_params("arbitrary"),
        name="moba_pick",
    )(scores)


def _moba_gather_kernel(pt_ref, pick_ref, q_ref, kn_ref, vn_ref, s_ref, *refs):
    v_refs, o_ref = refs[:2 * MOBA_TOPK], refs[2 * MOBA_TOPK]
    row = pl.program_id(0) * N_HEADS_A + pl.program_id(1)
    s_self = jnp.sum(q_ref[0, 0].astype(F32) * kn_ref[0, 0].astype(BF16).astype(F32), axis=-1, keepdims=True)
    s = [s_ref[0, 0, pl.ds(pick_ref[row, t], 1), :] for t in range(MOBA_TOPK)]
    m = s_self
    for st in s:
        m = jnp.maximum(m, jnp.max(st, axis=-1, keepdims=True))
    p_self = jnp.exp(s_self - m)
    l = p_self
    acc = jnp.zeros((8, HEAD_DIM), F32)
    for t, st in enumerate(s):
        p = jnp.exp(st - m)
        l = l + jnp.sum(p, axis=-1, keepdims=True)
        pb = jnp.broadcast_to(p, (8, MOBA_BLOCK)).astype(BF16)
        for half in range(2):
            vpage = v_refs[2 * t + half][0, :, 0, 0, :].astype(BF16)
            acc = acc + jnp.dot(pb[:, half * PAGE_SIZE:(half + 1) * PAGE_SIZE], vpage, preferred_element_type=F32)
    out = acc[0:1] + p_self * vn_ref[0, 0].astype(BF16).astype(F32)
    o_ref[0, 0] = (out * (1.0 / l)).astype(o_ref.dtype)


def _moba_gather(cache_v, page_table, picks, scores, q, k_new, v_new, nb):
    n = page_table.shape[0]
    pages_per_block = MOBA_BLOCK // PAGE_SIZE
    assert pages_per_block == 2
    scores4 = scores.reshape(n, N_HEADS_A, nb, MOBA_BLOCK)
    cache_v5 = cache_v.reshape(cache_v.shape[0], PAGE_SIZE, N_HEADS_A, 1, HEAD_DIM)

    def v_spec(t, half):
        return pl.BlockSpec((1, PAGE_SIZE, 1, 1, HEAD_DIM),
                            lambda i, h, pt, pk: (pt[i, pk[i * N_HEADS_A + h, t] * pages_per_block + half], 0, h, 0, 0))

    return pl.pallas_call(
        _moba_gather_kernel,
        grid_spec=pltpu.PrefetchScalarGridSpec(
            num_scalar_prefetch=2,
            grid=(n, N_HEADS_A),
            in_specs=[pl.BlockSpec((1, 1, 1, HEAD_DIM), lambda i, h, pt, pk: (i, h, 0, 0))] * 3
                     + [pl.BlockSpec((1, 1, nb, MOBA_BLOCK), lambda i, h, pt, pk: (i, h, 0, 0))]
                     + [v_spec(t, half) for t in range(MOBA_TOPK) for half in range(2)],
            out_specs=pl.BlockSpec((1, 1, 1, HEAD_DIM), lambda i, h, pt, pk: (i, h, 0, 0)),
        ),
        out_shape=jax.ShapeDtypeStruct((n, N_HEADS_A, 1, HEAD_DIM), BF16),
        compiler_params=_params("arbitrary", "arbitrary"),
        name="moba_gather",
    )(page_table, picks,
      q.reshape(n, N_HEADS_A, 1, HEAD_DIM), k_new.reshape(n, N_HEADS_A, 1, HEAD_DIM),
      v_new.reshape(n, N_HEADS_A, 1, HEAD_DIM), scores4,
      *([cache_v5] * (2 * MOBA_TOPK)))


def _out_proj_kernel(oa_ref, ob_ref, w_ref, x_ref, g_ref, o_ref, wb_scr):
    @pl.when(pl.program_id(1) == 0)
    def _cast():
        wb_scr[...] = w_ref[...].astype(BF16)

    mix = jnp.dot(oa_ref[...], wb_scr[0:W_A, :], preferred_element_type=F32)
    mix = mix + jnp.dot(ob_ref[...], wb_scr[W_A:W_A + W_VB, :], preferred_element_type=F32)
    o_ref[...] = x_ref[...] + g_ref[...] * mix


def _out_proj(oa, ob, w, layer, x, gate, tm, tn, tiles_per_group):
    m, d = x.shape
    _, r, _ = gate.shape
    return pl.pallas_call(
        _out_proj_kernel,
        grid=(d // tn, m // tm),
        in_specs=[pl.BlockSpec((tm, W_A), lambda j, i: (i, 0)),
                  pl.BlockSpec((tm, W_VB), lambda j, i: (i, 0)),
                  pl.BlockSpec((None, W_A + W_VB, tn), lambda j, i: (layer, 0, j)),
                  pl.BlockSpec((tm, tn), lambda j, i: (i, j)),
                  pl.BlockSpec((None, r, tn), lambda j, i: (i // tiles_per_group, 0, j))],
        out_specs=pl.BlockSpec((tm, tn), lambda j, i: (i, j)),
        out_shape=jax.ShapeDtypeStruct((m, d), F32),
        scratch_shapes=[pltpu.VMEM((W_A + W_VB, tn), BF16)],
        compiler_params=_params("arbitrary", "arbitrary"),
        name="out_proj",
    )(oa, ob, w, x, gate)


def _mlp_kernel(x_ref, sc_ref, sh_ref, g_ref, nw_ref, wu_ref, wd_ref, o_ref, h_scr):
    f = pl.program_id(1)
    d = o_ref.shape[1]
    tn = min(d, TN_DENSE)

    @pl.when(f == 0)
    def _first():
        y = _rms(x_ref[...], nw_ref[...])
        h_scr[...] = (y * (1.0 + sc_ref[...]) + sh_ref[...]).astype(BF16)
        o_ref[...] = jnp.zeros(o_ref.shape, F32)

    u = jnp.dot(h_scr[...], wu_ref[...], preferred_element_type=F32)
    r = jnp.maximum(u, 0.0)
    r2 = (r * r).astype(BF16)
    for c0 in range(0, d, tn):
        o_ref[:, c0:c0 + tn] += jnp.dot(r2, wd_ref[:, c0:c0 + tn], preferred_element_type=F32)

    @pl.when(f == pl.num_programs(1) - 1)
    def _last():
        o_ref[...] = x_ref[...] + g_ref[...] * o_ref[...]


def _mlp(x, sc, sh, gate, nw, wu, wd, tm, tf, tiles_per_group):
    m, d = x.shape
    ff = wu.shape[1]
    return pl.pallas_call(
        _mlp_kernel,
        grid=(m // tm, ff // tf),
        in_specs=[pl.BlockSpec((tm, d), lambda i, f: (i, 0)),
                  _mod_spec(sc, tiles_per_group), _mod_spec(sh, tiles_per_group), _mod_spec(gate, tiles_per_group),
                  pl.BlockSpec((1, d), lambda i, f: (0, 0)),
                  pl.BlockSpec((d, tf), lambda i, f: (0, f)),
                  pl.BlockSpec((tf, d), lambda i, f: (f, 0))],
        out_specs=pl.BlockSpec((tm, d), lambda i, f: (i, 0)),
        out_shape=jax.ShapeDtypeStruct((m, d), F32),
        scratch_shapes=[pltpu.VMEM((tm, d), BF16)],
        compiler_params=_params("arbitrary", "arbitrary"),
        name="mlp",
    )(x, sc, sh, gate, nw.reshape(1, d), wu, wd)


def _dense_in(x2, mods, norm_w, w_in, layer, w_gate, rope, qk_norm_w, q_scale, tm_row, tm, tiles_per_group):
    h = _norm_mod(x2, mods["sc1"], mods["sh1"], norm_w, tm_row, tiles_per_group)
    tn = TN_DENSE
    cos, sin = rope
    pos_tiles = cos.shape[0] // tm
    q = _proj_rope(h, w_in, layer, 0, cos, sin, qk_norm_w[0], tm, pos_tiles, q_scale, BF16)
    k = _proj_rope(h, w_in, layer, 1, cos, sin, qk_norm_w[1], tm, pos_tiles, 1.0, F32)
    v = _proj(h, w_in, layer, 2 * W_A // tn, W_A, tm, tn)
    g = _proj(h, w_in, layer, 3 * W_A // tn, W_GLA, tm, tn)
    pre = _proj(h, w_gate, 0, 0, W_QB, tm, W_QB)
    return q, k, v, g, pre


def kernel(x_prompt, x_sample, c_prompt, c_sample, cache_k, cache_v, state_gla, page_table, w_ada, b_ada, norm_mix_w, w_in, q_norm_w, k_norm_w, w_gate_up, b_gate, gla_norm_w, w_out, norm_ffn_w, w_up, w_down):
    n_p, s_p, d = x_prompt.shape
    n_s, s_s, _ = x_sample.shape
    depth = w_ada.shape[0]
    assert depth == 1 and s_s == 1 and d == D_MODEL
    n_pages = page_table.shape[1]
    past_len = n_pages * PAGE_SIZE
    nb_past = past_len // MOBA_BLOCK
    assert past_len % MOBA_BLOCK == 0 and nb_past >= MOBA_TOPK
    l = 0
    tm_row = min(TM_ROWWISE, s_p)
    tm = min(TM_DENSE, s_p)

    rows = n_p + n_s
    rows_pad = -(-rows // 8) * 8
    c_all = jnp.concatenate([c_prompt, c_sample, jnp.zeros((rows_pad - rows, d), F32)], axis=0)
    ada = _ada(c_all, w_ada[l], b_ada[l])
    names = ("sh1", "sc1", "g1", "sh2", "sc2", "g2")
    mods_p = {nm: ada[0:n_p, i * d:(i + 1) * d].reshape(n_p, 1, d) for i, nm in enumerate(names)}
    mods_s = {nm: ada[n_p:rows, i * d:(i + 1) * d].reshape(1, n_s, d) for i, nm in enumerate(names)}

    w_gate = _gate_weight(w_in[l, :, 3 * W_A + W_GLA:], w_gate_up[l])[None]
    w_up_b = w_up[l].astype(BF16)
    w_down_b = w_down[l].astype(BF16)
    scale = HEAD_DIM ** -0.5
    qk_norm_w = (q_norm_w[l], k_norm_w[l])

    xp = x_prompt.reshape(n_p * s_p, d)
    rope_p = _rope_tables(jnp.arange(s_p, dtype=jnp.int32))
    q_p, k_p, v_p, g, pre = _dense_in(xp, mods_p, norm_mix_w[l], w_in, l, w_gate, rope_p, qk_norm_w,
                                      scale * LOG2E, tm_row, tm, s_p // tm_row)
    oa_p = _moba_prompt(q_p, k_p, v_p, n_p, s_p)
    ob_p, gla_p = _gla_prompt(g, pre, b_gate[l], gla_norm_w[l], n_p, s_p)
    x1_p = _out_proj(oa_p, ob_p, w_out, l, xp, mods_p["g1"], tm, TN_DENSE, s_p // tm)
    y_p = _mlp(x1_p, mods_p["sc2"], mods_p["sh2"], mods_p["g2"], norm_ffn_w[l], w_up_b, w_down_b,
               tm, TF_MLP, s_p // tm)

    xs = x_sample.reshape(n_s, d)
    rope_s = _rope_tables(jnp.full((n_s,), past_len, jnp.int32))
    q_s, k_s, v_s, g, pre = _dense_in(xs, mods_s, norm_mix_w[l], w_in, l, w_gate, rope_s, qk_norm_w,
                                      scale, n_s, n_s, 1)
    scores = _moba_scores(cache_k[l], page_table, q_s)
    picks = _moba_pick(scores.reshape(n_s * N_HEADS_A, past_len), nb_past)
    oa_s = _moba_gather(cache_v[l], page_table, picks, scores, q_s, k_s, v_s, nb_past).reshape(n_s, W_A)
    ob_s, gla_s = _gla_step(g, pre, b_gate[l], gla_norm_w[l], state_gla[l])
    x1_s = _out_proj(oa_s, ob_s, w_out, l, xs, mods_s["g1"], n_s, TN_DENSE, 1)
    y_s = _mlp(x1_s, mods_s["sc2"], mods_s["sh2"], mods_s["g2"], norm_ffn_w[l], w_up_b, w_down_b,
               n_s, TF_MLP, 1)

    return (y_p.reshape(n_p, s_p, d), y_s.reshape(n_s, s_s, d),
            k_p.reshape(1, n_p, s_p, N_HEADS_A, HEAD_DIM), v_p.reshape(1, n_p, s_p, N_HEADS_A, HEAD_DIM),
            gla_p[None],
            k_s.reshape(1, n_s, s_s, N_HEADS_A, HEAD_DIM), v_s.reshape(1, n_s, s_s, N_HEADS_A, HEAD_DIM),
            gla_s[None])
```

```python
import functools

import numpy as np
import jax
import jax.numpy as jnp
from jax import lax
from jax.experimental import pallas as pl
from jax.experimental.pallas import tpu as pltpu

D_MODEL = 2048
PAGE_SIZE = 128
HEAD_DIM = 128
N_HEADS_A = 8
N_HEADS_B = 4
DK_B = 128
DV_B = 256
GATE_RANK = 16
GATE_TEMP = 16.0
MOBA_BLOCK = 256
MOBA_TOPK = 3
GLA_CHUNK = 128
D_FF = 4 * D_MODEL
ROPE_THETA = 10000.0
EPS = 1e-6

W_A = N_HEADS_A * HEAD_DIM
W_QB = N_HEADS_B * DK_B
W_VB = N_HEADS_B * DV_B
W_GLA = 2 * W_QB + 2 * W_VB
LANES = 128

F32 = jnp.float32
BF16 = jnp.bfloat16
NEG_INF = float("-inf")
VMEM_LIMIT = 56 * 1024 * 1024

TM_ROWWISE = 512
TM_DENSE = 1024
TN_DENSE = 1024
TM_MLP = 512
TF_MLP = 512
ROPE_SUB_ROWS = 256
MOBA_HEADS_PER_STEP = 2
MOBA_QBLOCK_GROUP = 4
ONES_ROWS = 16
LOG2E = 1.4426950408889634

_NT = (((1,), (1,)), ((), ()))


def _params(*sem):
    return pltpu.CompilerParams(dimension_semantics=sem, vmem_limit_bytes=VMEM_LIMIT)


def _rms(x, w):
    return x * lax.rsqrt(jnp.mean(x * x, axis=-1, keepdims=True) + EPS) * w


def _log_sigmoid(x):
    return jnp.minimum(x, 0.0) - jnp.log(1.0 + jnp.exp(-jnp.abs(x)))


def _silu(x):
    return x / (1.0 + jnp.exp(-x))


def _ada_kernel(c_ref, w_ref, b_ref, o_ref):
    a = _silu(c_ref[...]).astype(BF16)
    o_ref[...] = jnp.dot(a, w_ref[...].astype(BF16), preferred_element_type=F32) + b_ref[...]


def _ada(c, w, b, tn=512):
    m, d = c.shape
    n = w.shape[1]
    return pl.pallas_call(
        _ada_kernel,
        grid=(n // tn,),
        in_specs=[pl.BlockSpec((m, d), lambda j: (0, 0)),
                  pl.BlockSpec((d, tn), lambda j: (0, j)),
                  pl.BlockSpec((1, tn), lambda j: (0, j))],
        out_specs=pl.BlockSpec((m, tn), lambda j: (0, j)),
        out_shape=jax.ShapeDtypeStruct((m, n), F32),
        compiler_params=_params("arbitrary"),
        name="ada_proj",
    )(c, w, b.reshape(1, n))


def _mod_spec(mod, tiles_per_group):
    _, r, w = mod.shape
    return pl.BlockSpec((None, r, w), lambda i, *_: (i // tiles_per_group, 0, 0))


def _norm_mod_kernel(x_ref, sc_ref, sh_ref, w_ref, o_ref):
    y = _rms(x_ref[...], w_ref[...])
    o_ref[...] = (y * (1.0 + sc_ref[...]) + sh_ref[...]).astype(o_ref.dtype)


def _norm_mod(x, sc, sh, w, tm, tiles_per_group):
    m, d = x.shape
    return pl.pallas_call(
        _norm_mod_kernel,
        grid=(m // tm,),
        in_specs=[pl.BlockSpec((tm, d), lambda i: (i, 0)),
                  _mod_spec(sc, tiles_per_group), _mod_spec(sh, tiles_per_group),
                  pl.BlockSpec((1, d), lambda i: (0, 0))],
        out_specs=pl.BlockSpec((tm, d), lambda i: (i, 0)),
        out_shape=jax.ShapeDtypeStruct((m, d), BF16),
        compiler_params=_params("arbitrary"),
        name="norm_mod",
    )(x, sc, sh, w.reshape(1, d))


def _weight_spec(k, tn, layer, col_block0, transposed):
    if transposed:
        return pl.BlockSpec((None, tn, k), lambda j, *_: (layer, j + col_block0, 0))
    return pl.BlockSpec((None, k, tn), lambda j, *_: (layer, 0, j + col_block0))


def _weight_dot(a, wb, transposed):
    if transposed:
        return lax.dot_general(a, wb, _NT, preferred_element_type=F32)
    return jnp.dot(a, wb, preferred_element_type=F32)


def _proj_kernel(a_ref, w_ref, o_ref, wb_scr, *, transposed):
    @pl.when(pl.program_id(1) == 0)
    def _cast():
        wb_scr[...] = w_ref[...].astype(BF16)

    o_ref[...] = _weight_dot(a_ref[...], wb_scr[...], transposed).astype(o_ref.dtype)


def _proj(a, w, layer, col_block0, n_out, tm, tn, transposed=False, out_dtype=F32):
    m, k = a.shape
    w_spec = _weight_spec(k, tn, layer, col_block0, transposed)
    return pl.pallas_call(
        functools.partial(_proj_kernel, transposed=transposed),
        grid=(n_out // tn, m // tm),
        in_specs=[pl.BlockSpec((tm, k), lambda j, i: (i, 0)), w_spec],
        out_specs=pl.BlockSpec((tm, tn), lambda j, i: (i, j)),
        out_shape=jax.ShapeDtypeStruct((m, n_out), out_dtype),
        scratch_shapes=[pltpu.VMEM(w_spec.block_shape[1:], BF16)],
        compiler_params=_params("arbitrary", "arbitrary"),
        name="proj",
    )(a, w)


def _rope_heads(y, cos, sin, w, scale):
    out = []
    for h in range(y.shape[1] // HEAD_DIM):
        x = _rms(y[:, h * HEAD_DIM:(h + 1) * HEAD_DIM], w)
        r = x * cos + pltpu.roll(x, HEAD_DIM // 2, 1) * sin
        out.append(r * scale if scale != 1.0 else r)
    return out


def _proj_rope_kernel(a_ref, w_ref, cos_ref, sin_ref, nw_ref, o_ref, wb_scr, *, scale, rows, transposed):
    @pl.when(pl.program_id(0) == 0)
    def _cast():
        wb_scr[...] = w_ref[...].astype(BF16)

    for r0 in range(0, a_ref.shape[0], rows):
        rs = slice(r0, r0 + rows)
        y = _weight_dot(a_ref[rs, :], wb_scr[...], transposed)
        for h, r in enumerate(_rope_heads(y, cos_ref[rs, :], sin_ref[rs, :], nw_ref[...], scale)):
            o_ref[rs, h * HEAD_DIM:(h + 1) * HEAD_DIM] = r.astype(o_ref.dtype)


def _proj_rope(a, w, layer, col_block, cos, sin, norm_w, tm, pos_tiles, scale, transposed, out_dtype):
    m, k = a.shape
    if transposed:
        w_spec = pl.BlockSpec((None, W_A, k), lambda i: (layer, col_block, 0))
    else:
        w_spec = pl.BlockSpec((None, k, W_A), lambda i: (layer, 0, col_block))
    return pl.pallas_call(
        functools.partial(_proj_rope_kernel, scale=scale, rows=min(tm, ROPE_SUB_ROWS), transposed=transposed),
        grid=(m // tm,),
        in_specs=[pl.BlockSpec((tm, k), lambda i: (i, 0)),
                  w_spec,
                  pl.BlockSpec((tm, HEAD_DIM), lambda i: (i % pos_tiles, 0)),
                  pl.BlockSpec((tm, HEAD_DIM), lambda i: (i % pos_tiles, 0)),
                  pl.BlockSpec((1, HEAD_DIM), lambda i: (0, 0))],
        out_specs=pl.BlockSpec((tm, W_A), lambda i: (i, 0)),
        out_shape=jax.ShapeDtypeStruct((m, W_A), out_dtype),
        scratch_shapes=[pltpu.VMEM(w_spec.block_shape[1:], BF16)],
        compiler_params=_params("arbitrary"),
        name="proj_rope",
    )(a, w, cos, sin, norm_w.reshape(1, HEAD_DIM))


def _gate_weight_kernel(a_ref, b_ref, o_ref):
    o_ref[...] = jnp.dot(a_ref[...], b_ref[...], preferred_element_type=F32, precision=lax.Precision.HIGHEST)


def _gate_weight(w_ab, w_gate_up):
    d, r = w_ab.shape
    n = w_gate_up.shape[1]
    return pl.pallas_call(
        _gate_weight_kernel,
        grid=(1,),
        in_specs=[pl.BlockSpec((d, r), lambda i: (0, 0)), pl.BlockSpec((r, n), lambda i: (0, 0))],
        out_specs=pl.BlockSpec((d, n), lambda i: (0, 0)),
        out_shape=jax.ShapeDtypeStruct((d, n), F32),
        compiler_params=_params("arbitrary"),
        name="gate_weight",
    )(w_ab, w_gate_up)


def _rope_tables(pos):
    half = HEAD_DIM // 2
    inv_freq = ROPE_THETA ** (-jnp.arange(half, dtype=F32) / half)
    ang = pos.astype(F32)[:, None] * inv_freq[None, :]
    cos, sin = jnp.cos(ang), jnp.sin(ang)
    return jnp.concatenate([cos, cos], axis=-1), jnp.concatenate([-sin, sin], axis=-1)


def _topk_select(g, valid, axis):
    n = g.shape[axis]
    idx = lax.broadcasted_iota(jnp.int32, g.shape, axis)
    gm = jnp.where(valid, g, NEG_INF)
    rank = jnp.zeros(g.shape, F32)
    for c in range(n):
        gc = lax.slice_in_dim(gm, c, c + 1, axis=axis)
        beats = (gc > gm) | ((gc == gm) & (idx > c))
        rank = rank + beats.astype(F32)
    return valid & (rank < MOBA_TOPK)


def _moba_prompt_kernel(q_ref, k_ref, v_ref, o_ref, kb_scr, vt_scr, km_scr, sel_scr, s_scr, *, nb, hp, group):
    qb = pl.program_id(2)
    blk = MOBA_BLOCK
    heads = [slice(hh * HEAD_DIM, (hh + 1) * HEAD_DIM) for hh in range(hp)]

    @pl.when(qb == 0)
    def _prep():
        def per_block(j, carry):
            rows = pl.ds(pl.multiple_of(j * blk, blk), blk)
            for hh, cols in enumerate(heads):
                kj = k_ref[rows, cols]
                kb_scr[hh, j] = kj.astype(BF16)
                km_scr[hh, pl.ds(j, 1), :] = jnp.sum(kj, axis=0, keepdims=True) * (1.0 / blk)
                vt_scr[hh, j, 0:HEAD_DIM, :] = v_ref[rows, cols].T.astype(BF16)
                vt_scr[hh, j, HEAD_DIM:HEAD_DIM + ONES_ROWS, :] = jnp.ones((ONES_ROWS, blk), BF16)
            return carry

        lax.fori_loop(0, nb, per_block, 0)

    causal = (lax.broadcasted_iota(jnp.int32, (blk, blk), 0) <= lax.broadcasted_iota(jnp.int32, (blk, blk), 1))
    for hh, cols in enumerate(heads):
        gate = lax.dot_general(km_scr[hh].astype(BF16), q_ref[:, cols], _NT, preferred_element_type=F32)
        cand = lax.broadcasted_iota(jnp.int32, gate.shape, 0)
        sel_scr[hh] = (_topk_select(gate, cand < qb, 0) | (cand == qb)).astype(F32)

    def attend(n_keys):
        first_own = n_keys - group
        for hh, cols in enumerate(heads):
            q = q_ref[:, cols]
            m = jnp.full((1, blk), NEG_INF, F32)
            for j in range(n_keys):
                s = lax.dot_general(kb_scr[hh, j], q, _NT, preferred_element_type=F32)
                s = jnp.where(sel_scr[hh, j:j + 1, :] > 0.0, s, NEG_INF)
                s_scr[hh, j] = s
                bmax = jnp.max(s, axis=0, keepdims=True)
                if j >= first_own:
                    bmax = jnp.where(qb == j, NEG_INF, bmax)
                m = jnp.maximum(m, bmax)
            s_own = jnp.where(causal, s_scr[hh, qb], NEG_INF)
            s_scr[hh, qb] = s_own
            m = jnp.maximum(m, jnp.max(s_own, axis=0, keepdims=True))
            acc = jnp.zeros((HEAD_DIM + ONES_ROWS, blk), F32)
            for j in range(n_keys):
                p = jnp.exp2(s_scr[hh, j] - m)
                acc = acc + jnp.dot(vt_scr[hh, j], p.astype(BF16), preferred_element_type=F32)
            l = acc[HEAD_DIM:HEAD_DIM + 1, :]
            o_ref[:, cols] = (acc[0:HEAD_DIM, :] * (1.0 / l)).T.astype(o_ref.dtype)

    for g in range(nb // group):
        pl.when(qb // group == g)(functools.partial(attend, (g + 1) * group))


def _moba_prompt(q, k, v, n_seq, seq):
    nb = seq // MOBA_BLOCK
    hp = MOBA_HEADS_PER_STEP
    group = min(MOBA_QBLOCK_GROUP, nb)
    assert nb % group == 0
    w = hp * HEAD_DIM
    return pl.pallas_call(
        functools.partial(_moba_prompt_kernel, nb=nb, hp=hp, group=group),
        grid=(n_seq, N_HEADS_A // hp, nb),
        in_specs=[pl.BlockSpec((MOBA_BLOCK, w), lambda n, h, b: (n * nb + b, h)),
                  pl.BlockSpec((seq, w), lambda n, h, b: (n, h)),
                  pl.BlockSpec((seq, w), lambda n, h, b: (n, h))],
        out_specs=pl.BlockSpec((MOBA_BLOCK, w), lambda n, h, b: (n * nb + b, h)),
        out_shape=jax.ShapeDtypeStruct((n_seq * seq, W_A), BF16),
        scratch_shapes=[pltpu.VMEM((hp, nb, MOBA_BLOCK, HEAD_DIM), BF16),
                        pltpu.VMEM((hp, nb, HEAD_DIM + ONES_ROWS, MOBA_BLOCK), BF16),
                        pltpu.VMEM((hp, nb, HEAD_DIM), F32),
                        pltpu.VMEM((hp, nb, MOBA_BLOCK), F32),
                        pltpu.VMEM((hp, nb, MOBA_BLOCK, MOBA_BLOCK), F32)],
        compiler_params=_params("arbitrary", "arbitrary", "arbitrary"),
        name="moba_prompt",
    )(q, k, v)


def _gla_levels():
    m, out = 1, []
    while m < GLA_CHUNK:
        out.append(m)
        m *= 2
    return out


def _gla_cum_matrices():
    c = GLA_CHUNK
    t = np.arange(c)
    tri = (t[None, :] <= t[:, None]).astype(np.float32)
    mats = [tri]
    for m in _gla_levels():
        mats.append(tri[(t // (2 * m)) * (2 * m) + m - 1])
    return np.concatenate(mats, axis=0)


def _split_bf16(x):
    hi = x.astype(BF16)
    return hi, (x - hi.astype(F32)).astype(BF16)


def _gla_prompt_kernel(g_ref, pre_ref, cm_ref, bg_ref, nw_ref, o_ref, s_ref):
    c = GLA_CHUNK

    @pl.when(pl.program_id(1) == 0)
    def _init():
        s_ref[...] = jnp.zeros(s_ref.shape, F32)

    tok = lax.broadcasted_iota(jnp.int32, (c, c), 0)
    col = lax.broadcasted_iota(jnp.int32, (c, c), 1)
    tok_d = lax.broadcasted_iota(jnp.int32, (c, DK_B), 0)
    cm = cm_ref[...]
    for h in range(N_HEADS_B):
        dk = slice(h * DK_B, (h + 1) * DK_B)
        la = _log_sigmoid(pre_ref[:, dk] + bg_ref[:, dk]) * (LOG2E / GATE_TEMP)
        cums = sum(jnp.dot(cm, piece, preferred_element_type=F32) for piece in _split_bf16(la))
        b = cums[0:c]
        q = g_ref[:, dk] * (DK_B ** -0.5)
        k = g_ref[:, W_QB + h * DK_B:W_QB + (h + 1) * DK_B]
        v = g_ref[:, 2 * W_QB + h * DV_B:2 * W_QB + (h + 1) * DV_B]
        rb = g_ref[:, 2 * W_QB + W_VB + h * DV_B:2 * W_QB + W_VB + (h + 1) * DV_B]
        vb = v.astype(BF16)

        attn = lax.dot_general(q.astype(BF16), k.astype(BF16), _NT, preferred_element_type=F32)
        attn = jnp.where(col == tok, attn, 0.0)
        for lvl, m in enumerate(_gla_levels()):
            r = cums[(lvl + 1) * c:(lvl + 2) * c]
            upper = ((tok_d // m) % 2) == 1
            x = b - r
            e = jnp.exp2(jnp.where(upper, x, -x))
            e_up = jnp.where(upper, e, 0.0)
            a_m = lax.dot_general((q * e_up).astype(BF16), (k * (e - e_up)).astype(BF16), _NT,
                                  preferred_element_type=F32)
            attn = attn + jnp.where((tok // (2 * m)) == (col // (2 * m)), a_m, 0.0)

        state = s_ref[0, h]
        o = jnp.dot(attn.astype(BF16), vb, preferred_element_type=F32)
        o = o + jnp.dot((q * jnp.exp2(b)).astype(BF16), state.astype(BF16), preferred_element_type=F32)
        b_last = b[c - 1:c, :]
        kd = k * jnp.exp2(b_last - b)
        kv = jnp.dot(kd.T.astype(BF16), vb, preferred_element_type=F32)
        dcol = b.T[:, c - 1:c]
        s_ref[0, h] = jnp.exp2(dcol) * state + kv

        y = _rms(o, nw_ref[...]) * _silu(rb)
        o_ref[:, h * DV_B:(h + 1) * DV_B] = y.astype(o_ref.dtype)


def _gla_prompt(g, pre, bg, nw, n_seq, seq):
    nc = seq // GLA_CHUNK
    cm = jnp.asarray(_gla_cum_matrices(), BF16)
    return pl.pallas_call(
        _gla_prompt_kernel,
        grid=(n_seq, nc),
        in_specs=[pl.BlockSpec((GLA_CHUNK, W_GLA), lambda n, t: (n * nc + t, 0)),
                  pl.BlockSpec((GLA_CHUNK, W_QB), lambda n, t: (n * nc + t, 0)),
                  pl.BlockSpec(cm.shape, lambda n, t: (0, 0)),
                  pl.BlockSpec((1, W_QB), lambda n, t: (0, 0)),
                  pl.BlockSpec((1, DV_B), lambda n, t: (0, 0))],
        out_specs=[pl.BlockSpec((GLA_CHUNK, W_VB), lambda n, t: (n * nc + t, 0)),
                   pl.BlockSpec((1, N_HEADS_B, DK_B, DV_B), lambda n, t: (n, 0, 0, 0))],
        out_shape=[jax.ShapeDtypeStruct((n_seq * seq, W_VB), BF16),
                   jax.ShapeDtypeStruct((n_seq, N_HEADS_B, DK_B, DV_B), F32)],
        compiler_params=_params("arbitrary", "arbitrary"),
        name="gla_prompt",
    )(g, pre, cm, bg.reshape(1, W_QB), nw.reshape(1, DV_B))


def _gla_step_kernel(qc_ref, kc_ref, prec_ref, v_ref, rb_ref, bgc_ref, nw_ref, s0_ref, o_ref, s_ref):
    for h in range(N_HEADS_B):
        rows = slice(h * DK_B, (h + 1) * DK_B)
        a = jnp.exp(_log_sigmoid(prec_ref[0, rows, :] + bgc_ref[rows, :]) * (1.0 / GATE_TEMP))
        v = v_ref[0, :, h * DV_B:(h + 1) * DV_B]
        s_new = a * s0_ref[0, h] + kc_ref[0, rows, :] * v
        s_ref[0, h] = s_new
        o = jnp.sum((qc_ref[0, rows, :] * (DK_B ** -0.5)) * s_new, axis=0, keepdims=True)
        y = _rms(o, nw_ref[...]) * _silu(rb_ref[0, :, h * DV_B:(h + 1) * DV_B])
        o_ref[0, :, h * DV_B:(h + 1) * DV_B] = y.astype(o_ref.dtype)


def _gla_step(g, pre, bg, nw, s0):
    n = g.shape[0]
    qc = g[:, 0:W_QB].reshape(n, W_QB, 1)
    kc = g[:, W_QB:2 * W_QB].reshape(n, W_QB, 1)
    v = g[:, 2 * W_QB:2 * W_QB + W_VB].reshape(n, 1, W_VB)
    rb = g[:, 2 * W_QB + W_VB:W_GLA].reshape(n, 1, W_VB)
    per_n3 = lambda i: (i, 0, 0)
    fixed = lambda i: (0, 0)
    o, s = pl.pallas_call(
        _gla_step_kernel,
        grid=(n,),
        in_specs=[pl.BlockSpec((1, W_QB, 1), per_n3), pl.BlockSpec((1, W_QB, 1), per_n3),
                  pl.BlockSpec((1, W_QB, 1), per_n3),
                  pl.BlockSpec((1, 1, W_VB), per_n3), pl.BlockSpec((1, 1, W_VB), per_n3),
                  pl.BlockSpec((W_QB, 1), fixed), pl.BlockSpec((1, DV_B), fixed),
                  pl.BlockSpec((1, N_HEADS_B, DK_B, DV_B), lambda i: (i, 0, 0, 0))],
        out_specs=[pl.BlockSpec((1, 1, W_VB), per_n3),
                   pl.BlockSpec((1, N_HEADS_B, DK_B, DV_B), lambda i: (i, 0, 0, 0))],
        out_shape=[jax.ShapeDtypeStruct((n, 1, W_VB), BF16),
                   jax.ShapeDtypeStruct(s0.shape, F32)],
        compiler_params=_params("arbitrary"),
        name="gla_step",
    )(qc, kc, pre.reshape(n, W_QB, 1), v, rb, bg.reshape(W_QB, 1), nw.reshape(1, DV_B), s0)
    return o.reshape(n, W_VB), s


PAGES_PER_STEP = 8


def _page_scores(q_ref, k_refs, o_ref):
    q = q_ref[0].astype(F32)
    shape3 = (PAGE_SIZE, N_HEADS_A, LANES)
    diag = lax.broadcasted_iota(jnp.int32, shape3, 0) == lax.broadcasted_iota(jnp.int32, shape3, 2)
    for p, k_ref in enumerate(k_refs):
        r = jnp.sum(k_ref[0] * q[None], axis=-1, keepdims=True)
        o_ref[0, :, p * PAGE_SIZE:(p + 1) * PAGE_SIZE] = jnp.sum(jnp.where(diag, r, 0.0), axis=0)


def _moba_pick_kernel(s_ref, o_ref, *, nb):
    rows = s_ref.shape[0]
    lane = lax.broadcasted_iota(jnp.int32, (rows, LANES), 1)
    gate = jnp.zeros((rows, LANES), F32)
    for j in range(nb):
        gj = jnp.sum(s_ref[:, j * MOBA_BLOCK:(j + 1) * MOBA_BLOCK], axis=-1, keepdims=True) * (1.0 / MOBA_BLOCK)
        gate = jnp.where(lane == j, gj, gate)
    valid = lane < nb
    gm = jnp.where(valid, gate, NEG_INF)
    rank = jnp.zeros((rows, LANES), F32)
    for c in range(nb):
        gc = gm[:, c:c + 1]
        beats = (gc > gm) | ((gc == gm) & (lane > c))
        rank = rank + beats.astype(F32)
    out = jnp.zeros((rows, LANES), jnp.int32)
    for t in range(MOBA_TOPK):
        pick = jnp.sum(jnp.where(valid & (rank == float(t)), lane, 0), axis=-1, keepdims=True)
        out = jnp.where(lane == t, pick, out)
    o_ref[...] = out


def _moba_pick(scores, nb):
    rows = scores.shape[0]
    return pl.pallas_call(
        functools.partial(_moba_pick_kernel, nb=nb),
        grid=(1,),
        in_specs=[pl.BlockSpec(scores.shape, lambda i: (0, 0))],
        out_specs=pl.BlockSpec((rows, LANES), lambda i: (0, 0)),
        out_shape=jax.ShapeDtypeStruct((rows, LANES), jnp.int32),
        compiler_params=_params("arbitrary"),
        name="moba_pick",
    )(scores)


def _moba_gather_kernel(pt_ref, pick_ref, q_ref, kn_ref, vn_ref, s_ref, *refs):
    v_refs, o_ref = refs[:2 * MOBA_TOPK], refs[2 * MOBA_TOPK]
    row = pl.program_id(0) * N_HEADS_A + pl.program_id(1)
    s_self = jnp.sum(q_ref[0, 0].astype(F32) * kn_ref[0, 0].astype(BF16).astype(F32), axis=-1, keepdims=True)
    s = [s_ref[0, 0, pl.ds(pick_ref[row, t], 1), :] for t in range(MOBA_TOPK)]
    m = s_self
    for st in s:
        m = jnp.maximum(m, jnp.max(st, axis=-1, keepdims=True))
    p_self = jnp.exp(s_self - m)
    l = p_self
    acc = jnp.zeros((8, HEAD_DIM), F32)
    for t, st in enumerate(s):
        p = jnp.exp(st - m)
        l = l + jnp.sum(p, axis=-1, keepdims=True)
        pb = jnp.broadcast_to(p, (8, MOBA_BLOCK)).astype(BF16)
        for half in range(2):
            vpage = v_refs[2 * t + half][0, :, 0, 0, :].astype(BF16)
            acc = acc + jnp.dot(pb[:, half * PAGE_SIZE:(half + 1) * PAGE_SIZE], vpage, preferred_element_type=F32)
    out = acc[0:1] + p_self * vn_ref[0, 0].astype(BF16).astype(F32)
    o_ref[0, 0] = (out * (1.0 / l)).astype(o_ref.dtype)


def _moba_gather(cache_v, page_table, picks, scores, q, k_new, v_new, nb):
    n = page_table.shape[0]
    pages_per_block = MOBA_BLOCK // PAGE_SIZE
    assert pages_per_block == 2
    scores4 = scores.reshape(n, N_HEADS_A, nb, MOBA_BLOCK)
    cache_v5 = cache_v.reshape(cache_v.shape[0], PAGE_SIZE, N_HEADS_A, 1, HEAD_DIM)

    def v_spec(t, half):
        return pl.BlockSpec((1, PAGE_SIZE, 1, 1, HEAD_DIM),
                            lambda i, h, pt, pk: (pt[i, pk[i * N_HEADS_A + h, t] * pages_per_block + half], 0, h, 0, 0))

    return pl.pallas_call(
        _moba_gather_kernel,
        grid_spec=pltpu.PrefetchScalarGridSpec(
            num_scalar_prefetch=2,
            grid=(n, N_HEADS_A),
            in_specs=[pl.BlockSpec((1, 1, 1, HEAD_DIM), lambda i, h, pt, pk: (i, h, 0, 0))] * 3
                     + [pl.BlockSpec((1, 1, nb, MOBA_BLOCK), lambda i, h, pt, pk: (i, h, 0, 0))]
                     + [v_spec(t, half) for t in range(MOBA_TOPK) for half in range(2)],
            out_specs=pl.BlockSpec((1, 1, 1, HEAD_DIM), lambda i, h, pt, pk: (i, h, 0, 0)),
        ),
        out_shape=jax.ShapeDtypeStruct((n, N_HEADS_A, 1, HEAD_DIM), BF16),
        compiler_params=_params("arbitrary", "arbitrary"),
        name="moba_gather",
    )(page_table, picks,
      q.reshape(n, N_HEADS_A, 1, HEAD_DIM), k_new.reshape(n, N_HEADS_A, 1, HEAD_DIM),
      v_new.reshape(n, N_HEADS_A, 1, HEAD_DIM), scores4,
      *([cache_v5] * (2 * MOBA_TOPK)))


def _out_proj_kernel(oa_ref, ob_ref, w_ref, x_ref, g_ref, o_ref, wb_scr):
    @pl.when(pl.program_id(1) == 0)
    def _cast():
        wb_scr[...] = w_ref[...].astype(BF16)

    mix = jnp.dot(oa_ref[...], wb_scr[0:W_A, :], preferred_element_type=F32)
    mix = mix + jnp.dot(ob_ref[...], wb_scr[W_A:W_A + W_VB, :], preferred_element_type=F32)
    o_ref[...] = x_ref[...] + g_ref[...] * mix


def _out_proj(oa, ob, w, layer, x, gate, tm, tn, tiles_per_group):
    m, d = x.shape
    _, r, _ = gate.shape
    return pl.pallas_call(
        _out_proj_kernel,
        grid=(d // tn, m // tm),
        in_specs=[pl.BlockSpec((tm, W_A), lambda j, i: (i, 0)),
                  pl.BlockSpec((tm, W_VB), lambda j, i: (i, 0)),
                  pl.BlockSpec((None, W_A + W_VB, tn), lambda j, i: (layer, 0, j)),
                  pl.BlockSpec((tm, tn), lambda j, i: (i, j)),
                  pl.BlockSpec((None, r, tn), lambda j, i: (i // tiles_per_group, 0, j))],
        out_specs=pl.BlockSpec((tm, tn), lambda j, i: (i, j)),
        out_shape=jax.ShapeDtypeStruct((m, d), F32),
        scratch_shapes=[pltpu.VMEM((W_A + W_VB, tn), BF16)],
        compiler_params=_params("arbitrary", "arbitrary"),
        name="out_proj",
    )(oa, ob, w, x, gate)


def _mlp_scores_kernel(pt_ref, x_ref, sc_ref, sh_ref, g_ref, nw_ref, wu_ref, wd_ref, q_ref, *refs):
    k_refs = refs[:PAGES_PER_STEP]
    o_ref, s_ref, h_scr = refs[PAGES_PER_STEP:]
    _mlp_kernel(x_ref, sc_ref, sh_ref, g_ref, nw_ref, wu_ref, wd_ref, o_ref, h_scr,
                beside_matmuls=functools.partial(_page_scores, q_ref, k_refs, s_ref))


def _mlp_kernel(x_ref, sc_ref, sh_ref, g_ref, nw_ref, wu_ref, wd_ref, o_ref, h_scr, beside_matmuls=None):
    f = pl.program_id(1)
    d = o_ref.shape[1]
    tn = min(d, TN_DENSE)

    @pl.when(f == 0)
    def _first():
        y = _rms(x_ref[...], nw_ref[...])
        h_scr[...] = (y * (1.0 + sc_ref[...]) + sh_ref[...]).astype(BF16)
        o_ref[...] = jnp.zeros(o_ref.shape, F32)

    if beside_matmuls is not None:
        beside_matmuls()
    u = jnp.dot(h_scr[...], wu_ref[...], preferred_element_type=F32)
    r = jnp.maximum(u, 0.0)
    r2 = (r * r).astype(BF16)
    for c0 in range(0, d, tn):
        o_ref[:, c0:c0 + tn] += jnp.dot(r2, wd_ref[:, c0:c0 + tn], preferred_element_type=F32)

    @pl.when(f == pl.num_programs(1) - 1)
    def _last():
        o_ref[...] = x_ref[...] + g_ref[...] * o_ref[...]


def _mlp(x, sc, sh, gate, nw, wu, wd, tm, tf, tiles_per_group):
    m, d = x.shape
    ff = wu.shape[1]
    return pl.pallas_call(
        _mlp_kernel,
        grid=(m // tm, ff // tf),
        in_specs=[pl.BlockSpec((tm, d), lambda i, f: (i, 0)),
                  _mod_spec(sc, tiles_per_group), _mod_spec(sh, tiles_per_group), _mod_spec(gate, tiles_per_group),
                  pl.BlockSpec((1, d), lambda i, f: (0, 0)),
                  pl.BlockSpec((d, tf), lambda i, f: (0, f)),
                  pl.BlockSpec((tf, d), lambda i, f: (f, 0))],
        out_specs=pl.BlockSpec((tm, d), lambda i, f: (i, 0)),
        out_shape=jax.ShapeDtypeStruct((m, d), F32),
        scratch_shapes=[pltpu.VMEM((tm, d), BF16)],
        compiler_params=_params("arbitrary", "arbitrary"),
        name="mlp",
    )(x, sc, sh, gate, nw.reshape(1, d), wu, wd)


def _mlp_with_scores(x, sc, sh, gate, nw, wu, wd, tm, tf, tiles_per_group, cache_k, page_table, q):
    m, d = x.shape
    ff = wu.shape[1]
    n, n_pages = page_table.shape
    nf = ff // tf
    groups = n_pages // PAGES_PER_STEP
    assert (m // tm) * nf == n * groups and n_pages % PAGES_PER_STEP == 0

    def sample(i, f):
        return (i * nf + f) // groups

    def group(i, f):
        return (i * nf + f) % groups

    def page_spec(p):
        return pl.BlockSpec((1, PAGE_SIZE, N_HEADS_A, HEAD_DIM),
                            lambda i, f, pt: (pt[sample(i, f), group(i, f) * PAGES_PER_STEP + p], 0, 0, 0))

    def mod_spec(mod):
        _, r, w = mod.shape
        return pl.BlockSpec((None, r, w), lambda i, f, pt: (i // tiles_per_group, 0, 0))

    y, scores = pl.pallas_call(
        _mlp_scores_kernel,
        grid_spec=pltpu.PrefetchScalarGridSpec(
            num_scalar_prefetch=1,
            grid=(m // tm, nf),
            in_specs=[pl.BlockSpec((tm, d), lambda i, f, pt: (i, 0)),
                      mod_spec(sc), mod_spec(sh), mod_spec(gate),
                      pl.BlockSpec((1, d), lambda i, f, pt: (0, 0)),
                      pl.BlockSpec((d, tf), lambda i, f, pt: (0, f)),
                      pl.BlockSpec((tf, d), lambda i, f, pt: (f, 0)),
                      pl.BlockSpec((1, N_HEADS_A, HEAD_DIM), lambda i, f, pt: (sample(i, f), 0, 0))]
                     + [page_spec(p) for p in range(PAGES_PER_STEP)],
            out_specs=[pl.BlockSpec((tm, d), lambda i, f, pt: (i, 0)),
                       pl.BlockSpec((1, N_HEADS_A, PAGES_PER_STEP * PAGE_SIZE),
                                    lambda i, f, pt: (sample(i, f), 0, group(i, f)))],
            scratch_shapes=[pltpu.VMEM((tm, d), BF16)],
        ),
        out_shape=[jax.ShapeDtypeStruct((m, d), F32),
                   jax.ShapeDtypeStruct((n, N_HEADS_A, n_pages * PAGE_SIZE), F32)],
        compiler_params=_params("arbitrary", "arbitrary"),
        name="mlp_scores",
    )(page_table, x, sc, sh, gate, nw.reshape(1, d), wu, wd, q.reshape(n, N_HEADS_A, HEAD_DIM),
      *([cache_k] * PAGES_PER_STEP))
    return y, scores


def _dense_in(x2, mods, norm_w, w_in_t, layer, w_gate, rope, qk_norm_w, q_scale, tm_row, tm, tiles_per_group):
    h = _norm_mod(x2, mods["sc1"], mods["sh1"], norm_w, tm_row, tiles_per_group)
    tn = TN_DENSE
    cos, sin = rope
    pos_tiles = cos.shape[0] // tm
    q = _proj_rope(h, w_in_t, layer, 0, cos, sin, qk_norm_w[0], tm, pos_tiles, q_scale, True, BF16)
    k = _proj_rope(h, w_in_t, layer, 1, cos, sin, qk_norm_w[1], tm, pos_tiles, 1.0, True, F32)
    v = _proj(h, w_in_t, layer, 2 * W_A // tn, W_A, tm, tn, transposed=True)
    g = _proj(h, w_in_t, layer, 3 * W_A // tn, W_GLA, tm, tn, transposed=True)
    pre = _proj(h, w_gate, 0, 0, W_QB, tm, W_QB)
    return q, k, v, g, pre


def kernel(x_prompt, x_sample, c_prompt, c_sample, cache_k, cache_v, state_gla, page_table, w_ada, b_ada, norm_mix_w, w_in, q_norm_w, k_norm_w, w_gate_up, b_gate, gla_norm_w, w_out, norm_ffn_w, w_up, w_down):
    n_p, s_p, d = x_prompt.shape
    n_s, s_s, _ = x_sample.shape
    depth = w_ada.shape[0]
    assert depth == 1 and s_s == 1 and d == D_MODEL
    n_pages = page_table.shape[1]
    past_len = n_pages * PAGE_SIZE
    nb_past = past_len // MOBA_BLOCK
    assert past_len % MOBA_BLOCK == 0 and nb_past >= MOBA_TOPK
    l = 0
    tm_row = min(TM_ROWWISE, s_p)
    tm = min(TM_DENSE, s_p)

    rows = n_p + n_s
    rows_pad = -(-rows // 8) * 8
    c_all = jnp.concatenate([c_prompt, c_sample, jnp.zeros((rows_pad - rows, d), F32)], axis=0)
    ada = _ada(c_all, w_ada[l], b_ada[l])
    names = ("sh1", "sc1", "g1", "sh2", "sc2", "g2")
    mods_p = {nm: ada[0:n_p, i * d:(i + 1) * d].reshape(n_p, 1, d) for i, nm in enumerate(names)}
    mods_s = {nm: ada[n_p:rows, i * d:(i + 1) * d].reshape(1, n_s, d) for i, nm in enumerate(names)}

    w_in_t = jnp.swapaxes(w_in, 1, 2)
    w_gate = _gate_weight(w_in[l, :, 3 * W_A + W_GLA:], w_gate_up[l])[None]
    w_up_b = w_up[l].astype(BF16)
    w_down_b = w_down[l].astype(BF16)
    scale = HEAD_DIM ** -0.5
    qk_norm_w = (q_norm_w[l], k_norm_w[l])

    xp = x_prompt.reshape(n_p * s_p, d)
    rope_p = _rope_tables(jnp.arange(s_p, dtype=jnp.int32))
    q_p, k_p, v_p, g, pre = _dense_in(xp, mods_p, norm_mix_w[l], w_in_t, l, w_gate, rope_p, qk_norm_w,
                                      scale * LOG2E, tm_row, tm, s_p // tm_row)
    oa_p = _moba_prompt(q_p, k_p, v_p, n_p, s_p)
    ob_p, gla_p = _gla_prompt(g, pre, b_gate[l], gla_norm_w[l], n_p, s_p)
    x1_p = _out_proj(oa_p, ob_p, w_out, l, xp, mods_p["g1"], tm, TN_DENSE, s_p // tm)

    xs = x_sample.reshape(n_s, d)
    rope_s = _rope_tables(jnp.full((n_s,), past_len, jnp.int32))
    q_s, k_s, v_s, g, pre = _dense_in(xs, mods_s, norm_mix_w[l], w_in_t, l, w_gate, rope_s, qk_norm_w,
                                      scale, n_s, n_s, 1)
    tm_mlp = min(TM_MLP, s_p)
    y_p, scores = _mlp_with_scores(x1_p, mods_p["sc2"], mods_p["sh2"], mods_p["g2"], norm_ffn_w[l],
                                   w_up_b, w_down_b, tm_mlp, TF_MLP, s_p // tm_mlp, cache_k[l], page_table, q_s)
    picks = _moba_pick(scores.reshape(n_s * N_HEADS_A, past_len), nb_past)
    oa_s = _moba_gather(cache_v[l], page_table, picks, scores, q_s, k_s, v_s, nb_past).reshape(n_s, W_A)
    ob_s, gla_s = _gla_step(g, pre, b_gate[l], gla_norm_w[l], state_gla[l])
    x1_s = _out_proj(oa_s, ob_s, w_out, l, xs, mods_s["g1"], n_s, TN_DENSE, 1)
    y_s = _mlp(x1_s, mods_s["sc2"], mods_s["sh2"], mods_s["g2"], norm_ffn_w[l], w_up_b, w_down_b,
               n_s, TF_MLP, 1)

    return (y_p.reshape(n_p, s_p, d), y_s.reshape(n_s, s_s, d),
            k_p.reshape(1, n_p, s_p, N_HEADS_A, HEAD_DIM), v_p.reshape(1, n_p, s_p, N_HEADS_A, HEAD_DIM),
            gla_p[None],
            k_s.reshape(1, n_s, s_s, N_HEADS_A, HEAD_DIM), v_s.reshape(1, n_s, s_s, N_HEADS_A, HEAD_DIM),
            gla_s[None])
```

```python
import functools

import numpy as np
import jax
import jax.numpy as jnp
from jax import lax
from jax.experimental import pallas as pl
from jax.experimental.pallas import tpu as pltpu

D_MODEL = 2048
PAGE_SIZE = 128
HEAD_DIM = 128
N_HEADS_A = 8
N_HEADS_B = 4
DK_B = 128
DV_B = 256
GATE_RANK = 16
GATE_TEMP = 16.0
MOBA_BLOCK = 256
MOBA_TOPK = 3
GLA_CHUNK = 128
D_FF = 4 * D_MODEL
ROPE_THETA = 10000.0
EPS = 1e-6

W_A = N_HEADS_A * HEAD_DIM
W_QB = N_HEADS_B * DK_B
W_VB = N_HEADS_B * DV_B
W_GLA = 2 * W_QB + 2 * W_VB
LANES = 128

F32 = jnp.float32
BF16 = jnp.bfloat16
NEG_INF = float("-inf")
VMEM_LIMIT = 56 * 1024 * 1024

TM_ROWWISE = 512
TM_DENSE = 1024
TN_DENSE = 1024
TM_MLP = 1024
TF_MLP_PROMPT = 512
TF_MLP = 512
GLA_STEP_SAMPLES = 4
ROPE_SUB_ROWS = 256
MOBA_HEADS_PER_STEP = 2
MOBA_QBLOCK_GROUP = 2
ONES_ROWS = 16
LOG2E = 1.4426950408889634

_NT = (((1,), (1,)), ((), ()))


def _params(*sem):
    return pltpu.CompilerParams(dimension_semantics=sem, vmem_limit_bytes=VMEM_LIMIT)


def _rms(x, w):
    return x * lax.rsqrt(jnp.mean(x * x, axis=-1, keepdims=True) + EPS) * w


def _log_sigmoid(x):
    return jnp.minimum(x, 0.0) - jnp.log(1.0 + jnp.exp(-jnp.abs(x)))


def _silu(x):
    return x / (1.0 + jnp.exp(-x))


def _ada_kernel(c_ref, w_ref, b_ref, o_ref):
    a = _silu(c_ref[...]).astype(BF16)
    o_ref[...] = jnp.dot(a, w_ref[...].astype(BF16), preferred_element_type=F32) + b_ref[...]


def _ada(c, w, b, tn=512):
    m, d = c.shape
    n = w.shape[1]
    return pl.pallas_call(
        _ada_kernel,
        grid=(n // tn,),
        in_specs=[pl.BlockSpec((m, d), lambda j: (0, 0)),
                  pl.BlockSpec((d, tn), lambda j: (0, j)),
                  pl.BlockSpec((1, tn), lambda j: (0, j))],
        out_specs=pl.BlockSpec((m, tn), lambda j: (0, j)),
        out_shape=jax.ShapeDtypeStruct((m, n), F32),
        compiler_params=_params("arbitrary"),
        name="ada_proj",
    )(c, w, b.reshape(1, n))


def _mod_spec(mod, tiles_per_group):
    _, r, w = mod.shape
    return pl.BlockSpec((None, r, w), lambda i, *_: (i // tiles_per_group, 0, 0))


def _norm_mod_kernel(x_ref, sc_ref, sh_ref, w_ref, o_ref):
    y = _rms(x_ref[...], w_ref[...])
    o_ref[...] = (y * (1.0 + sc_ref[...]) + sh_ref[...]).astype(o_ref.dtype)


def _norm_mod(x, sc, sh, w, tm, tiles_per_group):
    m, d = x.shape
    return pl.pallas_call(
        _norm_mod_kernel,
        grid=(m // tm,),
        in_specs=[pl.BlockSpec((tm, d), lambda i: (i, 0)),
                  _mod_spec(sc, tiles_per_group), _mod_spec(sh, tiles_per_group),
                  pl.BlockSpec((1, d), lambda i: (0, 0))],
        out_specs=pl.BlockSpec((tm, d), lambda i: (i, 0)),
        out_shape=jax.ShapeDtypeStruct((m, d), BF16),
        compiler_params=_params("arbitrary"),
        name="norm_mod",
    )(x, sc, sh, w.reshape(1, d))


def _weight_spec(k, tn, layer, col_block0, transposed):
    if transposed:
        return pl.BlockSpec((None, tn, k), lambda j, *_: (layer, j + col_block0, 0))
    return pl.BlockSpec((None, k, tn), lambda j, *_: (layer, 0, j + col_block0))


def _weight_dot(a, wb, transposed):
    if transposed:
        return lax.dot_general(a, wb, _NT, preferred_element_type=F32)
    return jnp.dot(a, wb, preferred_element_type=F32)


def _proj_kernel(a_ref, w_ref, o_ref, wb_scr, *, transposed):
    @pl.when(pl.program_id(1) == 0)
    def _cast():
        wb_scr[...] = w_ref[...].astype(BF16)

    o_ref[...] = _weight_dot(a_ref[...], wb_scr[...], transposed).astype(o_ref.dtype)


def _proj(a, w, layer, col_block0, n_out, tm, tn, transposed=False, out_dtype=F32):
    m, k = a.shape
    w_spec = _weight_spec(k, tn, layer, col_block0, transposed)
    return pl.pallas_call(
        functools.partial(_proj_kernel, transposed=transposed),
        grid=(n_out // tn, m // tm),
        in_specs=[pl.BlockSpec((tm, k), lambda j, i: (i, 0)), w_spec],
        out_specs=pl.BlockSpec((tm, tn), lambda j, i: (i, j)),
        out_shape=jax.ShapeDtypeStruct((m, n_out), out_dtype),
        scratch_shapes=[pltpu.VMEM(w_spec.block_shape[1:], BF16)],
        compiler_params=_params("arbitrary", "arbitrary"),
        name="proj",
    )(a, w)


def _rope_heads(y, cos, sin, w, scale):
    out = []
    for h in range(y.shape[1] // HEAD_DIM):
        x = _rms(y[:, h * HEAD_DIM:(h + 1) * HEAD_DIM], w)
        r = x * cos + pltpu.roll(x, HEAD_DIM // 2, 1) * sin
        out.append(r * scale if scale != 1.0 else r)
    return out


def _proj_rope_kernel(a_ref, w_ref, cos_ref, sin_ref, nw_ref, o_ref, wb_scr, *, scale, rows, transposed):
    @pl.when(pl.program_id(0) == 0)
    def _cast():
        wb_scr[...] = w_ref[...].astype(BF16)

    for r0 in range(0, a_ref.shape[0], rows):
        rs = slice(r0, r0 + rows)
        y = _weight_dot(a_ref[rs, :], wb_scr[...], transposed)
        for h, r in enumerate(_rope_heads(y, cos_ref[rs, :], sin_ref[rs, :], nw_ref[...], scale)):
            o_ref[rs, h * HEAD_DIM:(h + 1) * HEAD_DIM] = r.astype(o_ref.dtype)


def _proj_rope(a, w, layer, col_block, cos, sin, norm_w, tm, pos_tiles, scale, transposed, out_dtype):
    m, k = a.shape
    if transposed:
        w_spec = pl.BlockSpec((None, W_A, k), lambda i: (layer, col_block, 0))
    else:
        w_spec = pl.BlockSpec((None, k, W_A), lambda i: (layer, 0, col_block))
    return pl.pallas_call(
        functools.partial(_proj_rope_kernel, scale=scale, rows=min(tm, ROPE_SUB_ROWS), transposed=transposed),
        grid=(m // tm,),
        in_specs=[pl.BlockSpec((tm, k), lambda i: (i, 0)),
                  w_spec,
                  pl.BlockSpec((tm, HEAD_DIM), lambda i: (i % pos_tiles, 0)),
                  pl.BlockSpec((tm, HEAD_DIM), lambda i: (i % pos_tiles, 0)),
                  pl.BlockSpec((1, HEAD_DIM), lambda i: (0, 0))],
        out_specs=pl.BlockSpec((tm, W_A), lambda i: (i, 0)),
        out_shape=jax.ShapeDtypeStruct((m, W_A), out_dtype),
        scratch_shapes=[pltpu.VMEM(w_spec.block_shape[1:], BF16)],
        compiler_params=_params("arbitrary"),
        name="proj_rope",
    )(a, w, cos, sin, norm_w.reshape(1, HEAD_DIM))


def _gate_weight_kernel(a_ref, b_ref, o_ref):
    o_ref[...] = jnp.dot(a_ref[...], b_ref[...], preferred_element_type=F32, precision=lax.Precision.HIGHEST)


def _gate_weight(w_ab, w_gate_up):
    d, r = w_ab.shape
    n = w_gate_up.shape[1]
    return pl.pallas_call(
        _gate_weight_kernel,
        grid=(1,),
        in_specs=[pl.BlockSpec((d, r), lambda i: (0, 0)), pl.BlockSpec((r, n), lambda i: (0, 0))],
        out_specs=pl.BlockSpec((d, n), lambda i: (0, 0)),
        out_shape=jax.ShapeDtypeStruct((d, n), F32),
        compiler_params=_params("arbitrary"),
        name="gate_weight",
    )(w_ab, w_gate_up)


def _rope_tables(pos):
    half = HEAD_DIM // 2
    inv_freq = ROPE_THETA ** (-jnp.arange(half, dtype=F32) / half)
    ang = pos.astype(F32)[:, None] * inv_freq[None, :]
    cos, sin = jnp.cos(ang), jnp.sin(ang)
    return jnp.concatenate([cos, cos], axis=-1), jnp.concatenate([-sin, sin], axis=-1)


def _topk_select(g, valid, axis):
    n = g.shape[axis]
    idx = lax.broadcasted_iota(jnp.int32, g.shape, axis)
    gm = jnp.where(valid, g, NEG_INF)
    rank = jnp.zeros(g.shape, F32)
    for c in range(n):
        gc = lax.slice_in_dim(gm, c, c + 1, axis=axis)
        beats = (gc > gm) | ((gc == gm) & (idx > c))
        rank = rank + beats.astype(F32)
    return valid & (rank < MOBA_TOPK)


def _moba_prompt_kernel(q_ref, k_ref, v_ref, o_ref, kb_scr, vt_scr, km_scr, sel_scr, s_scr, *, nb, hp, group):
    qb = pl.program_id(2)
    blk = MOBA_BLOCK
    heads = [slice(hh * HEAD_DIM, (hh + 1) * HEAD_DIM) for hh in range(hp)]

    @pl.when(qb == 0)
    def _prep():
        def per_block(j, carry):
            rows = pl.ds(pl.multiple_of(j * blk, blk), blk)
            for hh, cols in enumerate(heads):
                kj = k_ref[rows, cols]
                kb_scr[hh, j] = kj.astype(BF16)
                km_scr[hh, pl.ds(j, 1), :] = jnp.sum(kj, axis=0, keepdims=True) * (1.0 / blk)
                vt_scr[hh, j, 0:HEAD_DIM, :] = v_ref[rows, cols].T.astype(BF16)
                vt_scr[hh, j, HEAD_DIM:HEAD_DIM + ONES_ROWS, :] = jnp.ones((ONES_ROWS, blk), BF16)
            return carry

        lax.fori_loop(0, nb, per_block, 0)

    causal = (lax.broadcasted_iota(jnp.int32, (blk, blk), 0) <= lax.broadcasted_iota(jnp.int32, (blk, blk), 1))
    for hh, cols in enumerate(heads):
        gate = lax.dot_general(km_scr[hh].astype(BF16), q_ref[:, cols], _NT, preferred_element_type=F32)
        cand = lax.broadcasted_iota(jnp.int32, gate.shape, 0)
        sel_scr[hh] = (_topk_select(gate, cand < qb, 0) | (cand == qb)).astype(F32)

    def attend(n_keys):
        first_own = n_keys - group
        for hh, cols in enumerate(heads):
            q = q_ref[:, cols]
            m = jnp.full((1, blk), NEG_INF, F32)
            for j in range(n_keys):
                s = lax.dot_general(kb_scr[hh, j], q, _NT, preferred_element_type=F32)
                s = jnp.where(sel_scr[hh, j:j + 1, :] > 0.0, s, NEG_INF)
                s_scr[hh, j] = s
                bmax = jnp.max(s, axis=0, keepdims=True)
                if j >= first_own:
                    bmax = jnp.where(qb == j, NEG_INF, bmax)
                m = jnp.maximum(m, bmax)
            s_own = jnp.where(causal, s_scr[hh, qb], NEG_INF)
            s_scr[hh, qb] = s_own
            m = jnp.maximum(m, jnp.max(s_own, axis=0, keepdims=True))
            acc = jnp.zeros((HEAD_DIM + ONES_ROWS, blk), F32)
            for j in range(n_keys):
                p = jnp.exp2(s_scr[hh, j] - m)
                acc = acc + jnp.dot(vt_scr[hh, j], p.astype(BF16), preferred_element_type=F32)
            l = acc[HEAD_DIM:HEAD_DIM + 1, :]
            o_ref[:, cols] = (acc[0:HEAD_DIM, :] * (1.0 / l)).T.astype(o_ref.dtype)

    for g in range(nb // group):
        pl.when(qb // group == g)(functools.partial(attend, (g + 1) * group))


def _moba_prompt(q, k, v, n_seq, seq):
    nb = seq // MOBA_BLOCK
    hp = MOBA_HEADS_PER_STEP
    group = min(MOBA_QBLOCK_GROUP, nb)
    assert nb % group == 0
    w = hp * HEAD_DIM
    return pl.pallas_call(
        functools.partial(_moba_prompt_kernel, nb=nb, hp=hp, group=group),
        grid=(n_seq, N_HEADS_A // hp, nb),
        in_specs=[pl.BlockSpec((MOBA_BLOCK, w), lambda n, h, b: (n * nb + b, h)),
                  pl.BlockSpec((seq, w), lambda n, h, b: (n, h)),
                  pl.BlockSpec((seq, w), lambda n, h, b: (n, h))],
        out_specs=pl.BlockSpec((MOBA_BLOCK, w), lambda n, h, b: (n * nb + b, h)),
        out_shape=jax.ShapeDtypeStruct((n_seq * seq, W_A), BF16),
        scratch_shapes=[pltpu.VMEM((hp, nb, MOBA_BLOCK, HEAD_DIM), BF16),
                        pltpu.VMEM((hp, nb, HEAD_DIM + ONES_ROWS, MOBA_BLOCK), BF16),
                        pltpu.VMEM((hp, nb, HEAD_DIM), F32),
                        pltpu.VMEM((hp, nb, MOBA_BLOCK), F32),
                        pltpu.VMEM((hp, nb, MOBA_BLOCK, MOBA_BLOCK), F32)],
        compiler_params=_params("arbitrary", "arbitrary", "arbitrary"),
        name="moba_prompt",
    )(q, k, v)


def _gla_levels():
    m, out = 1, []
    while m < GLA_CHUNK:
        out.append(m)
        m *= 2
    return out


def _gla_cum_matrices():
    c = GLA_CHUNK
    t = np.arange(c)
    tri = (t[None, :] <= t[:, None]).astype(np.float32)
    mats = [tri]
    for m in _gla_levels():
        mats.append(tri[(t // (2 * m)) * (2 * m) + m - 1])
    return np.concatenate(mats, axis=0)


def _split_bf16(x):
    hi = x.astype(BF16)
    return hi, (x - hi.astype(F32)).astype(BF16)


def _gla_prompt_kernel(g_ref, pre_ref, cm_ref, bg_ref, nw_ref, o_ref, s_ref):
    c = GLA_CHUNK

    @pl.when(pl.program_id(1) == 0)
    def _init():
        s_ref[...] = jnp.zeros(s_ref.shape, F32)

    tok = lax.broadcasted_iota(jnp.int32, (c, c), 0)
    col = lax.broadcasted_iota(jnp.int32, (c, c), 1)
    tok_d = lax.broadcasted_iota(jnp.int32, (c, DK_B), 0)
    cm = cm_ref[...]
    for h in range(N_HEADS_B):
        dk = slice(h * DK_B, (h + 1) * DK_B)
        la = _log_sigmoid(pre_ref[:, dk] + bg_ref[:, dk]) * (LOG2E / GATE_TEMP)
        cums = sum(jnp.dot(cm, piece, preferred_element_type=F32) for piece in _split_bf16(la))
        b = cums[0:c]
        q = g_ref[:, dk] * (DK_B ** -0.5)
        k = g_ref[:, W_QB + h * DK_B:W_QB + (h + 1) * DK_B]
        v = g_ref[:, 2 * W_QB + h * DV_B:2 * W_QB + (h + 1) * DV_B]
        rb = g_ref[:, 2 * W_QB + W_VB + h * DV_B:2 * W_QB + W_VB + (h + 1) * DV_B]
        vb = v.astype(BF16)

        attn = lax.dot_general(q.astype(BF16), k.astype(BF16), _NT, preferred_element_type=F32)
        attn = jnp.where(col == tok, attn, 0.0)
        for lvl, m in enumerate(_gla_levels()):
            r = cums[(lvl + 1) * c:(lvl + 2) * c]
            upper = ((tok_d // m) % 2) == 1
            x = b - r
            e = jnp.exp2(jnp.where(upper, x, -x))
            e_up = jnp.where(upper, e, 0.0)
            a_m = lax.dot_general((q * e_up).astype(BF16), (k * (e - e_up)).astype(BF16), _NT,
                                  preferred_element_type=F32)
            attn = attn + jnp.where((tok // (2 * m)) == (col // (2 * m)), a_m, 0.0)

        state = s_ref[0, h]
        o = jnp.dot(attn.astype(BF16), vb, preferred_element_type=F32)
        o = o + jnp.dot((q * jnp.exp2(b)).astype(BF16), state.astype(BF16), preferred_element_type=F32)
        b_last = b[c - 1:c, :]
        kd = k * jnp.exp2(b_last - b)
        kv = jnp.dot(kd.T.astype(BF16), vb, preferred_element_type=F32)
        dcol = b.T[:, c - 1:c]
        s_ref[0, h] = jnp.exp2(dcol) * state + kv

        y = _rms(o, nw_ref[...]) * _silu(rb)
        o_ref[:, h * DV_B:(h + 1) * DV_B] = y.astype(o_ref.dtype)


def _gla_prompt(g, pre, bg, nw, n_seq, seq):
    nc = seq // GLA_CHUNK
    cm = jnp.asarray(_gla_cum_matrices(), BF16)
    return pl.pallas_call(
        _gla_prompt_kernel,
        grid=(n_seq, nc),
        in_specs=[pl.BlockSpec((GLA_CHUNK, W_GLA), lambda n, t: (n * nc + t, 0)),
                  pl.BlockSpec((GLA_CHUNK, W_QB), lambda n, t: (n * nc + t, 0)),
                  pl.BlockSpec(cm.shape, lambda n, t: (0, 0)),
                  pl.BlockSpec((1, W_QB), lambda n, t: (0, 0)),
                  pl.BlockSpec((1, DV_B), lambda n, t: (0, 0))],
        out_specs=[pl.BlockSpec((GLA_CHUNK, W_VB), lambda n, t: (n * nc + t, 0)),
                   pl.BlockSpec((1, N_HEADS_B, DK_B, DV_B), lambda n, t: (n, 0, 0, 0))],
        out_shape=[jax.ShapeDtypeStruct((n_seq * seq, W_VB), BF16),
                   jax.ShapeDtypeStruct((n_seq, N_HEADS_B, DK_B, DV_B), F32)],
        compiler_params=_params("arbitrary", "arbitrary"),
        name="gla_prompt",
    )(g, pre, cm, bg.reshape(1, W_QB), nw.reshape(1, DV_B))


def _gla_step_kernel(qc_ref, kc_ref, prec_ref, v_ref, rb_ref, bgc_ref, nw_ref, s0_ref, o_ref, s_ref):
    for i in range(s0_ref.shape[0]):
        for h in range(N_HEADS_B):
            rows = slice(h * DK_B, (h + 1) * DK_B)
            a = jnp.exp(_log_sigmoid(prec_ref[i, rows, :] + bgc_ref[rows, :]) * (1.0 / GATE_TEMP))
            v = v_ref[i, :, h * DV_B:(h + 1) * DV_B]
            s_new = a * s0_ref[i, h] + kc_ref[i, rows, :] * v
            s_ref[i, h] = s_new
            o = jnp.sum((qc_ref[i, rows, :] * (DK_B ** -0.5)) * s_new, axis=0, keepdims=True)
            y = _rms(o, nw_ref[...]) * _silu(rb_ref[i, :, h * DV_B:(h + 1) * DV_B])
            o_ref[i, :, h * DV_B:(h + 1) * DV_B] = y.astype(o_ref.dtype)


def _gla_step(g, pre, bg, nw, s0):
    n = g.shape[0]
    sb = GLA_STEP_SAMPLES if n % GLA_STEP_SAMPLES == 0 else 1
    qc = g[:, 0:W_QB].reshape(n, W_QB, 1)
    kc = g[:, W_QB:2 * W_QB].reshape(n, W_QB, 1)
    v = g[:, 2 * W_QB:2 * W_QB + W_VB].reshape(n, 1, W_VB)
    rb = g[:, 2 * W_QB + W_VB:W_GLA].reshape(n, 1, W_VB)
    per_n3 = lambda i: (i, 0, 0)
    fixed = lambda i: (0, 0)
    o, s = pl.pallas_call(
        _gla_step_kernel,
        grid=(n // sb,),
        in_specs=[pl.BlockSpec((sb, W_QB, 1), per_n3), pl.BlockSpec((sb, W_QB, 1), per_n3),
                  pl.BlockSpec((sb, W_QB, 1), per_n3),
                  pl.BlockSpec((sb, 1, W_VB), per_n3), pl.BlockSpec((sb, 1, W_VB), per_n3),
                  pl.BlockSpec((W_QB, 1), fixed), pl.BlockSpec((1, DV_B), fixed),
                  pl.BlockSpec((sb, N_HEADS_B, DK_B, DV_B), lambda i: (i, 0, 0, 0))],
        out_specs=[pl.BlockSpec((sb, 1, W_VB), per_n3),
                   pl.BlockSpec((sb, N_HEADS_B, DK_B, DV_B), lambda i: (i, 0, 0, 0))],
        out_shape=[jax.ShapeDtypeStruct((n, 1, W_VB), BF16),
                   jax.ShapeDtypeStruct(s0.shape, F32)],
        compiler_params=_params("arbitrary"),
        name="gla_step",
    )(qc, kc, pre.reshape(n, W_QB, 1), v, rb, bg.reshape(W_QB, 1), nw.reshape(1, DV_B), s0)
    return o.reshape(n, W_VB), s


PAGES_PER_STEP = 16


def _page_scores(q_ref, k_refs, o_ref):
    q = q_ref[0].astype(F32)
    shape3 = (PAGE_SIZE, N_HEADS_A, LANES)
    diag = lax.broadcasted_iota(jnp.int32, shape3, 0) == lax.broadcasted_iota(jnp.int32, shape3, 2)
    for p, k_ref in enumerate(k_refs):
        r = jnp.sum(k_ref[0] * q[None], axis=-1, keepdims=True)
        o_ref[0, :, p * PAGE_SIZE:(p + 1) * PAGE_SIZE] = jnp.sum(jnp.where(diag, r, 0.0), axis=0)


def _moba_pick_kernel(s_ref, o_ref, *, nb):
    rows = s_ref.shape[0]
    lane = lax.broadcasted_iota(jnp.int32, (rows, LANES), 1)
    gate = jnp.zeros((rows, LANES), F32)
    for j in range(nb):
        gj = jnp.sum(s_ref[:, j * MOBA_BLOCK:(j + 1) * MOBA_BLOCK], axis=-1, keepdims=True) * (1.0 / MOBA_BLOCK)
        gate = jnp.where(lane == j, gj, gate)
    valid = lane < nb
    gm = jnp.where(valid, gate, NEG_INF)
    rank = jnp.zeros((rows, LANES), F32)
    for c in range(nb):
        gc = gm[:, c:c + 1]
        beats = (gc > gm) | ((gc == gm) & (lane > c))
        rank = rank + beats.astype(F32)
    out = jnp.zeros((rows, LANES), jnp.int32)
    for t in range(MOBA_TOPK):
        pick = jnp.sum(jnp.where(valid & (rank == float(t)), lane, 0), axis=-1, keepdims=True)
        out = jnp.where(lane == t, pick, out)
    o_ref[...] = out


def _moba_pick(scores, nb):
    rows = scores.shape[0]
    return pl.pallas_call(
        functools.partial(_moba_pick_kernel, nb=nb),
        grid=(1,),
        in_specs=[pl.BlockSpec(scores.shape, lambda i: (0, 0))],
        out_specs=pl.BlockSpec((rows, LANES), lambda i: (0, 0)),
        out_shape=jax.ShapeDtypeStruct((rows, LANES), jnp.int32),
        compiler_params=_params("arbitrary"),
        name="moba_pick",
    )(scores)


def _moba_gather_kernel(pt_ref, pick_ref, q_ref, kn_ref, vn_ref, s_ref, *refs):
    v_refs, o_ref = refs[:2 * MOBA_TOPK], refs[2 * MOBA_TOPK]
    row = pl.program_id(0) * N_HEADS_A + pl.program_id(1)
    s_self = jnp.sum(q_ref[0, 0].astype(F32) * kn_ref[0, 0].astype(BF16).astype(F32), axis=-1, keepdims=True)
    s = [s_ref[0, 0, pl.ds(pick_ref[row, t], 1), :] for t in range(MOBA_TOPK)]
    m = s_self
    for st in s:
        m = jnp.maximum(m, jnp.max(st, axis=-1, keepdims=True))
    p_self = jnp.exp(s_self - m)
    l = p_self
    acc = jnp.zeros((8, HEAD_DIM), F32)
    for t, st in enumerate(s):
        p = jnp.exp(st - m)
        l = l + jnp.sum(p, axis=-1, keepdims=True)
        pb = jnp.broadcast_to(p, (8, MOBA_BLOCK)).astype(BF16)
        for half in range(2):
            vpage = v_refs[2 * t + half].reshape(PAGE_SIZE, HEAD_DIM)[...].astype(BF16)
            acc = acc + jnp.dot(pb[:, half * PAGE_SIZE:(half + 1) * PAGE_SIZE], vpage, preferred_element_type=F32)
    out = acc[0:1] + p_self * vn_ref[0, 0].astype(BF16).astype(F32)
    o_ref[0, 0] = (out * (1.0 / l)).astype(o_ref.dtype)


def _moba_gather(cache_v, page_table, picks, scores, q, k_new, v_new, nb):
    n = page_table.shape[0]
    pages_per_block = MOBA_BLOCK // PAGE_SIZE
    assert pages_per_block == 2
    scores4 = scores.reshape(n, N_HEADS_A, nb, MOBA_BLOCK)
    cache_v5 = cache_v.reshape(cache_v.shape[0], PAGE_SIZE, N_HEADS_A, 1, HEAD_DIM)

    def v_spec(t, half):
        return pl.BlockSpec((1, PAGE_SIZE, 1, 1, HEAD_DIM),
                            lambda i, h, pt, pk: (pt[i, pk[i * N_HEADS_A + h, t] * pages_per_block + half], 0, h, 0, 0))

    return pl.pallas_call(
        _moba_gather_kernel,
        grid_spec=pltpu.PrefetchScalarGridSpec(
            num_scalar_prefetch=2,
            grid=(n, N_HEADS_A),
            in_specs=[pl.BlockSpec((1, 1, 1, HEAD_DIM), lambda i, h, pt, pk: (i, h, 0, 0))] * 3
                     + [pl.BlockSpec((1, 1, nb, MOBA_BLOCK), lambda i, h, pt, pk: (i, h, 0, 0))]
                     + [v_spec(t, half) for t in range(MOBA_TOPK) for half in range(2)],
            out_specs=pl.BlockSpec((1, 1, 1, HEAD_DIM), lambda i, h, pt, pk: (i, h, 0, 0)),
        ),
        out_shape=jax.ShapeDtypeStruct((n, N_HEADS_A, 1, HEAD_DIM), BF16),
        compiler_params=_params("arbitrary", "arbitrary"),
        name="moba_gather",
    )(page_table, picks,
      q.reshape(n, N_HEADS_A, 1, HEAD_DIM), k_new.reshape(n, N_HEADS_A, 1, HEAD_DIM),
      v_new.reshape(n, N_HEADS_A, 1, HEAD_DIM), scores4,
      *([cache_v5] * (2 * MOBA_TOPK)))


def _out_proj_kernel(oa_ref, ob_ref, w_ref, x_ref, g_ref, o_ref, wb_scr):
    @pl.when(pl.program_id(1) == 0)
    def _cast():
        wb_scr[...] = w_ref[...].astype(BF16)

    mix = jnp.dot(oa_ref[...], wb_scr[0:W_A, :], preferred_element_type=F32)
    mix = mix + jnp.dot(ob_ref[...], wb_scr[W_A:W_A + W_VB, :], preferred_element_type=F32)
    o_ref[...] = x_ref[...] + g_ref[...] * mix


def _out_proj(oa, ob, w, layer, x, gate, tm, tn, tiles_per_group):
    m, d = x.shape
    _, r, _ = gate.shape
    return pl.pallas_call(
        _out_proj_kernel,
        grid=(d // tn, m // tm),
        in_specs=[pl.BlockSpec((tm, W_A), lambda j, i: (i, 0)),
                  pl.BlockSpec((tm, W_VB), lambda j, i: (i, 0)),
                  pl.BlockSpec((None, W_A + W_VB, tn), lambda j, i: (layer, 0, j)),
                  pl.BlockSpec((tm, tn), lambda j, i: (i, j)),
                  pl.BlockSpec((None, r, tn), lambda j, i: (i // tiles_per_group, 0, j))],
        out_specs=pl.BlockSpec((tm, tn), lambda j, i: (i, j)),
        out_shape=jax.ShapeDtypeStruct((m, d), F32),
        scratch_shapes=[pltpu.VMEM((W_A + W_VB, tn), BF16)],
        compiler_params=_params("arbitrary", "arbitrary"),
        name="out_proj",
    )(oa, ob, w, x, gate)


def _mlp_scores_kernel(pt_ref, x_ref, sc_ref, sh_ref, g_ref, nw_ref, wu_ref, wd_ref, q_ref, *refs):
    k_refs = refs[:PAGES_PER_STEP]
    o_ref, s_ref, h_scr = refs[PAGES_PER_STEP:]
    _mlp_kernel(x_ref, sc_ref, sh_ref, g_ref, nw_ref, wu_ref, wd_ref, o_ref, h_scr,
                beside_matmuls=functools.partial(_page_scores, q_ref, k_refs, s_ref))


def _mlp_kernel(x_ref, sc_ref, sh_ref, g_ref, nw_ref, wu_ref, wd_ref, o_ref, h_scr, beside_matmuls=None):
    f = pl.program_id(1)
    d = o_ref.shape[1]
    tn = min(d, TN_DENSE)

    @pl.when(f == 0)
    def _first():
        y = _rms(x_ref[...], nw_ref[...])
        h_scr[...] = (y * (1.0 + sc_ref[...]) + sh_ref[...]).astype(BF16)
        o_ref[...] = jnp.zeros(o_ref.shape, F32)

    if beside_matmuls is not None:
        beside_matmuls()
    u = jnp.dot(h_scr[...], wu_ref[...], preferred_element_type=F32)
    r = jnp.maximum(u, 0.0)
    r2 = (r * r).astype(BF16)
    for c0 in range(0, d, tn):
        o_ref[:, c0:c0 + tn] += jnp.dot(r2, wd_ref[:, c0:c0 + tn], preferred_element_type=F32)

    @pl.when(f == pl.num_programs(1) - 1)
    def _last():
        o_ref[...] = x_ref[...] + g_ref[...] * o_ref[...]


def _mlp(x, sc, sh, gate, nw, wu, wd, tm, tf, tiles_per_group):
    m, d = x.shape
    ff = wu.shape[1]
    return pl.pallas_call(
        _mlp_kernel,
        grid=(m // tm, ff // tf),
        in_specs=[pl.BlockSpec((tm, d), lambda i, f: (i, 0)),
                  _mod_spec(sc, tiles_per_group), _mod_spec(sh, tiles_per_group), _mod_spec(gate, tiles_per_group),
                  pl.BlockSpec((1, d), lambda i, f: (0, 0)),
                  pl.BlockSpec((d, tf), lambda i, f: (0, f)),
                  pl.BlockSpec((tf, d), lambda i, f: (f, 0))],
        out_specs=pl.BlockSpec((tm, d), lambda i, f: (i, 0)),
        out_shape=jax.ShapeDtypeStruct((m, d), F32),
        scratch_shapes=[pltpu.VMEM((tm, d), BF16)],
        compiler_params=_params("arbitrary", "arbitrary"),
        name="mlp",
    )(x, sc, sh, gate, nw.reshape(1, d), wu, wd)


def _mlp_with_scores(x, sc, sh, gate, nw, wu, wd, tm, tf, tiles_per_group, cache_k, page_table, q):
    m, d = x.shape
    ff = wu.shape[1]
    n, n_pages = page_table.shape
    nf = ff // tf
    groups = n_pages // PAGES_PER_STEP
    assert (m // tm) * nf == n * groups and n_pages % PAGES_PER_STEP == 0

    def sample(i, f):
        return (i * nf + f) // groups

    def group(i, f):
        return (i * nf + f) % groups

    def page_spec(p):
        return pl.BlockSpec((1, PAGE_SIZE, N_HEADS_A, HEAD_DIM),
                            lambda i, f, pt: (pt[sample(i, f), group(i, f) * PAGES_PER_STEP + p], 0, 0, 0))

    def mod_spec(mod):
        _, r, w = mod.shape
        return pl.BlockSpec((None, r, w), lambda i, f, pt: (i // tiles_per_group, 0, 0))

    y, scores = pl.pallas_call(
        _mlp_scores_kernel,
        grid_spec=pltpu.PrefetchScalarGridSpec(
            num_scalar_prefetch=1,
            grid=(m // tm, nf),
            in_specs=[pl.BlockSpec((tm, d), lambda i, f, pt: (i, 0), pipeline_mode=pl.Buffered(1)),
                      mod_spec(sc), mod_spec(sh), mod_spec(gate),
                      pl.BlockSpec((1, d), lambda i, f, pt: (0, 0)),
                      pl.BlockSpec((d, tf), lambda i, f, pt: (0, f)),
                      pl.BlockSpec((tf, d), lambda i, f, pt: (f, 0)),
                      pl.BlockSpec((1, N_HEADS_A, HEAD_DIM), lambda i, f, pt: (sample(i, f), 0, 0))]
                     + [page_spec(p) for p in range(PAGES_PER_STEP)],
            out_specs=[pl.BlockSpec((tm, d), lambda i, f, pt: (i, 0), pipeline_mode=pl.Buffered(1)),
                       pl.BlockSpec((1, N_HEADS_A, PAGES_PER_STEP * PAGE_SIZE),
                                    lambda i, f, pt: (sample(i, f), 0, group(i, f)))],
            scratch_shapes=[pltpu.VMEM((tm, d), BF16)],
        ),
        out_shape=[jax.ShapeDtypeStruct((m, d), F32),
                   jax.ShapeDtypeStruct((n, N_HEADS_A, n_pages * PAGE_SIZE), F32)],
        compiler_params=_params("arbitrary", "arbitrary"),
        name="mlp_scores",
    )(page_table, x, sc, sh, gate, nw.reshape(1, d), wu, wd, q.reshape(n, N_HEADS_A, HEAD_DIM),
      *([cache_k] * PAGES_PER_STEP))
    return y, scores


def _dense_in(x2, mods, norm_w, w_in_t, layer, w_gate, rope, qk_norm_w, q_scale, tm_row, tm, tiles_per_group):
    h = _norm_mod(x2, mods["sc1"], mods["sh1"], norm_w, tm_row, tiles_per_group)
    tn = TN_DENSE
    cos, sin = rope
    pos_tiles = cos.shape[0] // tm
    q = _proj_rope(h, w_in_t, layer, 0, cos, sin, qk_norm_w[0], tm, pos_tiles, q_scale, True, BF16)
    k = _proj_rope(h, w_in_t, layer, 1, cos, sin, qk_norm_w[1], tm, pos_tiles, 1.0, True, F32)
    v = _proj(h, w_in_t, layer, 2 * W_A // tn, W_A, tm, tn, transposed=True)
    g = _proj(h, w_in_t, layer, 3 * W_A // tn, W_GLA, tm, tn, transposed=True)
    pre = _proj(h, w_gate, 0, 0, W_QB, tm, W_QB)
    return q, k, v, g, pre


def kernel(x_prompt, x_sample, c_prompt, c_sample, cache_k, cache_v, state_gla, page_table, w_ada, b_ada, norm_mix_w, w_in, q_norm_w, k_norm_w, w_gate_up, b_gate, gla_norm_w, w_out, norm_ffn_w, w_up, w_down):
    n_p, s_p, d = x_prompt.shape
    n_s, s_s, _ = x_sample.shape
    depth = w_ada.shape[0]
    assert depth == 1 and s_s == 1 and d == D_MODEL
    n_pages = page_table.shape[1]
    past_len = n_pages * PAGE_SIZE
    nb_past = past_len // MOBA_BLOCK
    assert past_len % MOBA_BLOCK == 0 and nb_past >= MOBA_TOPK
    l = 0
    tm_row = min(TM_ROWWISE, s_p)
    tm = min(TM_DENSE, s_p)

    rows = n_p + n_s
    rows_pad = -(-rows // 8) * 8
    c_all = jnp.concatenate([c_prompt, c_sample, jnp.zeros((rows_pad - rows, d), F32)], axis=0)
    ada = _ada(c_all, w_ada[l], b_ada[l])
    names = ("sh1", "sc1", "g1", "sh2", "sc2", "g2")
    mods_p = {nm: ada[0:n_p, i * d:(i + 1) * d].reshape(n_p, 1, d) for i, nm in enumerate(names)}
    mods_s = {nm: ada[n_p:rows, i * d:(i + 1) * d].reshape(1, n_s, d) for i, nm in enumerate(names)}

    w_in_t = jnp.swapaxes(w_in, 1, 2)
    w_gate = _gate_weight(w_in[l, :, 3 * W_A + W_GLA:], w_gate_up[l])[None]
    w_up_b = w_up[l].astype(BF16)
    w_down_b = w_down[l].astype(BF16)
    scale = HEAD_DIM ** -0.5
    qk_norm_w = (q_norm_w[l], k_norm_w[l])

    xp = x_prompt.reshape(n_p * s_p, d)
    rope_p = _rope_tables(jnp.arange(s_p, dtype=jnp.int32))
    q_p, k_p, v_p, g, pre = _dense_in(xp, mods_p, norm_mix_w[l], w_in_t, l, w_gate, rope_p, qk_norm_w,
                                      scale * LOG2E, tm_row, tm, s_p // tm_row)
    oa_p = _moba_prompt(q_p, k_p, v_p, n_p, s_p)
    ob_p, gla_p = _gla_prompt(g, pre, b_gate[l], gla_norm_w[l], n_p, s_p)
    x1_p = _out_proj(oa_p, ob_p, w_out, l, xp, mods_p["g1"], tm, TN_DENSE, s_p // tm)

    xs = x_sample.reshape(n_s, d)
    rope_s = _rope_tables(jnp.full((n_s,), past_len, jnp.int32))
    q_s, k_s, v_s, g, pre = _dense_in(xs, mods_s, norm_mix_w[l], w_in_t, l, w_gate, rope_s, qk_norm_w,
                                      scale, n_s, n_s, 1)
    tm_mlp = min(TM_MLP, s_p)
    y_p, scores = _mlp_with_scores(x1_p, mods_p["sc2"], mods_p["sh2"], mods_p["g2"], norm_ffn_w[l],
                                   w_up_b, w_down_b, tm_mlp, TF_MLP_PROMPT, s_p // tm_mlp, cache_k[l], page_table, q_s)
    picks = _moba_pick(scores.reshape(n_s * N_HEADS_A, past_len), nb_past)
    oa_s = _moba_gather(cache_v[l], page_table, picks, scores, q_s, k_s, v_s, nb_past).reshape(n_s, W_A)
    ob_s, gla_s = _gla_step(g, pre, b_gate[l], gla_norm_w[l], state_gla[l])
    x1_s = _out_proj(oa_s, ob_s, w_out, l, xs, mods_s["g1"], n_s, TN_DENSE, 1)
    y_s = _mlp(x1_s, mods_s["sc2"], mods_s["sh2"], mods_s["g2"], norm_ffn_w[l], w_up_b, w_down_b,
               n_s, TF_MLP, 1)

    return (y_p.reshape(n_p, s_p, d), y_s.reshape(n_s, s_s, d),
            k_p.reshape(1, n_p, s_p, N_HEADS_A, HEAD_DIM), v_p.reshape(1, n_p, s_p, N_HEADS_A, HEAD_DIM),
            gla_p[None],
            k_s.reshape(1, n_s, s_s, N_HEADS_A, HEAD_DIM), v_s.reshape(1, n_s, s_s, N_HEADS_A, HEAD_DIM),
            gla_s[None])
```

```python
import functools

import numpy as np
import jax
import jax.numpy as jnp
from jax import lax
from jax.experimental import pallas as pl
from jax.experimental.pallas import tpu as pltpu

D_MODEL = 2048
PAGE_SIZE = 128
HEAD_DIM = 128
N_HEADS_A = 8
N_HEADS_B = 4
DK_B = 128
DV_B = 256
GATE_RANK = 16
GATE_TEMP = 16.0
MOBA_BLOCK = 256
MOBA_TOPK = 3
GLA_CHUNK = 128
D_FF = 4 * D_MODEL
ROPE_THETA = 10000.0
EPS = 1e-6

W_A = N_HEADS_A * HEAD_DIM
W_QB = N_HEADS_B * DK_B
W_VB = N_HEADS_B * DV_B
W_GLA = 2 * W_QB + 2 * W_VB
LANES = 128

F32 = jnp.float32
BF16 = jnp.bfloat16
NEG_INF = float("-inf")
VMEM_LIMIT = 56 * 1024 * 1024

TM_DENSE = 1024
TN_DENSE = 1024
TM_MLP = 1024
TF_MLP_PROMPT = 512
TF_MLP = 512
GLA_STEP_SAMPLES = 4
ROPE_SUB_ROWS = 256
MOBA_HEADS_PER_STEP = 2
MOBA_QBLOCK_GROUP = 2
ONES_ROWS = 16
LOG2E = 1.4426950408889634

_NT = (((1,), (1,)), ((), ()))


def _params(*sem):
    return pltpu.CompilerParams(dimension_semantics=sem, vmem_limit_bytes=VMEM_LIMIT)


def _rms(x, w):
    return x * lax.rsqrt(jnp.mean(x * x, axis=-1, keepdims=True) + EPS) * w


def _log_sigmoid(x):
    return jnp.minimum(x, 0.0) - jnp.log(1.0 + jnp.exp(-jnp.abs(x)))


def _silu(x):
    return x / (1.0 + jnp.exp(-x))


def _ada_kernel(c_ref, w_ref, b_ref, o_ref):
    a = _silu(c_ref[...]).astype(BF16)
    o_ref[...] = jnp.dot(a, w_ref[...].astype(BF16), preferred_element_type=F32) + b_ref[...]


def _ada(c, w, b, tn=TN_DENSE):
    m, d = c.shape
    n = w.shape[1]
    return pl.pallas_call(
        _ada_kernel,
        grid=(n // tn,),
        in_specs=[pl.BlockSpec((m, d), lambda j: (0, 0)),
                  pl.BlockSpec((d, tn), lambda j: (0, j)),
                  pl.BlockSpec((1, tn), lambda j: (0, j))],
        out_specs=pl.BlockSpec((m, tn), lambda j: (0, j)),
        out_shape=jax.ShapeDtypeStruct((m, n), F32),
        compiler_params=_params("arbitrary"),
        name="ada_proj",
    )(c, w, b.reshape(1, n))


def _mod_spec(mod, tiles_per_group):
    _, r, w = mod.shape
    return pl.BlockSpec((None, r, w), lambda i, *_: (i // tiles_per_group, 0, 0))


def _mod_rows(ref, rs):
    return ref[...] if ref.shape[0] == 1 else ref[rs, :]


def _weight_spec(k, tn, layer, col_block0, transposed):
    if transposed:
        return pl.BlockSpec((None, tn, k), lambda j, *_: (layer, j + col_block0, 0))
    return pl.BlockSpec((None, k, tn), lambda j, *_: (layer, 0, j + col_block0))


def _weight_dot(a, wb, transposed):
    if transposed:
        return lax.dot_general(a, wb, _NT, preferred_element_type=F32)
    return jnp.dot(a, wb, preferred_element_type=F32)


def _proj_kernel(a_ref, w_ref, o_ref, wb_scr, *, transposed):
    @pl.when(pl.program_id(1) == 0)
    def _cast():
        wb_scr[...] = w_ref[...].astype(BF16)

    o_ref[...] = _weight_dot(a_ref[...], wb_scr[...], transposed).astype(o_ref.dtype)


def _proj(a, w, layer, col_block0, n_out, tm, tn, transposed=False, out_dtype=F32):
    m, k = a.shape
    w_spec = _weight_spec(k, tn, layer, col_block0, transposed)
    return pl.pallas_call(
        functools.partial(_proj_kernel, transposed=transposed),
        grid=(n_out // tn, m // tm),
        in_specs=[pl.BlockSpec((tm, k), lambda j, i: (i, 0)), w_spec],
        out_specs=pl.BlockSpec((tm, tn), lambda j, i: (i, j)),
        out_shape=jax.ShapeDtypeStruct((m, n_out), out_dtype),
        scratch_shapes=[pltpu.VMEM(w_spec.block_shape[1:], BF16)],
        compiler_params=_params("arbitrary", "arbitrary"),
        name="proj",
    )(a, w)


def _rope_heads(y, cos, sin, w, scale):
    out = []
    for h in range(y.shape[1] // HEAD_DIM):
        x = _rms(y[:, h * HEAD_DIM:(h + 1) * HEAD_DIM], w)
        r = x * cos + pltpu.roll(x, HEAD_DIM // 2, 1) * sin
        out.append(r * scale if scale != 1.0 else r)
    return out


def _proj_rope_kernel(a_ref, w_ref, cos_ref, sin_ref, nw_ref, o_ref, wb_scr, *, scale, rows, transposed):
    @pl.when(pl.program_id(0) == 0)
    def _cast():
        wb_scr[...] = w_ref[...].astype(BF16)

    for r0 in range(0, a_ref.shape[0], rows):
        rs = slice(r0, r0 + rows)
        y = _weight_dot(a_ref[rs, :], wb_scr[...], transposed)
        for h, r in enumerate(_rope_heads(y, cos_ref[rs, :], sin_ref[rs, :], nw_ref[...], scale)):
            o_ref[rs, h * HEAD_DIM:(h + 1) * HEAD_DIM] = r.astype(o_ref.dtype)


def _proj_rope(a, w, layer, col_block, cos, sin, norm_w, tm, pos_tiles, scale, transposed, out_dtype):
    m, k = a.shape
    if transposed:
        w_spec = pl.BlockSpec((None, W_A, k), lambda i: (layer, col_block, 0))
    else:
        w_spec = pl.BlockSpec((None, k, W_A), lambda i: (layer, 0, col_block))
    return pl.pallas_call(
        functools.partial(_proj_rope_kernel, scale=scale, rows=min(tm, ROPE_SUB_ROWS), transposed=transposed),
        grid=(m // tm,),
        in_specs=[pl.BlockSpec((tm, k), lambda i: (i, 0)),
                  w_spec,
                  pl.BlockSpec((tm, HEAD_DIM), lambda i: (i % pos_tiles, 0)),
                  pl.BlockSpec((tm, HEAD_DIM), lambda i: (i % pos_tiles, 0)),
                  pl.BlockSpec((1, HEAD_DIM), lambda i: (0, 0))],
        out_specs=pl.BlockSpec((tm, W_A), lambda i: (i, 0)),
        out_shape=jax.ShapeDtypeStruct((m, W_A), out_dtype),
        scratch_shapes=[pltpu.VMEM(w_spec.block_shape[1:], BF16)],
        compiler_params=_params("arbitrary"),
        name="proj_rope",
    )(a, w, cos, sin, norm_w.reshape(1, HEAD_DIM))


def _norm_proj_rope_kernel(x_ref, sc_ref, sh_ref, nmw_ref, w_ref, cos_ref, sin_ref, nw_ref, o_ref, h_ref, wb_scr,
                           *, scale, rows, transposed):
    @pl.when(pl.program_id(0) == 0)
    def _cast():
        wb_scr[...] = w_ref[...].astype(BF16)

    for r0 in range(0, x_ref.shape[0], rows):
        rs = slice(r0, r0 + rows)
        y = _rms(x_ref[rs, :], nmw_ref[...])
        hb = (y * (1.0 + _mod_rows(sc_ref, rs)) + _mod_rows(sh_ref, rs)).astype(BF16)
        h_ref[rs, :] = hb
        y = _weight_dot(hb, wb_scr[...], transposed)
        for h, r in enumerate(_rope_heads(y, cos_ref[rs, :], sin_ref[rs, :], nw_ref[...], scale)):
            o_ref[rs, h * HEAD_DIM:(h + 1) * HEAD_DIM] = r.astype(o_ref.dtype)


def _norm_proj_rope(x, sc, sh, norm_w, tiles_per_group, w, layer, col_block, cos, sin, qk_w, tm, pos_tiles,
                    scale, out_dtype):
    m, k = x.shape
    w_spec = pl.BlockSpec((None, W_A, k), lambda i: (layer, col_block, 0))
    return pl.pallas_call(
        functools.partial(_norm_proj_rope_kernel, scale=scale, rows=min(tm, ROPE_SUB_ROWS), transposed=True),
        grid=(m // tm,),
        in_specs=[pl.BlockSpec((tm, k), lambda i: (i, 0)),
                  _mod_spec(sc, tiles_per_group), _mod_spec(sh, tiles_per_group),
                  pl.BlockSpec((1, k), lambda i: (0, 0)),
                  w_spec,
                  pl.BlockSpec((tm, HEAD_DIM), lambda i: (i % pos_tiles, 0)),
                  pl.BlockSpec((tm, HEAD_DIM), lambda i: (i % pos_tiles, 0)),
                  pl.BlockSpec((1, HEAD_DIM), lambda i: (0, 0))],
        out_specs=[pl.BlockSpec((tm, W_A), lambda i: (i, 0)), pl.BlockSpec((tm, k), lambda i: (i, 0))],
        out_shape=[jax.ShapeDtypeStruct((m, W_A), out_dtype), jax.ShapeDtypeStruct((m, k), BF16)],
        scratch_shapes=[pltpu.VMEM(w_spec.block_shape[1:], BF16)],
        compiler_params=_params("arbitrary"),
        name="norm_proj_rope",
    )(x, sc, sh, norm_w.reshape(1, k), w, cos, sin, qk_w.reshape(1, HEAD_DIM))


def _gate_weight_kernel(a_ref, b_ref, o_ref):
    o_ref[...] = jnp.dot(a_ref[...], b_ref[...], preferred_element_type=F32, precision=lax.Precision.HIGHEST)


def _gate_weight(w_ab, w_gate_up):
    d, r = w_ab.shape
    n = w_gate_up.shape[1]
    return pl.pallas_call(
        _gate_weight_kernel,
        grid=(1,),
        in_specs=[pl.BlockSpec((d, r), lambda i: (0, 0)), pl.BlockSpec((r, n), lambda i: (0, 0))],
        out_specs=pl.BlockSpec((d, n), lambda i: (0, 0)),
        out_shape=jax.ShapeDtypeStruct((d, n), F32),
        compiler_params=_params("arbitrary"),
        name="gate_weight",
    )(w_ab, w_gate_up)


def _rope_tables(pos):
    half = HEAD_DIM // 2
    inv_freq = ROPE_THETA ** (-jnp.arange(half, dtype=F32) / half)
    ang = pos.astype(F32)[:, None] * inv_freq[None, :]
    cos, sin = jnp.cos(ang), jnp.sin(ang)
    return jnp.concatenate([cos, cos], axis=-1), jnp.concatenate([-sin, sin], axis=-1)


def _topk_select(g, valid, axis):
    n = g.shape[axis]
    idx = lax.broadcasted_iota(jnp.int32, g.shape, axis)
    gm = jnp.where(valid, g, NEG_INF)
    rank = jnp.zeros(g.shape, F32)
    for c in range(n):
        gc = lax.slice_in_dim(gm, c, c + 1, axis=axis)
        beats = (gc > gm) | ((gc == gm) & (idx > c))
        rank = rank + beats.astype(F32)
    return valid & (rank < MOBA_TOPK)


def _moba_prompt_kernel(q_ref, k_ref, v_ref, o_ref, kb_scr, vt_scr, km_scr, sel_scr, s_scr, *, nb, hp, group):
    qb = pl.program_id(2)
    blk = MOBA_BLOCK
    heads = [slice(hh * HEAD_DIM, (hh + 1) * HEAD_DIM) for hh in range(hp)]

    @pl.when(qb == 0)
    def _prep():
        def per_block(j, carry):
            rows = pl.ds(pl.multiple_of(j * blk, blk), blk)
            for hh, cols in enumerate(heads):
                kj = k_ref[rows, cols]
                kb_scr[hh, j] = kj.astype(BF16)
                km_scr[hh, pl.ds(j, 1), :] = jnp.sum(kj, axis=0, keepdims=True) * (1.0 / blk)
                vt_scr[hh, j, 0:HEAD_DIM, :] = v_ref[rows, cols].T.astype(BF16)
                vt_scr[hh, j, HEAD_DIM:HEAD_DIM + ONES_ROWS, :] = jnp.ones((ONES_ROWS, blk), BF16)
            return carry

        lax.fori_loop(0, nb, per_block, 0)

    causal = (lax.broadcasted_iota(jnp.int32, (blk, blk), 0) <= lax.broadcasted_iota(jnp.int32, (blk, blk), 1))
    for hh, cols in enumerate(heads):
        gate = lax.dot_general(km_scr[hh].astype(BF16), q_ref[:, cols], _NT, preferred_element_type=F32)
        cand = lax.broadcasted_iota(jnp.int32, gate.shape, 0)
        sel_scr[hh] = (_topk_select(gate, cand < qb, 0) | (cand == qb)).astype(F32)

    def attend(n_keys):
        first_own = n_keys - group
        for hh, cols in enumerate(heads):
            q = q_ref[:, cols]
            m = jnp.full((1, blk), NEG_INF, F32)
            for j in range(n_keys):
                s = lax.dot_general(kb_scr[hh, j], q, _NT, preferred_element_type=F32)
                s = jnp.where(sel_scr[hh, j:j + 1, :] > 0.0, s, NEG_INF)
                s_scr[hh, j] = s
                bmax = jnp.max(s, axis=0, keepdims=True)
                if j >= first_own:
                    bmax = jnp.where(qb == j, NEG_INF, bmax)
                m = jnp.maximum(m, bmax)
            s_own = jnp.where(causal, s_scr[hh, qb], NEG_INF)
            s_scr[hh, qb] = s_own
            m = jnp.maximum(m, jnp.max(s_own, axis=0, keepdims=True))
            acc = jnp.zeros((HEAD_DIM + ONES_ROWS, blk), F32)
            for j in range(n_keys):
                p = jnp.exp2(s_scr[hh, j] - m)
                acc = acc + jnp.dot(vt_scr[hh, j], p.astype(BF16), preferred_element_type=F32)
            l = acc[HEAD_DIM:HEAD_DIM + 1, :]
            o_ref[:, cols] = (acc[0:HEAD_DIM, :] * (1.0 / l)).T.astype(o_ref.dtype)

    for g in range(nb // group):
        pl.when(qb // group == g)(functools.partial(attend, (g + 1) * group))


def _moba_prompt(q, k, v, n_seq, seq):
    nb = seq // MOBA_BLOCK
    hp = MOBA_HEADS_PER_STEP
    group = min(MOBA_QBLOCK_GROUP, nb)
    assert nb % group == 0
    w = hp * HEAD_DIM
    return pl.pallas_call(
        functools.partial(_moba_prompt_kernel, nb=nb, hp=hp, group=group),
        grid=(n_seq, N_HEADS_A // hp, nb),
        in_specs=[pl.BlockSpec((MOBA_BLOCK, w), lambda n, h, b: (n * nb + b, h)),
                  pl.BlockSpec((seq, w), lambda n, h, b: (n, h)),
                  pl.BlockSpec((seq, w), lambda n, h, b: (n, h))],
        out_specs=pl.BlockSpec((MOBA_BLOCK, w), lambda n, h, b: (n * nb + b, h)),
        out_shape=jax.ShapeDtypeStruct((n_seq * seq, W_A), BF16),
        scratch_shapes=[pltpu.VMEM((hp, nb, MOBA_BLOCK, HEAD_DIM), BF16),
                        pltpu.VMEM((hp, nb, HEAD_DIM + ONES_ROWS, MOBA_BLOCK), BF16),
                        pltpu.VMEM((hp, nb, HEAD_DIM), F32),
                        pltpu.VMEM((hp, nb, MOBA_BLOCK), F32),
                        pltpu.VMEM((hp, nb, MOBA_BLOCK, MOBA_BLOCK), F32)],
        compiler_params=_params("arbitrary", "arbitrary", "arbitrary"),
        name="moba_prompt",
    )(q, k, v)


def _gla_levels():
    m, out = 1, []
    while m < GLA_CHUNK:
        out.append(m)
        m *= 2
    return out


def _gla_cum_matrices():
    c = GLA_CHUNK
    t = np.arange(c)
    tri = (t[None, :] <= t[:, None]).astype(np.float32)
    mats = [tri]
    for m in _gla_levels():
        mats.append(tri[(t // (2 * m)) * (2 * m) + m - 1])
    return np.concatenate(mats, axis=0)


def _split_bf16(x):
    hi = x.astype(BF16)
    return hi, (x - hi.astype(F32)).astype(BF16)


def _gla_prompt_kernel(g_ref, pre_ref, cm_ref, bg_ref, nw_ref, o_ref, s_ref):
    c = GLA_CHUNK

    @pl.when(pl.program_id(1) == 0)
    def _init():
        s_ref[...] = jnp.zeros(s_ref.shape, F32)

    tok = lax.broadcasted_iota(jnp.int32, (c, c), 0)
    col = lax.broadcasted_iota(jnp.int32, (c, c), 1)
    tok_d = lax.broadcasted_iota(jnp.int32, (c, DK_B), 0)
    cm = cm_ref[...]
    for h in range(N_HEADS_B):
        dk = slice(h * DK_B, (h + 1) * DK_B)
        la = _log_sigmoid(pre_ref[:, dk] + bg_ref[:, dk]) * (LOG2E / GATE_TEMP)
        cums = sum(jnp.dot(cm, piece, preferred_element_type=F32) for piece in _split_bf16(la))
        b = cums[0:c]
        q = g_ref[:, dk] * (DK_B ** -0.5)
        k = g_ref[:, W_QB + h * DK_B:W_QB + (h + 1) * DK_B]
        v = g_ref[:, 2 * W_QB + h * DV_B:2 * W_QB + (h + 1) * DV_B]
        rb = g_ref[:, 2 * W_QB + W_VB + h * DV_B:2 * W_QB + W_VB + (h + 1) * DV_B]
        vb = v.astype(BF16)

        attn = lax.dot_general(q.astype(BF16), k.astype(BF16), _NT, preferred_element_type=F32)
        attn = jnp.where(col == tok, attn, 0.0)
        for lvl, m in enumerate(_gla_levels()):
            r = cums[(lvl + 1) * c:(lvl + 2) * c]
            upper = ((tok_d // m) % 2) == 1
            x = b - r
            e = jnp.exp2(jnp.where(upper, x, -x))
            e_up = jnp.where(upper, e, 0.0)
            a_m = lax.dot_general((q * e_up).astype(BF16), (k * (e - e_up)).astype(BF16), _NT,
                                  preferred_element_type=F32)
            attn = attn + jnp.where((tok // (2 * m)) == (col // (2 * m)), a_m, 0.0)

        state = s_ref[0, h]
        o = jnp.dot(attn.astype(BF16), vb, preferred_element_type=F32)
        o = o + jnp.dot((q * jnp.exp2(b)).astype(BF16), state.astype(BF16), preferred_element_type=F32)
        b_last = b[c - 1:c, :]
        kd = k * jnp.exp2(b_last - b)
        kv = jnp.dot(kd.T.astype(BF16), vb, preferred_element_type=F32)
        dcol = b.T[:, c - 1:c]
        s_ref[0, h] = jnp.exp2(dcol) * state + kv

        y = _rms(o, nw_ref[...]) * _silu(rb)
        o_ref[:, h * DV_B:(h + 1) * DV_B] = y.astype(o_ref.dtype)


def _gla_prompt(g, pre, bg, nw, n_seq, seq):
    nc = seq // GLA_CHUNK
    cm = jnp.asarray(_gla_cum_matrices(), BF16)
    return pl.pallas_call(
        _gla_prompt_kernel,
        grid=(n_seq, nc),
        in_specs=[pl.BlockSpec((GLA_CHUNK, W_GLA), lambda n, t: (n * nc + t, 0)),
                  pl.BlockSpec((GLA_CHUNK, W_QB), lambda n, t: (n * nc + t, 0)),
                  pl.BlockSpec(cm.shape, lambda n, t: (0, 0)),
                  pl.BlockSpec((1, W_QB), lambda n, t: (0, 0)),
                  pl.BlockSpec((1, DV_B), lambda n, t: (0, 0))],
        out_specs=[pl.BlockSpec((GLA_CHUNK, W_VB), lambda n, t: (n * nc + t, 0)),
                   pl.BlockSpec((1, N_HEADS_B, DK_B, DV_B), lambda n, t: (n, 0, 0, 0))],
        out_shape=[jax.ShapeDtypeStruct((n_seq * seq, W_VB), BF16),
                   jax.ShapeDtypeStruct((n_seq, N_HEADS_B, DK_B, DV_B), F32)],
        compiler_params=_params("arbitrary", "arbitrary"),
        name="gla_prompt",
    )(g, pre, cm, bg.reshape(1, W_QB), nw.reshape(1, DV_B))


def _gla_step_kernel(qc_ref, kc_ref, prec_ref, v_ref, rb_ref, bgc_ref, nw_ref, s0_ref, o_ref, s_ref):
    for i in range(s0_ref.shape[0]):
        for h in range(N_HEADS_B):
            rows = slice(h * DK_B, (h + 1) * DK_B)
            a = jnp.exp(_log_sigmoid(prec_ref[i, rows, :] + bgc_ref[rows, :]) * (1.0 / GATE_TEMP))
            v = v_ref[i, :, h * DV_B:(h + 1) * DV_B]
            s_new = a * s0_ref[i, h] + kc_ref[i, rows, :] * v
            s_ref[i, h] = s_new
            o = jnp.sum((qc_ref[i, rows, :] * (DK_B ** -0.5)) * s_new, axis=0, keepdims=True)
            y = _rms(o, nw_ref[...]) * _silu(rb_ref[i, :, h * DV_B:(h + 1) * DV_B])
            o_ref[i, :, h * DV_B:(h + 1) * DV_B] = y.astype(o_ref.dtype)


def _gla_step(g, pre, bg, nw, s0):
    n = g.shape[0]
    sb = GLA_STEP_SAMPLES if n % GLA_STEP_SAMPLES == 0 else 1
    qc = g[:, 0:W_QB].reshape(n, W_QB, 1)
    kc = g[:, W_QB:2 * W_QB].reshape(n, W_QB, 1)
    v = g[:, 2 * W_QB:2 * W_QB + W_VB].reshape(n, 1, W_VB)
    rb = g[:, 2 * W_QB + W_VB:W_GLA].reshape(n, 1, W_VB)
    per_n3 = lambda i: (i, 0, 0)
    fixed = lambda i: (0, 0)
    o, s = pl.pallas_call(
        _gla_step_kernel,
        grid=(n // sb,),
        in_specs=[pl.BlockSpec((sb, W_QB, 1), per_n3), pl.BlockSpec((sb, W_QB, 1), per_n3),
                  pl.BlockSpec((sb, W_QB, 1), per_n3),
                  pl.BlockSpec((sb, 1, W_VB), per_n3), pl.BlockSpec((sb, 1, W_VB), per_n3),
                  pl.BlockSpec((W_QB, 1), fixed), pl.BlockSpec((1, DV_B), fixed),
                  pl.BlockSpec((sb, N_HEADS_B, DK_B, DV_B), lambda i: (i, 0, 0, 0))],
        out_specs=[pl.BlockSpec((sb, 1, W_VB), per_n3),
                   pl.BlockSpec((sb, N_HEADS_B, DK_B, DV_B), lambda i: (i, 0, 0, 0))],
        out_shape=[jax.ShapeDtypeStruct((n, 1, W_VB), BF16),
                   jax.ShapeDtypeStruct(s0.shape, F32)],
        compiler_params=_params("arbitrary"),
        name="gla_step",
    )(qc, kc, pre.reshape(n, W_QB, 1), v, rb, bg.reshape(W_QB, 1), nw.reshape(1, DV_B), s0)
    return o.reshape(n, W_VB), s


PAGES_PER_STEP = 16


def _page_scores(q_ref, k_refs, o_ref):
    q = q_ref[0].astype(F32)
    shape3 = (PAGE_SIZE, N_HEADS_A, LANES)
    diag = lax.broadcasted_iota(jnp.int32, shape3, 0) == lax.broadcasted_iota(jnp.int32, shape3, 2)
    for p, k_ref in enumerate(k_refs):
        r = jnp.sum(k_ref[0] * q[None], axis=-1, keepdims=True)
        o_ref[0, :, p * PAGE_SIZE:(p + 1) * PAGE_SIZE] = jnp.sum(jnp.where(diag, r, 0.0), axis=0)


def _moba_pick_kernel(s_ref, o_ref, *, nb):
    rows = s_ref.shape[0]
    lane = lax.broadcasted_iota(jnp.int32, (rows, LANES), 1)
    gate = jnp.zeros((rows, LANES), F32)
    for j in range(nb):
        gj = jnp.sum(s_ref[:, j * MOBA_BLOCK:(j + 1) * MOBA_BLOCK], axis=-1, keepdims=True) * (1.0 / MOBA_BLOCK)
        gate = jnp.where(lane == j, gj, gate)
    valid = lane < nb
    gm = jnp.where(valid, gate, NEG_INF)
    rank = jnp.zeros((rows, LANES), F32)
    for c in range(nb):
        gc = gm[:, c:c + 1]
        beats = (gc > gm) | ((gc == gm) & (lane > c))
        rank = rank + beats.astype(F32)
    out = jnp.zeros((rows, LANES), jnp.int32)
    for t in range(MOBA_TOPK):
        pick = jnp.sum(jnp.where(valid & (rank == float(t)), lane, 0), axis=-1, keepdims=True)
        out = jnp.where(lane == t, pick, out)
    o_ref[...] = out


def _moba_pick(scores, nb):
    rows = scores.shape[0]
    return pl.pallas_call(
        functools.partial(_moba_pick_kernel, nb=nb),
        grid=(1,),
        in_specs=[pl.BlockSpec(scores.shape, lambda i: (0, 0))],
        out_specs=pl.BlockSpec((rows, LANES), lambda i: (0, 0)),
        out_shape=jax.ShapeDtypeStruct((rows, LANES), jnp.int32),
        compiler_params=_params("arbitrary"),
        name="moba_pick",
    )(scores)


def _moba_gather_kernel(pt_ref, pick_ref, q_ref, kn_ref, vn_ref, s_ref, *refs):
    v_refs, o_ref = refs[:2 * MOBA_TOPK], refs[2 * MOBA_TOPK]
    row = pl.program_id(0) * N_HEADS_A + pl.program_id(1)
    s_self = jnp.sum(q_ref[0, 0].astype(F32) * kn_ref[0, 0].astype(BF16).astype(F32), axis=-1, keepdims=True)
    s = [s_ref[0, 0, pl.ds(pick_ref[row, t], 1), :] for t in range(MOBA_TOPK)]
    m = s_self
    for st in s:
        m = jnp.maximum(m, jnp.max(st, axis=-1, keepdims=True))
    p_self = jnp.exp(s_self - m)
    l = p_self
    acc = jnp.zeros((8, HEAD_DIM), F32)
    for t, st in enumerate(s):
        p = jnp.exp(st - m)
        l = l + jnp.sum(p, axis=-1, keepdims=True)
        pb = jnp.broadcast_to(p, (8, MOBA_BLOCK)).astype(BF16)
        for half in range(2):
            vpage = v_refs[2 * t + half].reshape(PAGE_SIZE, HEAD_DIM)[...].astype(BF16)
            acc = acc + jnp.dot(pb[:, half * PAGE_SIZE:(half + 1) * PAGE_SIZE], vpage, preferred_element_type=F32)
    out = acc[0:1] + p_self * vn_ref[0, 0].astype(BF16).astype(F32)
    o_ref[0, 0] = (out * (1.0 / l)).astype(o_ref.dtype)


def _moba_gather(cache_v, page_table, picks, scores, q, k_new, v_new, nb):
    n = page_table.shape[0]
    pages_per_block = MOBA_BLOCK // PAGE_SIZE
    assert pages_per_block == 2
    scores4 = scores.reshape(n, N_HEADS_A, nb, MOBA_BLOCK)
    cache_v5 = cache_v.reshape(cache_v.shape[0], PAGE_SIZE, N_HEADS_A, 1, HEAD_DIM)

    def v_spec(t, half):
        return pl.BlockSpec((1, PAGE_SIZE, 1, 1, HEAD_DIM),
                            lambda i, h, pt, pk: (pt[i, pk[i * N_HEADS_A + h, t] * pages_per_block + half], 0, h, 0, 0))

    return pl.pallas_call(
        _moba_gather_kernel,
        grid_spec=pltpu.PrefetchScalarGridSpec(
            num_scalar_prefetch=2,
            grid=(n, N_HEADS_A),
            in_specs=[pl.BlockSpec((1, 1, 1, HEAD_DIM), lambda i, h, pt, pk: (i, h, 0, 0))] * 3
                     + [pl.BlockSpec((1, 1, nb, MOBA_BLOCK), lambda i, h, pt, pk: (i, h, 0, 0))]
                     + [v_spec(t, half) for t in range(MOBA_TOPK) for half in range(2)],
            out_specs=pl.BlockSpec((1, 1, 1, HEAD_DIM), lambda i, h, pt, pk: (i, h, 0, 0)),
        ),
        out_shape=jax.ShapeDtypeStruct((n, N_HEADS_A, 1, HEAD_DIM), BF16),
        compiler_params=_params("arbitrary", "arbitrary"),
        name="moba_gather",
    )(page_table, picks,
      q.reshape(n, N_HEADS_A, 1, HEAD_DIM), k_new.reshape(n, N_HEADS_A, 1, HEAD_DIM),
      v_new.reshape(n, N_HEADS_A, 1, HEAD_DIM), scores4,
      *([cache_v5] * (2 * MOBA_TOPK)))


def _out_proj_kernel(oa_ref, ob_ref, w_ref, x_ref, g_ref, o_ref, wb_scr):
    @pl.when(pl.program_id(1) == 0)
    def _cast():
        wb_scr[...] = w_ref[...].astype(BF16)

    mix = jnp.dot(oa_ref[...], wb_scr[0:W_A, :], preferred_element_type=F32)
    mix = mix + jnp.dot(ob_ref[...], wb_scr[W_A:W_A + W_VB, :], preferred_element_type=F32)
    o_ref[...] = x_ref[...] + g_ref[...] * mix


def _out_proj(oa, ob, w, layer, x, gate, tm, tn, tiles_per_group):
    m, d = x.shape
    _, r, _ = gate.shape
    return pl.pallas_call(
        _out_proj_kernel,
        grid=(d // tn, m // tm),
        in_specs=[pl.BlockSpec((tm, W_A), lambda j, i: (i, 0)),
                  pl.BlockSpec((tm, W_VB), lambda j, i: (i, 0)),
                  pl.BlockSpec((None, W_A + W_VB, tn), lambda j, i: (layer, 0, j)),
                  pl.BlockSpec((tm, tn), lambda j, i: (i, j)),
                  pl.BlockSpec((None, r, tn), lambda j, i: (i // tiles_per_group, 0, j))],
        out_specs=pl.BlockSpec((tm, tn), lambda j, i: (i, j)),
        out_shape=jax.ShapeDtypeStruct((m, d), F32),
        scratch_shapes=[pltpu.VMEM((W_A + W_VB, tn), BF16)],
        compiler_params=_params("arbitrary", "arbitrary"),
        name="out_proj",
    )(oa, ob, w, x, gate)


def _mlp_scores_kernel(pt_ref, x_ref, sc_ref, sh_ref, g_ref, nw_ref, wu_ref, wd_ref, q_ref, *refs):
    k_refs = refs[:PAGES_PER_STEP]
    o_ref, s_ref, h_scr = refs[PAGES_PER_STEP:]
    _mlp_kernel(x_ref, sc_ref, sh_ref, g_ref, nw_ref, wu_ref, wd_ref, o_ref, h_scr,
                beside_matmuls=functools.partial(_page_scores, q_ref, k_refs, s_ref))


def _mlp_kernel(x_ref, sc_ref, sh_ref, g_ref, nw_ref, wu_ref, wd_ref, o_ref, h_scr, beside_matmuls=None):
    f = pl.program_id(1)
    d = o_ref.shape[1]
    tn = min(d, TN_DENSE)

    @pl.when(f == 0)
    def _first():
        y = _rms(x_ref[...], nw_ref[...])
        h_scr[...] = (y * (1.0 + sc_ref[...]) + sh_ref[...]).astype(BF16)
        o_ref[...] = jnp.zeros(o_ref.shape, F32)

    if beside_matmuls is not None:
        beside_matmuls()
    u = jnp.dot(h_scr[...], wu_ref[...], preferred_element_type=F32)
    r = jnp.maximum(u, 0.0)
    r2 = (r * r).astype(BF16)
    for c0 in range(0, d, tn):
        o_ref[:, c0:c0 + tn] += jnp.dot(r2, wd_ref[:, c0:c0 + tn], preferred_element_type=F32)

    @pl.when(f == pl.num_programs(1) - 1)
    def _last():
        o_ref[...] = x_ref[...] + g_ref[...] * o_ref[...]


def _mlp(x, sc, sh, gate, nw, wu, wd, tm, tf, tiles_per_group):
    m, d = x.shape
    ff = wu.shape[1]
    return pl.pallas_call(
        _mlp_kernel,
        grid=(m // tm, ff // tf),
        in_specs=[pl.BlockSpec((tm, d), lambda i, f: (i, 0)),
                  _mod_spec(sc, tiles_per_group), _mod_spec(sh, tiles_per_group), _mod_spec(gate, tiles_per_group),
                  pl.BlockSpec((1, d), lambda i, f: (0, 0)),
                  pl.BlockSpec((d, tf), lambda i, f: (0, f)),
                  pl.BlockSpec((tf, d), lambda i, f: (f, 0))],
        out_specs=pl.BlockSpec((tm, d), lambda i, f: (i, 0)),
        out_shape=jax.ShapeDtypeStruct((m, d), F32),
        scratch_shapes=[pltpu.VMEM((tm, d), BF16)],
        compiler_params=_params("arbitrary", "arbitrary"),
        name="mlp",
    )(x, sc, sh, gate, nw.reshape(1, d), wu, wd)


def _mlp_with_scores(x, sc, sh, gate, nw, wu, wd, tm, tf, tiles_per_group, cache_k, page_table, q):
    m, d = x.shape
    ff = wu.shape[1]
    n, n_pages = page_table.shape
    nf = ff // tf
    groups = n_pages // PAGES_PER_STEP
    assert (m // tm) * nf == n * groups and n_pages % PAGES_PER_STEP == 0

    def sample(i, f):
        return (i * nf + f) // groups

    def group(i, f):
        return (i * nf + f) % groups

    def page_spec(p):
        return pl.BlockSpec((1, PAGE_SIZE, N_HEADS_A, HEAD_DIM),
                            lambda i, f, pt: (pt[sample(i, f), group(i, f) * PAGES_PER_STEP + p], 0, 0, 0))

    def mod_spec(mod):
        _, r, w = mod.shape
        return pl.BlockSpec((None, r, w), lambda i, f, pt: (i // tiles_per_group, 0, 0))

    y, scores = pl.pallas_call(
        _mlp_scores_kernel,
        grid_spec=pltpu.PrefetchScalarGridSpec(
            num_scalar_prefetch=1,
            grid=(m // tm, nf),
            in_specs=[pl.BlockSpec((tm, d), lambda i, f, pt: (i, 0), pipeline_mode=pl.Buffered(1)),
                      mod_spec(sc), mod_spec(sh), mod_spec(gate),
                      pl.BlockSpec((1, d), lambda i, f, pt: (0, 0)),
                      pl.BlockSpec((d, tf), lambda i, f, pt: (0, f)),
                      pl.BlockSpec((tf, d), lambda i, f, pt: (f, 0)),
                      pl.BlockSpec((1, N_HEADS_A, HEAD_DIM), lambda i, f, pt: (sample(i, f), 0, 0))]
                     + [page_spec(p) for p in range(PAGES_PER_STEP)],
            out_specs=[pl.BlockSpec((tm, d), lambda i, f, pt: (i, 0), pipeline_mode=pl.Buffered(1)),
                       pl.BlockSpec((1, N_HEADS_A, PAGES_PER_STEP * PAGE_SIZE),
                                    lambda i, f, pt: (sample(i, f), 0, group(i, f)))],
            scratch_shapes=[pltpu.VMEM((tm, d), BF16)],
        ),
        out_shape=[jax.ShapeDtypeStruct((m, d), F32),
                   jax.ShapeDtypeStruct((n, N_HEADS_A, n_pages * PAGE_SIZE), F32)],
        compiler_params=_params("arbitrary", "arbitrary"),
        name="mlp_scores",
    )(page_table, x, sc, sh, gate, nw.reshape(1, d), wu, wd, q.reshape(n, N_HEADS_A, HEAD_DIM),
      *([cache_k] * PAGES_PER_STEP))
    return y, scores


def _dense_in(x2, mods, norm_w, w_in_t, layer, w_gate, rope, qk_norm_w, q_scale, tm, tiles_per_group):
    tn = TN_DENSE
    cos, sin = rope
    pos_tiles = cos.shape[0] // tm
    q, h = _norm_proj_rope(x2, mods["sc1"], mods["sh1"], norm_w, tiles_per_group, w_in_t, layer, 0, cos, sin,
                           qk_norm_w[0], tm, pos_tiles, q_scale, BF16)
    k = _proj_rope(h, w_in_t, layer, 1, cos, sin, qk_norm_w[1], tm, pos_tiles, 1.0, True, F32)
    v = _proj(h, w_in_t, layer, 2 * W_A // tn, W_A, tm, tn, transposed=True)
    g = _proj(h, w_in_t, layer, 3 * W_A // tn, W_GLA, tm, tn, transposed=True)
    pre = _proj(h, w_gate, 0, 0, W_QB, tm, W_QB)
    return q, k, v, g, pre


def kernel(x_prompt, x_sample, c_prompt, c_sample, cache_k, cache_v, state_gla, page_table, w_ada, b_ada, norm_mix_w, w_in, q_norm_w, k_norm_w, w_gate_up, b_gate, gla_norm_w, w_out, norm_ffn_w, w_up, w_down):
    n_p, s_p, d = x_prompt.shape
    n_s, s_s, _ = x_sample.shape
    depth = w_ada.shape[0]
    assert depth == 1 and s_s == 1 and d == D_MODEL
    n_pages = page_table.shape[1]
    past_len = n_pages * PAGE_SIZE
    nb_past = past_len // MOBA_BLOCK
    assert past_len % MOBA_BLOCK == 0 and nb_past >= MOBA_TOPK
    l = 0
    tm = min(TM_DENSE, s_p)

    rows = n_p + n_s
    rows_pad = -(-rows // 8) * 8
    c_all = jnp.concatenate([c_prompt, c_sample, jnp.zeros((rows_pad - rows, d), F32)], axis=0)
    ada = _ada(c_all, w_ada[l], b_ada[l])
    names = ("sh1", "sc1", "g1", "sh2", "sc2", "g2")
    mods_p = {nm: ada[0:n_p, i * d:(i + 1) * d].reshape(n_p, 1, d) for i, nm in enumerate(names)}
    mods_s = {nm: ada[n_p:rows, i * d:(i + 1) * d].reshape(1, n_s, d) for i, nm in enumerate(names)}

    w_in_t = jnp.swapaxes(w_in, 1, 2)
    w_gate = _gate_weight(w_in[l, :, 3 * W_A + W_GLA:], w_gate_up[l])[None]
    w_up_b = w_up[l].astype(BF16)
    w_down_b = w_down[l].astype(BF16)
    scale = HEAD_DIM ** -0.5
    qk_norm_w = (q_norm_w[l], k_norm_w[l])

    xp = x_prompt.reshape(n_p * s_p, d)
    rope_p = _rope_tables(jnp.arange(s_p, dtype=jnp.int32))
    q_p, k_p, v_p, g, pre = _dense_in(xp, mods_p, norm_mix_w[l], w_in_t, l, w_gate, rope_p, qk_norm_w,
                                      scale * LOG2E, tm, s_p // tm)
    oa_p = _moba_prompt(q_p, k_p, v_p, n_p, s_p)
    ob_p, gla_p = _gla_prompt(g, pre, b_gate[l], gla_norm_w[l], n_p, s_p)
    x1_p = _out_proj(oa_p, ob_p, w_out, l, xp, mods_p["g1"], tm, TN_DENSE, s_p // tm)

    xs = x_sample.reshape(n_s, d)
    rope_s = _rope_tables(jnp.full((n_s,), past_len, jnp.int32))
    q_s, k_s, v_s, g, pre = _dense_in(xs, mods_s, norm_mix_w[l], w_in_t, l, w_gate, rope_s, qk_norm_w,
                                      scale, n_s, 1)
    tm_mlp = min(TM_MLP, s_p)
    y_p, scores = _mlp_with_scores(x1_p, mods_p["sc2"], mods_p["sh2"], mods_p["g2"], norm_ffn_w[l],
                                   w_up_b, w_down_b, tm_mlp, TF_MLP_PROMPT, s_p // tm_mlp, cache_k[l], page_table, q_s)
    picks = _moba_pick(scores.reshape(n_s * N_HEADS_A, past_len), nb_past)
    oa_s = _moba_gather(cache_v[l], page_table, picks, scores, q_s, k_s, v_s, nb_past).reshape(n_s, W_A)
    ob_s, gla_s = _gla_step(g, pre, b_gate[l], gla_norm_w[l], state_gla[l])
    x1_s = _out_proj(oa_s, ob_s, w_out, l, xs, mods_s["g1"], n_s, TN_DENSE, 1)
    y_s = _mlp(x1_s, mods_s["sc2"], mods_s["sh2"], mods_s["g2"], norm_ffn_w[l], w_up_b, w_down_b,
               n_s, TF_MLP, 1)

    return (y_p.reshape(n_p, s_p, d), y_s.reshape(n_s, s_s, d),
            k_p.reshape(1, n_p, s_p, N_HEADS_A, HEAD_DIM), v_p.reshape(1, n_p, s_p, N_HEADS_A, HEAD_DIM),
            gla_p[None],
            k_s.reshape(1, n_s, s_s, N_HEADS_A, HEAD_DIM), v_s.reshape(1, n_s, s_s, N_HEADS_A, HEAD_DIM),
            gla_s[None])
```

```python
import functools

import numpy as np
import jax
import jax.numpy as jnp
from jax import lax
from jax.experimental import pallas as pl
from jax.experimental.pallas import tpu as pltpu

D_MODEL = 2048
PAGE_SIZE = 128
HEAD_DIM = 128
N_HEADS_A = 8
N_HEADS_B = 4
DK_B = 128
DV_B = 256
GATE_RANK = 16
GATE_TEMP = 16.0
MOBA_BLOCK = 256
MOBA_TOPK = 3
GLA_CHUNK = 128
D_FF = 4 * D_MODEL
ROPE_THETA = 10000.0
EPS = 1e-6

W_A = N_HEADS_A * HEAD_DIM
W_QB = N_HEADS_B * DK_B
W_VB = N_HEADS_B * DV_B
W_GLA = 2 * W_QB + 2 * W_VB
LANES = 128

F32 = jnp.float32
BF16 = jnp.bfloat16
NEG_INF = float("-inf")
VMEM_LIMIT = 56 * 1024 * 1024

TM_DENSE = 1024
TN_DENSE = 1024
TM_MLP = 1024
TF_MLP_PROMPT = 512
TF_MLP = 512
GATHER_HEADS_PER_STEP = 4
GLA_STEP_SAMPLES = 4
ROPE_SUB_ROWS = 256
MOBA_HEADS_PER_STEP = 2
MOBA_QBLOCK_GROUP = 2
ONES_ROWS = 16
LOG2E = 1.4426950408889634

_NT = (((1,), (1,)), ((), ()))


def _params(*sem):
    return pltpu.CompilerParams(dimension_semantics=sem, vmem_limit_bytes=VMEM_LIMIT)


def _rms(x, w):
    return x * lax.rsqrt(jnp.mean(x * x, axis=-1, keepdims=True) + EPS) * w


def _log_sigmoid(x):
    return jnp.minimum(x, 0.0) - jnp.log(1.0 + jnp.exp(-jnp.abs(x)))


def _silu(x):
    return x / (1.0 + jnp.exp(-x))


def _ada_kernel(c_ref, w_ref, b_ref, o_ref):
    a = _silu(c_ref[...]).astype(BF16)
    o_ref[...] = jnp.dot(a, w_ref[...].astype(BF16), preferred_element_type=F32) + b_ref[...]


def _ada(c, w, b, tn=TN_DENSE):
    m, d = c.shape
    n = w.shape[1]
    return pl.pallas_call(
        _ada_kernel,
        grid=(n // tn,),
        in_specs=[pl.BlockSpec((m, d), lambda j: (0, 0)),
                  pl.BlockSpec((d, tn), lambda j: (0, j)),
                  pl.BlockSpec((1, tn), lambda j: (0, j))],
        out_specs=pl.BlockSpec((m, tn), lambda j: (0, j)),
        out_shape=jax.ShapeDtypeStruct((m, n), F32),
        compiler_params=_params("arbitrary"),
        name="ada_proj",
    )(c, w, b.reshape(1, n))


def _mod_spec(mod, tiles_per_group):
    _, r, w = mod.shape
    return pl.BlockSpec((None, r, w), lambda i, *_: (i // tiles_per_group, 0, 0))


def _mod_rows(ref, rs):
    return ref[...] if ref.shape[0] == 1 else ref[rs, :]


def _weight_spec(k, tn, layer, col_block0, transposed):
    if transposed:
        return pl.BlockSpec((None, tn, k), lambda j, *_: (layer, j + col_block0, 0))
    return pl.BlockSpec((None, k, tn), lambda j, *_: (layer, 0, j + col_block0))


def _weight_dot(a, wb, transposed):
    if transposed:
        return lax.dot_general(a, wb, _NT, preferred_element_type=F32)
    return jnp.dot(a, wb, preferred_element_type=F32)


def _proj_kernel(a_ref, w_ref, o_ref, wb_scr, *, transposed):
    @pl.when(pl.program_id(1) == 0)
    def _cast():
        wb_scr[...] = w_ref[...].astype(BF16)

    o_ref[...] = _weight_dot(a_ref[...], wb_scr[...], transposed).astype(o_ref.dtype)


def _proj(a, w, layer, col_block0, n_out, tm, tn, transposed=False, out_dtype=F32):
    m, k = a.shape
    w_spec = _weight_spec(k, tn, layer, col_block0, transposed)
    return pl.pallas_call(
        functools.partial(_proj_kernel, transposed=transposed),
        grid=(n_out // tn, m // tm),
        in_specs=[pl.BlockSpec((tm, k), lambda j, i: (i, 0)), w_spec],
        out_specs=pl.BlockSpec((tm, tn), lambda j, i: (i, j)),
        out_shape=jax.ShapeDtypeStruct((m, n_out), out_dtype),
        scratch_shapes=[pltpu.VMEM(w_spec.block_shape[1:], BF16)],
        compiler_params=_params("arbitrary", "arbitrary"),
        name="proj",
    )(a, w)


def _rope_heads(y, cos, sin, w, scale):
    out = []
    for h in range(y.shape[1] // HEAD_DIM):
        x = _rms(y[:, h * HEAD_DIM:(h + 1) * HEAD_DIM], w)
        r = x * cos + pltpu.roll(x, HEAD_DIM // 2, 1) * sin
        out.append(r * scale if scale != 1.0 else r)
    return out


def _proj_rope_kernel(a_ref, w_ref, cos_ref, sin_ref, nw_ref, o_ref, wb_scr, *, scale, rows, transposed):
    @pl.when(pl.program_id(0) == 0)
    def _cast():
        wb_scr[...] = w_ref[...].astype(BF16)

    for r0 in range(0, a_ref.shape[0], rows):
        rs = slice(r0, r0 + rows)
        y = _weight_dot(a_ref[rs, :], wb_scr[...], transposed)
        for h, r in enumerate(_rope_heads(y, cos_ref[rs, :], sin_ref[rs, :], nw_ref[...], scale)):
            o_ref[rs, h * HEAD_DIM:(h + 1) * HEAD_DIM] = r.astype(o_ref.dtype)


def _proj_rope(a, w, layer, col_block, cos, sin, norm_w, tm, pos_tiles, scale, transposed, out_dtype):
    m, k = a.shape
    if transposed:
        w_spec = pl.BlockSpec((None, W_A, k), lambda i: (layer, col_block, 0))
    else:
        w_spec = pl.BlockSpec((None, k, W_A), lambda i: (layer, 0, col_block))
    return pl.pallas_call(
        functools.partial(_proj_rope_kernel, scale=scale, rows=min(tm, ROPE_SUB_ROWS), transposed=transposed),
        grid=(m // tm,),
        in_specs=[pl.BlockSpec((tm, k), lambda i: (i, 0)),
                  w_spec,
                  pl.BlockSpec((tm, HEAD_DIM), lambda i: (i % pos_tiles, 0)),
                  pl.BlockSpec((tm, HEAD_DIM), lambda i: (i % pos_tiles, 0)),
                  pl.BlockSpec((1, HEAD_DIM), lambda i: (0, 0))],
        out_specs=pl.BlockSpec((tm, W_A), lambda i: (i, 0)),
        out_shape=jax.ShapeDtypeStruct((m, W_A), out_dtype),
        scratch_shapes=[pltpu.VMEM(w_spec.block_shape[1:], BF16)],
        compiler_params=_params("arbitrary"),
        name="proj_rope",
    )(a, w, cos, sin, norm_w.reshape(1, HEAD_DIM))


def _norm_proj_rope_kernel(x_ref, sc_ref, sh_ref, nmw_ref, w_ref, cos_ref, sin_ref, nw_ref, o_ref, h_ref, wb_scr,
                           *, scale, rows, transposed):
    @pl.when(pl.program_id(0) == 0)
    def _cast():
        wb_scr[...] = w_ref[...].astype(BF16)

    for r0 in range(0, x_ref.shape[0], rows):
        rs = slice(r0, r0 + rows)
        y = _rms(x_ref[rs, :], nmw_ref[...])
        hb = (y * (1.0 + _mod_rows(sc_ref, rs)) + _mod_rows(sh_ref, rs)).astype(BF16)
        h_ref[rs, :] = hb
        y = _weight_dot(hb, wb_scr[...], transposed)
        for h, r in enumerate(_rope_heads(y, cos_ref[rs, :], sin_ref[rs, :], nw_ref[...], scale)):
            o_ref[rs, h * HEAD_DIM:(h + 1) * HEAD_DIM] = r.astype(o_ref.dtype)


def _norm_proj_rope(x, sc, sh, norm_w, tiles_per_group, w, layer, col_block, cos, sin, qk_w, tm, pos_tiles,
                    scale, out_dtype):
    m, k = x.shape
    w_spec = pl.BlockSpec((None, W_A, k), lambda i: (layer, col_block, 0))
    return pl.pallas_call(
        functools.partial(_norm_proj_rope_kernel, scale=scale, rows=min(tm, ROPE_SUB_ROWS), transposed=True),
        grid=(m // tm,),
        in_specs=[pl.BlockSpec((tm, k), lambda i: (i, 0)),
                  _mod_spec(sc, tiles_per_group), _mod_spec(sh, tiles_per_group),
                  pl.BlockSpec((1, k), lambda i: (0, 0)),
                  w_spec,
                  pl.BlockSpec((tm, HEAD_DIM), lambda i: (i % pos_tiles, 0)),
                  pl.BlockSpec((tm, HEAD_DIM), lambda i: (i % pos_tiles, 0)),
                  pl.BlockSpec((1, HEAD_DIM), lambda i: (0, 0))],
        out_specs=[pl.BlockSpec((tm, W_A), lambda i: (i, 0)), pl.BlockSpec((tm, k), lambda i: (i, 0))],
        out_shape=[jax.ShapeDtypeStruct((m, W_A), out_dtype), jax.ShapeDtypeStruct((m, k), BF16)],
        scratch_shapes=[pltpu.VMEM(w_spec.block_shape[1:], BF16)],
        compiler_params=_params("arbitrary"),
        name="norm_proj_rope",
    )(x, sc, sh, norm_w.reshape(1, k), w, cos, sin, qk_w.reshape(1, HEAD_DIM))


def _gate_weight_kernel(a_ref, b_ref, o_ref):
    o_ref[...] = jnp.dot(a_ref[...], b_ref[...], preferred_element_type=F32, precision=lax.Precision.HIGHEST)


def _gate_weight(w_ab, w_gate_up):
    d, r = w_ab.shape
    n = w_gate_up.shape[1]
    return pl.pallas_call(
        _gate_weight_kernel,
        grid=(1,),
        in_specs=[pl.BlockSpec((d, r), lambda i: (0, 0)), pl.BlockSpec((r, n), lambda i: (0, 0))],
        out_specs=pl.BlockSpec((d, n), lambda i: (0, 0)),
        out_shape=jax.ShapeDtypeStruct((d, n), F32),
        compiler_params=_params("arbitrary"),
        name="gate_weight",
    )(w_ab, w_gate_up)


def _rope_tables(pos):
    half = HEAD_DIM // 2
    inv_freq = ROPE_THETA ** (-jnp.arange(half, dtype=F32) / half)
    ang = pos.astype(F32)[:, None] * inv_freq[None, :]
    cos, sin = jnp.cos(ang), jnp.sin(ang)
    return jnp.concatenate([cos, cos], axis=-1), jnp.concatenate([-sin, sin], axis=-1)


def _topk_select(g, valid, axis):
    n = g.shape[axis]
    idx = lax.broadcasted_iota(jnp.int32, g.shape, axis)
    gm = jnp.where(valid, g, NEG_INF)
    rank = jnp.zeros(g.shape, F32)
    for c in range(n):
        gc = lax.slice_in_dim(gm, c, c + 1, axis=axis)
        beats = (gc > gm) | ((gc == gm) & (idx > c))
        rank = rank + beats.astype(F32)
    return valid & (rank < MOBA_TOPK)


def _moba_prompt_kernel(q_ref, k_ref, v_ref, o_ref, kb_scr, vt_scr, km_scr, sel_scr, s_scr, *, nb, hp, group):
    qb = pl.program_id(2)
    blk = MOBA_BLOCK
    heads = [slice(hh * HEAD_DIM, (hh + 1) * HEAD_DIM) for hh in range(hp)]

    @pl.when(qb == 0)
    def _prep():
        def per_block(j, carry):
            rows = pl.ds(pl.multiple_of(j * blk, blk), blk)
            for hh, cols in enumerate(heads):
                kj = k_ref[rows, cols]
                kb_scr[hh, j] = kj.astype(BF16)
                km_scr[hh, pl.ds(j, 1), :] = jnp.sum(kj, axis=0, keepdims=True) * (1.0 / blk)
                vt_scr[hh, j, 0:HEAD_DIM, :] = v_ref[rows, cols].T.astype(BF16)
                vt_scr[hh, j, HEAD_DIM:HEAD_DIM + ONES_ROWS, :] = jnp.ones((ONES_ROWS, blk), BF16)
            return carry

        lax.fori_loop(0, nb, per_block, 0)

    causal = (lax.broadcasted_iota(jnp.int32, (blk, blk), 0) <= lax.broadcasted_iota(jnp.int32, (blk, blk), 1))
    for hh, cols in enumerate(heads):
        gate = lax.dot_general(km_scr[hh].astype(BF16), q_ref[:, cols], _NT, preferred_element_type=F32)
        cand = lax.broadcasted_iota(jnp.int32, gate.shape, 0)
        sel_scr[hh] = (_topk_select(gate, cand < qb, 0) | (cand == qb)).astype(F32)

    def attend(n_keys):
        first_own = n_keys - group
        for hh, cols in enumerate(heads):
            q = q_ref[:, cols]
            m = jnp.full((1, blk), NEG_INF, F32)
            for j in range(n_keys):
                s = lax.dot_general(kb_scr[hh, j], q, _NT, preferred_element_type=F32)
                s = jnp.where(sel_scr[hh, j:j + 1, :] > 0.0, s, NEG_INF)
                s_scr[hh, j] = s
                bmax = jnp.max(s, axis=0, keepdims=True)
                if j >= first_own:
                    bmax = jnp.where(qb == j, NEG_INF, bmax)
                m = jnp.maximum(m, bmax)
            s_own = jnp.where(causal, s_scr[hh, qb], NEG_INF)
            s_scr[hh, qb] = s_own
            m = jnp.maximum(m, jnp.max(s_own, axis=0, keepdims=True))
            acc = jnp.zeros((HEAD_DIM + ONES_ROWS, blk), F32)
            for j in range(n_keys):
                p = jnp.exp2(s_scr[hh, j] - m)
                acc = acc + jnp.dot(vt_scr[hh, j], p.astype(BF16), preferred_element_type=F32)
            l = acc[HEAD_DIM:HEAD_DIM + 1, :]
            o_ref[:, cols] = (acc[0:HEAD_DIM, :] * (1.0 / l)).T.astype(o_ref.dtype)

    for g in range(nb // group):
        pl.when(qb // group == g)(functools.partial(attend, (g + 1) * group))


def _moba_prompt(q, k, v, n_seq, seq):
    nb = seq // MOBA_BLOCK
    hp = MOBA_HEADS_PER_STEP
    group = min(MOBA_QBLOCK_GROUP, nb)
    assert nb % group == 0
    w = hp * HEAD_DIM
    return pl.pallas_call(
        functools.partial(_moba_prompt_kernel, nb=nb, hp=hp, group=group),
        grid=(n_seq, N_HEADS_A // hp, nb),
        in_specs=[pl.BlockSpec((MOBA_BLOCK, w), lambda n, h, b: (n * nb + b, h)),
                  pl.BlockSpec((seq, w), lambda n, h, b: (n, h)),
                  pl.BlockSpec((seq, w), lambda n, h, b: (n, h))],
        out_specs=pl.BlockSpec((MOBA_BLOCK, w), lambda n, h, b: (n * nb + b, h)),
        out_shape=jax.ShapeDtypeStruct((n_seq * seq, W_A), BF16),
        scratch_shapes=[pltpu.VMEM((hp, nb, MOBA_BLOCK, HEAD_DIM), BF16),
                        pltpu.VMEM((hp, nb, HEAD_DIM + ONES_ROWS, MOBA_BLOCK), BF16),
                        pltpu.VMEM((hp, nb, HEAD_DIM), F32),
                        pltpu.VMEM((hp, nb, MOBA_BLOCK), F32),
                        pltpu.VMEM((hp, nb, MOBA_BLOCK, MOBA_BLOCK), F32)],
        compiler_params=_params("arbitrary", "arbitrary", "arbitrary"),
        name="moba_prompt",
    )(q, k, v)


def _gla_levels():
    m, out = 1, []
    while m < GLA_CHUNK:
        out.append(m)
        m *= 2
    return out


def _gla_cum_matrices():
    c = GLA_CHUNK
    t = np.arange(c)
    tri = (t[None, :] <= t[:, None]).astype(np.float32)
    mats = [tri]
    for m in _gla_levels():
        mats.append(tri[(t // (2 * m)) * (2 * m) + m - 1])
    return np.concatenate(mats, axis=0)


def _split_bf16(x):
    hi = x.astype(BF16)
    return hi, (x - hi.astype(F32)).astype(BF16)


def _gla_prompt_kernel(g_ref, pre_ref, cm_ref, bg_ref, nw_ref, o_ref, s_ref):
    c = GLA_CHUNK

    @pl.when(pl.program_id(1) == 0)
    def _init():
        s_ref[...] = jnp.zeros(s_ref.shape, F32)

    tok = lax.broadcasted_iota(jnp.int32, (c, c), 0)
    col = lax.broadcasted_iota(jnp.int32, (c, c), 1)
    tok_d = lax.broadcasted_iota(jnp.int32, (c, DK_B), 0)
    la = _log_sigmoid(pre_ref[...] + bg_ref[...]) * (LOG2E / GATE_TEMP)
    cums_all = sum(jnp.dot(cm_ref[...], piece, preferred_element_type=F32) for piece in _split_bf16(la))
    for h in range(N_HEADS_B):
        dk = slice(h * DK_B, (h + 1) * DK_B)
        cums = cums_all[:, dk]
        b = cums[0:c]
        q = g_ref[:, dk] * (DK_B ** -0.5)
        k = g_ref[:, W_QB + h * DK_B:W_QB + (h + 1) * DK_B]
        v = g_ref[:, 2 * W_QB + h * DV_B:2 * W_QB + (h + 1) * DV_B]
        rb = g_ref[:, 2 * W_QB + W_VB + h * DV_B:2 * W_QB + W_VB + (h + 1) * DV_B]
        vb = v.astype(BF16)

        attn = lax.dot_general(q.astype(BF16), k.astype(BF16), _NT, preferred_element_type=F32)
        attn = jnp.where(col == tok, attn, 0.0)
        for lvl, m in enumerate(_gla_levels()):
            r = cums[(lvl + 1) * c:(lvl + 2) * c]
            upper = ((tok_d // m) % 2) == 1
            x = b - r
            e = jnp.exp2(jnp.where(upper, x, -x))
            e_up = jnp.where(upper, e, 0.0)
            a_m = lax.dot_general((q * e_up).astype(BF16), (k * (e - e_up)).astype(BF16), _NT,
                                  preferred_element_type=F32)
            attn = attn + jnp.where((tok // (2 * m)) == (col // (2 * m)), a_m, 0.0)

        state = s_ref[0, h]
        o = jnp.dot(attn.astype(BF16), vb, preferred_element_type=F32)
        o = o + jnp.dot((q * jnp.exp2(b)).astype(BF16), state.astype(BF16), preferred_element_type=F32)
        b_last = b[c - 1:c, :]
        kd = k * jnp.exp2(b_last - b)
        kv = jnp.dot(kd.T.astype(BF16), vb, preferred_element_type=F32)
        dcol = b.T[:, c - 1:c]
        s_ref[0, h] = jnp.exp2(dcol) * state + kv

        y = _rms(o, nw_ref[...]) * _silu(rb)
        o_ref[:, h * DV_B:(h + 1) * DV_B] = y.astype(o_ref.dtype)


def _gla_prompt(g, pre, bg, nw, n_seq, seq):
    nc = seq // GLA_CHUNK
    cm = jnp.asarray(_gla_cum_matrices(), BF16)
    return pl.pallas_call(
        _gla_prompt_kernel,
        grid=(n_seq, nc),
        in_specs=[pl.BlockSpec((GLA_CHUNK, W_GLA), lambda n, t: (n * nc + t, 0)),
                  pl.BlockSpec((GLA_CHUNK, W_QB), lambda n, t: (n * nc + t, 0)),
                  pl.BlockSpec(cm.shape, lambda n, t: (0, 0)),
                  pl.BlockSpec((1, W_QB), lambda n, t: (0, 0)),
                  pl.BlockSpec((1, DV_B), lambda n, t: (0, 0))],
        out_specs=[pl.BlockSpec((GLA_CHUNK, W_VB), lambda n, t: (n * nc + t, 0)),
                   pl.BlockSpec((1, N_HEADS_B, DK_B, DV_B), lambda n, t: (n, 0, 0, 0))],
        out_shape=[jax.ShapeDtypeStruct((n_seq * seq, W_VB), BF16),
                   jax.ShapeDtypeStruct((n_seq, N_HEADS_B, DK_B, DV_B), F32)],
        compiler_params=_params("arbitrary", "arbitrary"),
        name="gla_prompt",
    )(g, pre, cm, bg.reshape(1, W_QB), nw.reshape(1, DV_B))


def _gla_step_kernel(qc_ref, kc_ref, prec_ref, v_ref, rb_ref, bgc_ref, nw_ref, s0_ref, o_ref, s_ref):
    for i in range(s0_ref.shape[0]):
        for h in range(N_HEADS_B):
            rows = slice(h * DK_B, (h + 1) * DK_B)
            a = jnp.exp(_log_sigmoid(prec_ref[i, rows, :] + bgc_ref[rows, :]) * (1.0 / GATE_TEMP))
            v = v_ref[i, :, h * DV_B:(h + 1) * DV_B]
            s_new = a * s0_ref[i, h] + kc_ref[i, rows, :] * v
            s_ref[i, h] = s_new
            o = jnp.sum((qc_ref[i, rows, :] * (DK_B ** -0.5)) * s_new, axis=0, keepdims=True)
            y = _rms(o, nw_ref[...]) * _silu(rb_ref[i, :, h * DV_B:(h + 1) * DV_B])
            o_ref[i, :, h * DV_B:(h + 1) * DV_B] = y.astype(o_ref.dtype)


def _gla_step(g, pre, bg, nw, s0):
    n = g.shape[0]
    sb = GLA_STEP_SAMPLES if n % GLA_STEP_SAMPLES == 0 else 1
    qc = g[:, 0:W_QB].reshape(n, W_QB, 1)
    kc = g[:, W_QB:2 * W_QB].reshape(n, W_QB, 1)
    v = g[:, 2 * W_QB:2 * W_QB + W_VB].reshape(n, 1, W_VB)
    rb = g[:, 2 * W_QB + W_VB:W_GLA].reshape(n, 1, W_VB)
    per_n3 = lambda i: (i, 0, 0)
    fixed = lambda i: (0, 0)
    o, s = pl.pallas_call(
        _gla_step_kernel,
        grid=(n // sb,),
        in_specs=[pl.BlockSpec((sb, W_QB, 1), per_n3), pl.BlockSpec((sb, W_QB, 1), per_n3),
                  pl.BlockSpec((sb, W_QB, 1), per_n3),
                  pl.BlockSpec((sb, 1, W_VB), per_n3), pl.BlockSpec((sb, 1, W_VB), per_n3),
                  pl.BlockSpec((W_QB, 1), fixed), pl.BlockSpec((1, DV_B), fixed),
                  pl.BlockSpec((sb, N_HEADS_B, DK_B, DV_B), lambda i: (i, 0, 0, 0))],
        out_specs=[pl.BlockSpec((sb, 1, W_VB), per_n3),
                   pl.BlockSpec((sb, N_HEADS_B, DK_B, DV_B), lambda i: (i, 0, 0, 0))],
        out_shape=[jax.ShapeDtypeStruct((n, 1, W_VB), BF16),
                   jax.ShapeDtypeStruct(s0.shape, F32)],
        compiler_params=_params("arbitrary"),
        name="gla_step",
    )(qc, kc, pre.reshape(n, W_QB, 1), v, rb, bg.reshape(W_QB, 1), nw.reshape(1, DV_B), s0)
    return o.reshape(n, W_VB), s


PAGES_PER_STEP = 16


def _page_scores(q_ref, k_refs, o_ref):
    q = q_ref[0].astype(F32)
    shape3 = (PAGE_SIZE, N_HEADS_A, LANES)
    diag = lax.broadcasted_iota(jnp.int32, shape3, 0) == lax.broadcasted_iota(jnp.int32, shape3, 2)
    for p, k_ref in enumerate(k_refs):
        r = jnp.sum(k_ref[0] * q[None], axis=-1, keepdims=True)
        o_ref[0, :, p * PAGE_SIZE:(p + 1) * PAGE_SIZE] = jnp.sum(jnp.where(diag, r, 0.0), axis=0)


def _moba_pick_kernel(s_ref, o_ref, *, nb):
    rows = s_ref.shape[0]
    lane = lax.broadcasted_iota(jnp.int32, (rows, LANES), 1)
    gate = jnp.zeros((rows, LANES), F32)
    for j in range(nb):
        gj = jnp.sum(s_ref[:, j * MOBA_BLOCK:(j + 1) * MOBA_BLOCK], axis=-1, keepdims=True) * (1.0 / MOBA_BLOCK)
        gate = jnp.where(lane == j, gj, gate)
    valid = lane < nb
    gm = jnp.where(valid, gate, NEG_INF)
    rank = jnp.zeros((rows, LANES), F32)
    for c in range(nb):
        gc = gm[:, c:c + 1]
        beats = (gc > gm) | ((gc == gm) & (lane > c))
        rank = rank + beats.astype(F32)
    out = jnp.zeros((rows, LANES), jnp.int32)
    for t in range(MOBA_TOPK):
        pick = jnp.sum(jnp.where(valid & (rank == float(t)), lane, 0), axis=-1, keepdims=True)
        out = jnp.where(lane == t, pick, out)
    o_ref[...] = out


def _moba_pick(scores, nb):
    rows = scores.shape[0]
    return pl.pallas_call(
        functools.partial(_moba_pick_kernel, nb=nb),
        grid=(1,),
        in_specs=[pl.BlockSpec(scores.shape, lambda i: (0, 0))],
        out_specs=pl.BlockSpec((rows, LANES), lambda i: (0, 0)),
        out_shape=jax.ShapeDtypeStruct((rows, LANES), jnp.int32),
        compiler_params=_params("arbitrary"),
        name="moba_pick",
    )(scores)


def _moba_gather_kernel(pt_ref, pick_ref, q_ref, kn_ref, vn_ref, s_ref, *refs):
    per_head = 2 * MOBA_TOPK
    hg = GATHER_HEADS_PER_STEP
    v_refs, o_ref = refs[:hg * per_head], refs[hg * per_head]
    for hh in range(hg):
        row = pl.program_id(0) * N_HEADS_A + pl.program_id(1) * hg + hh
        s_self = jnp.sum(q_ref[0, hh].astype(F32) * kn_ref[0, hh].astype(BF16).astype(F32), axis=-1,
                         keepdims=True)
        s = [s_ref[0, hh, pl.ds(pick_ref[row, t], 1), :] for t in range(MOBA_TOPK)]
        m = s_self
        for st in s:
            m = jnp.maximum(m, jnp.max(st, axis=-1, keepdims=True))
        p_self = jnp.exp(s_self - m)
        l = p_self
        acc = jnp.zeros((8, HEAD_DIM), F32)
        for t, st in enumerate(s):
            p = jnp.exp(st - m)
            l = l + jnp.sum(p, axis=-1, keepdims=True)
            pb = jnp.broadcast_to(p, (8, MOBA_BLOCK)).astype(BF16)
            for half in range(2):
                vpage = v_refs[hh * per_head + 2 * t + half].reshape(PAGE_SIZE, HEAD_DIM)[...].astype(BF16)
                acc = acc + jnp.dot(pb[:, half * PAGE_SIZE:(half + 1) * PAGE_SIZE], vpage,
                                    preferred_element_type=F32)
        out = acc[0:1] + p_self * vn_ref[0, hh].astype(BF16).astype(F32)
        o_ref[0, hh] = (out * (1.0 / l)).astype(o_ref.dtype)


def _moba_gather(cache_v, page_table, picks, scores, q, k_new, v_new, nb):
    n = page_table.shape[0]
    pages_per_block = MOBA_BLOCK // PAGE_SIZE
    assert pages_per_block == 2
    scores4 = scores.reshape(n, N_HEADS_A, nb, MOBA_BLOCK)
    cache_v5 = cache_v.reshape(cache_v.shape[0], PAGE_SIZE, N_HEADS_A, 1, HEAD_DIM)

    hg = GATHER_HEADS_PER_STEP

    def v_spec(hh, t, half):
        def index(i, g, pt, pk):
            h = g * hg + hh
            return (pt[i, pk[i * N_HEADS_A + h, t] * pages_per_block + half], 0, h, 0, 0)
        return pl.BlockSpec((1, PAGE_SIZE, 1, 1, HEAD_DIM), index)

    head_group = lambda i, g, pt, pk: (i, g, 0, 0)
    v_specs = [v_spec(hh, t, half) for hh in range(hg) for t in range(MOBA_TOPK) for half in range(2)]
    return pl.pallas_call(
        _moba_gather_kernel,
        grid_spec=pltpu.PrefetchScalarGridSpec(
            num_scalar_prefetch=2,
            grid=(n, N_HEADS_A // hg),
            in_specs=[pl.BlockSpec((1, hg, 1, HEAD_DIM), head_group)] * 3
                     + [pl.BlockSpec((1, hg, nb, MOBA_BLOCK), head_group)]
                     + v_specs,
            out_specs=pl.BlockSpec((1, hg, 1, HEAD_DIM), head_group),
        ),
        out_shape=jax.ShapeDtypeStruct((n, N_HEADS_A, 1, HEAD_DIM), BF16),
        compiler_params=_params("arbitrary", "arbitrary"),
        name="moba_gather",
    )(page_table, picks,
      q.reshape(n, N_HEADS_A, 1, HEAD_DIM), k_new.reshape(n, N_HEADS_A, 1, HEAD_DIM),
      v_new.reshape(n, N_HEADS_A, 1, HEAD_DIM), scores4,
      *([cache_v5] * len(v_specs)))


def _out_proj_kernel(oa_ref, ob_ref, w_ref, x_ref, g_ref, o_ref, wb_scr):
    @pl.when(pl.program_id(1) == 0)
    def _cast():
        wb_scr[...] = w_ref[...].astype(BF16)

    mix = jnp.dot(oa_ref[...], wb_scr[0:W_A, :], preferred_element_type=F32)
    mix = mix + jnp.dot(ob_ref[...], wb_scr[W_A:W_A + W_VB, :], preferred_element_type=F32)
    o_ref[...] = x_ref[...] + g_ref[...] * mix


def _out_proj(oa, ob, w, layer, x, gate, tm, tn, tiles_per_group):
    m, d = x.shape
    _, r, _ = gate.shape
    return pl.pallas_call(
        _out_proj_kernel,
        grid=(d // tn, m // tm),
        in_specs=[pl.BlockSpec((tm, W_A), lambda j, i: (i, 0)),
                  pl.BlockSpec((tm, W_VB), lambda j, i: (i, 0)),
                  pl.BlockSpec((None, W_A + W_VB, tn), lambda j, i: (layer, 0, j)),
                  pl.BlockSpec((tm, tn), lambda j, i: (i, j)),
                  pl.BlockSpec((None, r, tn), lambda j, i: (i // tiles_per_group, 0, j))],
        out_specs=pl.BlockSpec((tm, tn), lambda j, i: (i, j)),
        out_shape=jax.ShapeDtypeStruct((m, d), F32),
        scratch_shapes=[pltpu.VMEM((W_A + W_VB, tn), BF16)],
        compiler_params=_params("arbitrary", "arbitrary"),
        name="out_proj",
    )(oa, ob, w, x, gate)


def _mlp_scores_kernel(pt_ref, x_ref, sc_ref, sh_ref, g_ref, nw_ref, wu_ref, wd_ref, q_ref, *refs):
    k_refs = refs[:PAGES_PER_STEP]
    o_ref, s_ref, h_scr = refs[PAGES_PER_STEP:]
    _mlp_kernel(x_ref, sc_ref, sh_ref, g_ref, nw_ref, wu_ref, wd_ref, o_ref, h_scr,
                beside_matmuls=functools.partial(_page_scores, q_ref, k_refs, s_ref))


def _mlp_kernel(x_ref, sc_ref, sh_ref, g_ref, nw_ref, wu_ref, wd_ref, o_ref, h_scr, beside_matmuls=None):
    f = pl.program_id(1)
    d = o_ref.shape[1]
    tn = min(d, TN_DENSE)

    @pl.when(f == 0)
    def _first():
        y = _rms(x_ref[...], nw_ref[...])
        h_scr[...] = (y * (1.0 + sc_ref[...]) + sh_ref[...]).astype(BF16)
        o_ref[...] = jnp.zeros(o_ref.shape, F32)

    if beside_matmuls is not None:
        beside_matmuls()
    u = jnp.dot(h_scr[...], wu_ref[...], preferred_element_type=F32)
    r = jnp.maximum(u, 0.0)
    r2 = (r * r).astype(BF16)
    for c0 in range(0, d, tn):
        o_ref[:, c0:c0 + tn] += jnp.dot(r2, wd_ref[:, c0:c0 + tn], preferred_element_type=F32)

    @pl.when(f == pl.num_programs(1) - 1)
    def _last():
        o_ref[...] = x_ref[...] + g_ref[...] * o_ref[...]


def _mlp(x, sc, sh, gate, nw, wu, wd, tm, tf, tiles_per_group):
    m, d = x.shape
    ff = wu.shape[1]
    return pl.pallas_call(
        _mlp_kernel,
        grid=(m // tm, ff // tf),
        in_specs=[pl.BlockSpec((tm, d), lambda i, f: (i, 0)),
                  _mod_spec(sc, tiles_per_group), _mod_spec(sh, tiles_per_group), _mod_spec(gate, tiles_per_group),
                  pl.BlockSpec((1, d), lambda i, f: (0, 0)),
                  pl.BlockSpec((d, tf), lambda i, f: (0, f)),
                  pl.BlockSpec((tf, d), lambda i, f: (f, 0))],
        out_specs=pl.BlockSpec((tm, d), lambda i, f: (i, 0)),
        out_shape=jax.ShapeDtypeStruct((m, d), F32),
        scratch_shapes=[pltpu.VMEM((tm, d), BF16)],
        compiler_params=_params("arbitrary", "arbitrary"),
        name="mlp",
    )(x, sc, sh, gate, nw.reshape(1, d), wu, wd)


def _mlp_with_scores(x, sc, sh, gate, nw, wu, wd, tm, tf, tiles_per_group, cache_k, page_table, q):
    m, d = x.shape
    ff = wu.shape[1]
    n, n_pages = page_table.shape
    nf = ff // tf
    groups = n_pages // PAGES_PER_STEP
    assert (m // tm) * nf == n * groups and n_pages % PAGES_PER_STEP == 0

    def sample(i, f):
        return (i * nf + f) // groups

    def group(i, f):
        return (i * nf + f) % groups

    def page_spec(p):
        return pl.BlockSpec((1, PAGE_SIZE, N_HEADS_A, HEAD_DIM),
                            lambda i, f, pt: (pt[sample(i, f), group(i, f) * PAGES_PER_STEP + p], 0, 0, 0))

    def mod_spec(mod):
        _, r, w = mod.shape
        return pl.BlockSpec((None, r, w), lambda i, f, pt: (i // tiles_per_group, 0, 0))

    y, scores = pl.pallas_call(
        _mlp_scores_kernel,
        grid_spec=pltpu.PrefetchScalarGridSpec(
            num_scalar_prefetch=1,
            grid=(m // tm, nf),
            in_specs=[pl.BlockSpec((tm, d), lambda i, f, pt: (i, 0), pipeline_mode=pl.Buffered(1)),
                      mod_spec(sc), mod_spec(sh), mod_spec(gate),
                      pl.BlockSpec((1, d), lambda i, f, pt: (0, 0)),
                      pl.BlockSpec((d, tf), lambda i, f, pt: (0, f)),
                      pl.BlockSpec((tf, d), lambda i, f, pt: (f, 0)),
                      pl.BlockSpec((1, N_HEADS_A, HEAD_DIM), lambda i, f, pt: (sample(i, f), 0, 0))]
                     + [page_spec(p) for p in range(PAGES_PER_STEP)],
            out_specs=[pl.BlockSpec((tm, d), lambda i, f, pt: (i, 0), pipeline_mode=pl.Buffered(1)),
                       pl.BlockSpec((1, N_HEADS_A, PAGES_PER_STEP * PAGE_SIZE),
                                    lambda i, f, pt: (sample(i, f), 0, group(i, f)))],
            scratch_shapes=[pltpu.VMEM((tm, d), BF16)],
        ),
        out_shape=[jax.ShapeDtypeStruct((m, d), F32),
                   jax.ShapeDtypeStruct((n, N_HEADS_A, n_pages * PAGE_SIZE), F32)],
        compiler_params=_params("arbitrary", "arbitrary"),
        name="mlp_scores",
    )(page_table, x, sc, sh, gate, nw.reshape(1, d), wu, wd, q.reshape(n, N_HEADS_A, HEAD_DIM),
      *([cache_k] * PAGES_PER_STEP))
    return y, scores


def _dense_in(x2, mods, norm_w, w_in_t, layer, w_gate, rope, qk_norm_w, q_scale, tm, tiles_per_group):
    tn = TN_DENSE
    cos, sin = rope
    pos_tiles = cos.shape[0] // tm
    q, h = _norm_proj_rope(x2, mods["sc1"], mods["sh1"], norm_w, tiles_per_group, w_in_t, layer, 0, cos, sin,
                           qk_norm_w[0], tm, pos_tiles, q_scale, BF16)
    k = _proj_rope(h, w_in_t, layer, 1, cos, sin, qk_norm_w[1], tm, pos_tiles, 1.0, True, F32)
    v = _proj(h, w_in_t, layer, 2 * W_A // tn, W_A, tm, tn, transposed=True)
    g = _proj(h, w_in_t, layer, 3 * W_A // tn, W_GLA, tm, tn, transposed=True)
    pre = _proj(h, w_gate, 0, 0, W_QB, tm, W_QB)
    return q, k, v, g, pre


def kernel(x_prompt, x_sample, c_prompt, c_sample, cache_k, cache_v, state_gla, page_table, w_ada, b_ada, norm_mix_w, w_in, q_norm_w, k_norm_w, w_gate_up, b_gate, gla_norm_w, w_out, norm_ffn_w, w_up, w_down):
    n_p, s_p, d = x_prompt.shape
    n_s, s_s, _ = x_sample.shape
    depth = w_ada.shape[0]
    assert depth == 1 and s_s == 1 and d == D_MODEL
    n_pages = page_table.shape[1]
    past_len = n_pages * PAGE_SIZE
    nb_past = past_len // MOBA_BLOCK
    assert past_len % MOBA_BLOCK == 0 and nb_past >= MOBA_TOPK
    l = 0
    tm = min(TM_DENSE, s_p)

    rows = n_p + n_s
    rows_pad = -(-rows // 8) * 8
    c_all = jnp.concatenate([c_prompt, c_sample, jnp.zeros((rows_pad - rows, d), F32)], axis=0)
    ada = _ada(c_all, w_ada[l], b_ada[l])
    names = ("sh1", "sc1", "g1", "sh2", "sc2", "g2")
    mods_p = {nm: ada[0:n_p, i * d:(i + 1) * d].reshape(n_p, 1, d) for i, nm in enumerate(names)}
    mods_s = {nm: ada[n_p:rows, i * d:(i + 1) * d].reshape(1, n_s, d) for i, nm in enumerate(names)}

    w_in_t = jnp.swapaxes(w_in, 1, 2)
    w_gate = _gate_weight(w_in[l, :, 3 * W_A + W_GLA:], w_gate_up[l])[None]
    w_up_b = w_up[l].astype(BF16)
    w_down_b = w_down[l].astype(BF16)
    scale = HEAD_DIM ** -0.5
    qk_norm_w = (q_norm_w[l], k_norm_w[l])

    xp = x_prompt.reshape(n_p * s_p, d)
    rope_p = _rope_tables(jnp.arange(s_p, dtype=jnp.int32))
    q_p, k_p, v_p, g, pre = _dense_in(xp, mods_p, norm_mix_w[l], w_in_t, l, w_gate, rope_p, qk_norm_w,
                                      scale * LOG2E, tm, s_p // tm)
    oa_p = _moba_prompt(q_p, k_p, v_p, n_p, s_p)
    ob_p, gla_p = _gla_prompt(g, pre, b_gate[l], gla_norm_w[l], n_p, s_p)
    x1_p = _out_proj(oa_p, ob_p, w_out, l, xp, mods_p["g1"], tm, TN_DENSE, s_p // tm)

    xs = x_sample.reshape(n_s, d)
    rope_s = _rope_tables(jnp.full((n_s,), past_len, jnp.int32))
    q_s, k_s, v_s, g, pre = _dense_in(xs, mods_s, norm_mix_w[l], w_in_t, l, w_gate, rope_s, qk_norm_w,
                                      scale, n_s, 1)
    tm_mlp = min(TM_MLP, s_p)
    y_p, scores = _mlp_with_scores(x1_p, mods_p["sc2"], mods_p["sh2"], mods_p["g2"], norm_ffn_w[l],
                                   w_up_b, w_down_b, tm_mlp, TF_MLP_PROMPT, s_p // tm_mlp, cache_k[l], page_table, q_s)
    picks = _moba_pick(scores.reshape(n_s * N_HEADS_A, past_len), nb_past)
    oa_s = _moba_gather(cache_v[l], page_table, picks, scores, q_s, k_s, v_s, nb_past).reshape(n_s, W_A)
    ob_s, gla_s = _gla_step(g, pre, b_gate[l], gla_norm_w[l], state_gla[l])
    x1_s = _out_proj(oa_s, ob_s, w_out, l, xs, mods_s["g1"], n_s, TN_DENSE, 1)
    y_s = _mlp(x1_s, mods_s["sc2"], mods_s["sh2"], mods_s["g2"], norm_ffn_w[l], w_up_b, w_down_b,
               n_s, TF_MLP, 1)

    return (y_p.reshape(n_p, s_p, d), y_s.reshape(n_s, s_s, d),
            k_p.reshape(1, n_p, s_p, N_HEADS_A, HEAD_DIM), v_p.reshape(1, n_p, s_p, N_HEADS_A, HEAD_DIM),
            gla_p[None],
            k_s.reshape(1, n_s, s_s, N_HEADS_A, HEAD_DIM), v_s.reshape(1, n_s, s_s, N_HEADS_A, HEAD_DIM),
            gla_s[None])
```

```python
import functools

import numpy as np
import jax
import jax.numpy as jnp
from jax import lax
from jax.experimental import pallas as pl
from jax.experimental.pallas import tpu as pltpu

D_MODEL = 2048
PAGE_SIZE = 128
HEAD_DIM = 128
N_HEADS_A = 8
N_HEADS_B = 4
DK_B = 128
DV_B = 256
GATE_RANK = 16
GATE_TEMP = 16.0
MOBA_BLOCK = 256
MOBA_TOPK = 3
GLA_CHUNK = 128
D_FF = 4 * D_MODEL
ROPE_THETA = 10000.0
EPS = 1e-6

W_A = N_HEADS_A * HEAD_DIM
W_QB = N_HEADS_B * DK_B
W_VB = N_HEADS_B * DV_B
W_GLA = 2 * W_QB + 2 * W_VB
LANES = 128

F32 = jnp.float32
BF16 = jnp.bfloat16
NEG_INF = float("-inf")
VMEM_LIMIT = 56 * 1024 * 1024

TM_DENSE = 1024
TN_DENSE = 1024
TM_MLP = 1024
TF_MLP_PROMPT = 512
TF_MLP = 512
GATHER_HEADS_PER_STEP = 4
GLA_STEP_SAMPLES = 4
ROPE_SUB_ROWS = 256
MOBA_HEADS_PER_STEP = 2
MOBA_QBLOCK_GROUP = 2
ONES_ROWS = 16
LOG2E = 1.4426950408889634

_NT = (((1,), (1,)), ((), ()))


def _params(*sem):
    return pltpu.CompilerParams(dimension_semantics=sem, vmem_limit_bytes=VMEM_LIMIT)


def _rms(x, w):
    return x * lax.rsqrt(jnp.mean(x * x, axis=-1, keepdims=True) + EPS) * w


def _log_sigmoid(x):
    return jnp.minimum(x, 0.0) - jnp.log(1.0 + jnp.exp(-jnp.abs(x)))


def _silu(x):
    return x / (1.0 + jnp.exp(-x))


def _ada_kernel(c_ref, w_ref, b_ref, o_ref):
    a = _silu(c_ref[...]).astype(BF16)
    o_ref[...] = jnp.dot(a, w_ref[...].astype(BF16), preferred_element_type=F32) + b_ref[...]


def _ada(c, w, b, tn=TN_DENSE):
    m, d = c.shape
    n = w.shape[1]
    return pl.pallas_call(
        _ada_kernel,
        grid=(n // tn,),
        in_specs=[pl.BlockSpec((m, d), lambda j: (0, 0)),
                  pl.BlockSpec((d, tn), lambda j: (0, j)),
                  pl.BlockSpec((1, tn), lambda j: (0, j))],
        out_specs=pl.BlockSpec((m, tn), lambda j: (0, j)),
        out_shape=jax.ShapeDtypeStruct((m, n), F32),
        compiler_params=_params("arbitrary"),
        name="ada_proj",
    )(c, w, b.reshape(1, n))


def _mod_spec(mod, tiles_per_group):
    _, r, w = mod.shape
    return pl.BlockSpec((None, r, w), lambda i, *_: (i // tiles_per_group, 0, 0))


def _mod_rows(ref, rs):
    return ref[...] if ref.shape[0] == 1 else ref[rs, :]


def _weight_spec(k, tn, layer, col_block0, transposed):
    if transposed:
        return pl.BlockSpec((None, tn, k), lambda j, *_: (layer, j + col_block0, 0))
    return pl.BlockSpec((None, k, tn), lambda j, *_: (layer, 0, j + col_block0))


def _weight_dot(a, wb, transposed):
    if transposed:
        return lax.dot_general(a, wb, _NT, preferred_element_type=F32)
    return jnp.dot(a, wb, preferred_element_type=F32)


def _proj_kernel(a_ref, w_ref, o_ref, wb_scr, *, transposed):
    @pl.when(pl.program_id(1) == 0)
    def _cast():
        wb_scr[...] = w_ref[...].astype(BF16)

    o_ref[...] = _weight_dot(a_ref[...], wb_scr[...], transposed).astype(o_ref.dtype)


def _proj(a, w, layer, col_block0, n_out, tm, tn, transposed=False, out_dtype=F32):
    m, k = a.shape
    w_spec = _weight_spec(k, tn, layer, col_block0, transposed)
    return pl.pallas_call(
        functools.partial(_proj_kernel, transposed=transposed),
        grid=(n_out // tn, m // tm),
        in_specs=[pl.BlockSpec((tm, k), lambda j, i: (i, 0)), w_spec],
        out_specs=pl.BlockSpec((tm, tn), lambda j, i: (i, j)),
        out_shape=jax.ShapeDtypeStruct((m, n_out), out_dtype),
        scratch_shapes=[pltpu.VMEM(w_spec.block_shape[1:], BF16)],
        compiler_params=_params("arbitrary", "arbitrary"),
        name="proj",
    )(a, w)


def _rope_heads(y, cos, sin, w, scale):
    out = []
    for h in range(y.shape[1] // HEAD_DIM):
        x = _rms(y[:, h * HEAD_DIM:(h + 1) * HEAD_DIM], w)
        r = x * cos + pltpu.roll(x, HEAD_DIM // 2, 1) * sin
        out.append(r * scale if scale != 1.0 else r)
    return out


def _proj_rope_kernel(a_ref, w_ref, cos_ref, sin_ref, nw_ref, o_ref, wb_scr, *, scale, rows, transposed):
    @pl.when(pl.program_id(0) == 0)
    def _cast():
        wb_scr[...] = w_ref[...].astype(BF16)

    for r0 in range(0, a_ref.shape[0], rows):
        rs = slice(r0, r0 + rows)
        y = _weight_dot(a_ref[rs, :], wb_scr[...], transposed)
        for h, r in enumerate(_rope_heads(y, cos_ref[rs, :], sin_ref[rs, :], nw_ref[...], scale)):
            o_ref[rs, h * HEAD_DIM:(h + 1) * HEAD_DIM] = r.astype(o_ref.dtype)


def _proj_rope(a, w, layer, col_block, cos, sin, norm_w, tm, pos_tiles, scale, transposed, out_dtype):
    m, k = a.shape
    if transposed:
        w_spec = pl.BlockSpec((None, W_A, k), lambda i: (layer, col_block, 0))
    else:
        w_spec = pl.BlockSpec((None, k, W_A), lambda i: (layer, 0, col_block))
    return pl.pallas_call(
        functools.partial(_proj_rope_kernel, scale=scale, rows=min(tm, ROPE_SUB_ROWS), transposed=transposed),
        grid=(m // tm,),
        in_specs=[pl.BlockSpec((tm, k), lambda i: (i, 0)),
                  w_spec,
                  pl.BlockSpec((tm, HEAD_DIM), lambda i: (i % pos_tiles, 0)),
                  pl.BlockSpec((tm, HEAD_DIM), lambda i: (i % pos_tiles, 0)),
                  pl.BlockSpec((1, HEAD_DIM), lambda i: (0, 0))],
        out_specs=pl.BlockSpec((tm, W_A), lambda i: (i, 0)),
        out_shape=jax.ShapeDtypeStruct((m, W_A), out_dtype),
        scratch_shapes=[pltpu.VMEM(w_spec.block_shape[1:], BF16)],
        compiler_params=_params("arbitrary"),
        name="proj_rope",
    )(a, w, cos, sin, norm_w.reshape(1, HEAD_DIM))


def _norm_proj_rope_kernel(x_ref, sc_ref, sh_ref, nmw_ref, w_ref, cos_ref, sin_ref, nw_ref, o_ref, h_ref, wb_scr,
                           *, scale, rows, transposed):
    @pl.when(pl.program_id(0) == 0)
    def _cast():
        wb_scr[...] = w_ref[...].astype(BF16)

    for r0 in range(0, x_ref.shape[0], rows):
        rs = slice(r0, r0 + rows)
        y = _rms(x_ref[rs, :], nmw_ref[...])
        hb = (y * (1.0 + _mod_rows(sc_ref, rs)) + _mod_rows(sh_ref, rs)).astype(BF16)
        h_ref[rs, :] = hb
        y = _weight_dot(hb, wb_scr[...], transposed)
        for h, r in enumerate(_rope_heads(y, cos_ref[rs, :], sin_ref[rs, :], nw_ref[...], scale)):
            o_ref[rs, h * HEAD_DIM:(h + 1) * HEAD_DIM] = r.astype(o_ref.dtype)


def _norm_proj_rope(x, sc, sh, norm_w, tiles_per_group, w, layer, col_block, cos, sin, qk_w, tm, pos_tiles,
                    scale, out_dtype):
    m, k = x.shape
    w_spec = pl.BlockSpec((None, W_A, k), lambda i: (layer, col_block, 0))
    return pl.pallas_call(
        functools.partial(_norm_proj_rope_kernel, scale=scale, rows=min(tm, ROPE_SUB_ROWS), transposed=True),
        grid=(m // tm,),
        in_specs=[pl.BlockSpec((tm, k), lambda i: (i, 0)),
                  _mod_spec(sc, tiles_per_group), _mod_spec(sh, tiles_per_group),
                  pl.BlockSpec((1, k), lambda i: (0, 0)),
                  w_spec,
                  pl.BlockSpec((tm, HEAD_DIM), lambda i: (i % pos_tiles, 0)),
                  pl.BlockSpec((tm, HEAD_DIM), lambda i: (i % pos_tiles, 0)),
                  pl.BlockSpec((1, HEAD_DIM), lambda i: (0, 0))],
        out_specs=[pl.BlockSpec((tm, W_A), lambda i: (i, 0)), pl.BlockSpec((tm, k), lambda i: (i, 0))],
        out_shape=[jax.ShapeDtypeStruct((m, W_A), out_dtype), jax.ShapeDtypeStruct((m, k), BF16)],
        scratch_shapes=[pltpu.VMEM(w_spec.block_shape[1:], BF16)],
        compiler_params=_params("arbitrary"),
        name="norm_proj_rope",
    )(x, sc, sh, norm_w.reshape(1, k), w, cos, sin, qk_w.reshape(1, HEAD_DIM))


def _gate_weight_kernel(a_ref, b_ref, o_ref):
    o_ref[...] = jnp.dot(a_ref[...], b_ref[...], preferred_element_type=F32, precision=lax.Precision.HIGHEST)


def _gate_weight(w_ab, w_gate_up):
    d, r = w_ab.shape
    n = w_gate_up.shape[1]
    return pl.pallas_call(
        _gate_weight_kernel,
        grid=(1,),
        in_specs=[pl.BlockSpec((d, r), lambda i: (0, 0)), pl.BlockSpec((r, n), lambda i: (0, 0))],
        out_specs=pl.BlockSpec((d, n), lambda i: (0, 0)),
        out_shape=jax.ShapeDtypeStruct((d, n), F32),
        compiler_params=_params("arbitrary"),
        name="gate_weight",
    )(w_ab, w_gate_up)


def _rope_tables(pos):
    half = HEAD_DIM // 2
    inv_freq = ROPE_THETA ** (-jnp.arange(half, dtype=F32) / half)
    ang = pos.astype(F32)[:, None] * inv_freq[None, :]
    cos, sin = jnp.cos(ang), jnp.sin(ang)
    return jnp.concatenate([cos, cos], axis=-1), jnp.concatenate([-sin, sin], axis=-1)


def _topk_select(g, valid, axis):
    n = g.shape[axis]
    idx = lax.broadcasted_iota(jnp.int32, g.shape, axis)
    gm = jnp.where(valid, g, NEG_INF)
    rank = jnp.zeros(g.shape, F32)
    for c in range(n):
        gc = lax.slice_in_dim(gm, c, c + 1, axis=axis)
        beats = (gc > gm) | ((gc == gm) & (idx > c))
        rank = rank + beats.astype(F32)
    return valid & (rank < MOBA_TOPK)


def _moba_prompt_kernel(q_ref, k_ref, v_ref, wu_ref, wd_ref, o_ref, wub_ref, wdb_ref,
                        kb_scr, vt_scr, km_scr, sel_scr, s_scr, *, nb, hp, group):
    qb = pl.program_id(2)
    blk = MOBA_BLOCK
    wub_ref[...] = wu_ref[...].astype(BF16)
    wdb_ref[...] = wd_ref[...].astype(BF16)
    heads = [slice(hh * HEAD_DIM, (hh + 1) * HEAD_DIM) for hh in range(hp)]

    @pl.when(qb == 0)
    def _prep():
        def per_block(j, carry):
            rows = pl.ds(pl.multiple_of(j * blk, blk), blk)
            for hh, cols in enumerate(heads):
                kj = k_ref[rows, cols]
                kb_scr[hh, j] = kj.astype(BF16)
                km_scr[hh, pl.ds(j, 1), :] = jnp.sum(kj, axis=0, keepdims=True) * (1.0 / blk)
                vt_scr[hh, j, 0:HEAD_DIM, :] = v_ref[rows, cols].T.astype(BF16)
                vt_scr[hh, j, HEAD_DIM:HEAD_DIM + ONES_ROWS, :] = jnp.ones((ONES_ROWS, blk), BF16)
            return carry

        lax.fori_loop(0, nb, per_block, 0)

    causal = (lax.broadcasted_iota(jnp.int32, (blk, blk), 0) <= lax.broadcasted_iota(jnp.int32, (blk, blk), 1))
    for hh, cols in enumerate(heads):
        gate = lax.dot_general(km_scr[hh].astype(BF16), q_ref[:, cols], _NT, preferred_element_type=F32)
        cand = lax.broadcasted_iota(jnp.int32, gate.shape, 0)
        sel_scr[hh] = (_topk_select(gate, cand < qb, 0) | (cand == qb)).astype(F32)

    def attend(n_keys):
        first_own = n_keys - group
        for hh, cols in enumerate(heads):
            q = q_ref[:, cols]
            m = jnp.full((1, blk), NEG_INF, F32)
            for j in range(n_keys):
                s = lax.dot_general(kb_scr[hh, j], q, _NT, preferred_element_type=F32)
                s = jnp.where(sel_scr[hh, j:j + 1, :] > 0.0, s, NEG_INF)
                s_scr[hh, j] = s
                bmax = jnp.max(s, axis=0, keepdims=True)
                if j >= first_own:
                    bmax = jnp.where(qb == j, NEG_INF, bmax)
                m = jnp.maximum(m, bmax)
            s_own = jnp.where(causal, s_scr[hh, qb], NEG_INF)
            s_scr[hh, qb] = s_own
            m = jnp.maximum(m, jnp.max(s_own, axis=0, keepdims=True))
            acc = jnp.zeros((HEAD_DIM + ONES_ROWS, blk), F32)
            for j in range(n_keys):
                p = jnp.exp2(s_scr[hh, j] - m)
                acc = acc + jnp.dot(vt_scr[hh, j], p.astype(BF16), preferred_element_type=F32)
            l = acc[HEAD_DIM:HEAD_DIM + 1, :]
            o_ref[:, cols] = (acc[0:HEAD_DIM, :] * (1.0 / l)).T.astype(o_ref.dtype)

    for g in range(nb // group):
        pl.when(qb // group == g)(functools.partial(attend, (g + 1) * group))


def _moba_prompt(q, k, v, n_seq, seq, w_up, w_down, layer):
    nb = seq // MOBA_BLOCK
    hp = MOBA_HEADS_PER_STEP
    group = min(MOBA_QBLOCK_GROUP, nb)
    assert nb % group == 0
    w = hp * HEAD_DIM
    hgs = N_HEADS_A // hp
    steps = n_seq * hgs * nb
    _, d, ff = w_up.shape
    ru, rd = d // steps, ff // steps
    assert ru * steps == d and rd * steps == ff and ru % 16 == 0 and rd % 16 == 0
    step = lambda n, h, b: (n * hgs + h) * nb + b
    return pl.pallas_call(
        functools.partial(_moba_prompt_kernel, nb=nb, hp=hp, group=group),
        grid=(n_seq, hgs, nb),
        in_specs=[pl.BlockSpec((MOBA_BLOCK, w), lambda n, h, b: (n * nb + b, h)),
                  pl.BlockSpec((seq, w), lambda n, h, b: (n, h)),
                  pl.BlockSpec((seq, w), lambda n, h, b: (n, h)),
                  pl.BlockSpec((None, ru, ff), lambda n, h, b: (layer, step(n, h, b), 0)),
                  pl.BlockSpec((None, rd, d), lambda n, h, b: (layer, step(n, h, b), 0))],
        out_specs=[pl.BlockSpec((MOBA_BLOCK, w), lambda n, h, b: (n * nb + b, h)),
                   pl.BlockSpec((ru, ff), lambda n, h, b: (step(n, h, b), 0)),
                   pl.BlockSpec((rd, d), lambda n, h, b: (step(n, h, b), 0))],
        out_shape=[jax.ShapeDtypeStruct((n_seq * seq, W_A), BF16),
                   jax.ShapeDtypeStruct((d, ff), BF16),
                   jax.ShapeDtypeStruct((ff, d), BF16)],
        scratch_shapes=[pltpu.VMEM((hp, nb, MOBA_BLOCK, HEAD_DIM), BF16),
                        pltpu.VMEM((hp, nb, HEAD_DIM + ONES_ROWS, MOBA_BLOCK), BF16),
                        pltpu.VMEM((hp, nb, HEAD_DIM), F32),
                        pltpu.VMEM((hp, nb, MOBA_BLOCK), F32),
                        pltpu.VMEM((hp, nb, MOBA_BLOCK, MOBA_BLOCK), F32)],
        compiler_params=_params("arbitrary", "arbitrary", "arbitrary"),
        name="moba_prompt",
    )(q, k, v, w_up, w_down)


def _gla_levels():
    m, out = 1, []
    while m < GLA_CHUNK:
        out.append(m)
        m *= 2
    return out


def _gla_cum_matrices():
    c = GLA_CHUNK
    t = np.arange(c)
    tri = (t[None, :] <= t[:, None]).astype(np.float32)
    mats = [tri]
    for m in _gla_levels():
        mats.append(tri[(t // (2 * m)) * (2 * m) + m - 1])
    return np.concatenate(mats, axis=0)


def _split_bf16(x):
    hi = x.astype(BF16)
    return hi, (x - hi.astype(F32)).astype(BF16)


def _gla_prompt_kernel(g_ref, pre_ref, cm_ref, bg_ref, nw_ref, o_ref, s_ref):
    c = GLA_CHUNK

    @pl.when(pl.program_id(1) == 0)
    def _init():
        s_ref[...] = jnp.zeros(s_ref.shape, F32)

    tok = lax.broadcasted_iota(jnp.int32, (c, c), 0)
    col = lax.broadcasted_iota(jnp.int32, (c, c), 1)
    tok_d = lax.broadcasted_iota(jnp.int32, (c, DK_B), 0)
    la = _log_sigmoid(pre_ref[...] + bg_ref[...]) * (LOG2E / GATE_TEMP)
    cums_all = sum(jnp.dot(cm_ref[...], piece, preferred_element_type=F32) for piece in _split_bf16(la))
    for h in range(N_HEADS_B):
        dk = slice(h * DK_B, (h + 1) * DK_B)
        cums = cums_all[:, dk]
        b = cums[0:c]
        q = g_ref[:, dk] * (DK_B ** -0.5)
        k = g_ref[:, W_QB + h * DK_B:W_QB + (h + 1) * DK_B]
        v = g_ref[:, 2 * W_QB + h * DV_B:2 * W_QB + (h + 1) * DV_B]
        rb = g_ref[:, 2 * W_QB + W_VB + h * DV_B:2 * W_QB + W_VB + (h + 1) * DV_B]
        vb = v.astype(BF16)

        attn = lax.dot_general(q.astype(BF16), k.astype(BF16), _NT, preferred_element_type=F32)
        attn = jnp.where(col == tok, attn, 0.0)
        for lvl, m in enumerate(_gla_levels()):
            r = cums[(lvl + 1) * c:(lvl + 2) * c]
            upper = ((tok_d // m) % 2) == 1
            x = b - r
            e = jnp.exp2(jnp.where(upper, x, -x))
            e_up = jnp.where(upper, e, 0.0)
            a_m = lax.dot_general((q * e_up).astype(BF16), (k * (e - e_up)).astype(BF16), _NT,
                                  preferred_element_type=F32)
            attn = attn + jnp.where((tok // (2 * m)) == (col // (2 * m)), a_m, 0.0)

        state = s_ref[0, h]
        o = jnp.dot(attn.astype(BF16), vb, preferred_element_type=F32)
        o = o + jnp.dot((q * jnp.exp2(b)).astype(BF16), state.astype(BF16), preferred_element_type=F32)
        b_last = b[c - 1:c, :]
        kd = k * jnp.exp2(b_last - b)
        kv = jnp.dot(kd.T.astype(BF16), vb, preferred_element_type=F32)
        dcol = b.T[:, c - 1:c]
        s_ref[0, h] = jnp.exp2(dcol) * state + kv

        y = _rms(o, nw_ref[...]) * _silu(rb)
        o_ref[:, h * DV_B:(h + 1) * DV_B] = y.astype(o_ref.dtype)


def _gla_prompt(g, pre, bg, nw, n_seq, seq):
    nc = seq // GLA_CHUNK
    cm = jnp.asarray(_gla_cum_matrices(), BF16)
    return pl.pallas_call(
        _gla_prompt_kernel,
        grid=(n_seq, nc),
        in_specs=[pl.BlockSpec((GLA_CHUNK, W_GLA), lambda n, t: (n * nc + t, 0)),
                  pl.BlockSpec((GLA_CHUNK, W_QB), lambda n, t: (n * nc + t, 0)),
                  pl.BlockSpec(cm.shape, lambda n, t: (0, 0)),
                  pl.BlockSpec((1, W_QB), lambda n, t: (0, 0)),
                  pl.BlockSpec((1, DV_B), lambda n, t: (0, 0))],
        out_specs=[pl.BlockSpec((GLA_CHUNK, W_VB), lambda n, t: (n * nc + t, 0)),
                   pl.BlockSpec((1, N_HEADS_B, DK_B, DV_B), lambda n, t: (n, 0, 0, 0))],
        out_shape=[jax.ShapeDtypeStruct((n_seq * seq, W_VB), BF16),
                   jax.ShapeDtypeStruct((n_seq, N_HEADS_B, DK_B, DV_B), F32)],
        compiler_params=_params("arbitrary", "arbitrary"),
        name="gla_prompt",
    )(g, pre, cm, bg.reshape(1, W_QB), nw.reshape(1, DV_B))


def _gla_step_kernel(qc_ref, kc_ref, prec_ref, v_ref, rb_ref, bgc_ref, nw_ref, s0_ref, o_ref, s_ref):
    for i in range(s0_ref.shape[0]):
        for h in range(N_HEADS_B):
            rows = slice(h * DK_B, (h + 1) * DK_B)
            a = jnp.exp(_log_sigmoid(prec_ref[i, rows, :] + bgc_ref[rows, :]) * (1.0 / GATE_TEMP))
            v = v_ref[i, :, h * DV_B:(h + 1) * DV_B]
            s_new = a * s0_ref[i, h] + kc_ref[i, rows, :] * v
            s_ref[i, h] = s_new
            o = jnp.sum((qc_ref[i, rows, :] * (DK_B ** -0.5)) * s_new, axis=0, keepdims=True)
            y = _rms(o, nw_ref[...]) * _silu(rb_ref[i, :, h * DV_B:(h + 1) * DV_B])
            o_ref[i, :, h * DV_B:(h + 1) * DV_B] = y.astype(o_ref.dtype)


def _gla_step(g, pre, bg, nw, s0):
    n = g.shape[0]
    sb = GLA_STEP_SAMPLES if n % GLA_STEP_SAMPLES == 0 else 1
    qc = g[:, 0:W_QB].reshape(n, W_QB, 1)
    kc = g[:, W_QB:2 * W_QB].reshape(n, W_QB, 1)
    v = g[:, 2 * W_QB:2 * W_QB + W_VB].reshape(n, 1, W_VB)
    rb = g[:, 2 * W_QB + W_VB:W_GLA].reshape(n, 1, W_VB)
    per_n3 = lambda i: (i, 0, 0)
    fixed = lambda i: (0, 0)
    o, s = pl.pallas_call(
        _gla_step_kernel,
        grid=(n // sb,),
        in_specs=[pl.BlockSpec((sb, W_QB, 1), per_n3), pl.BlockSpec((sb, W_QB, 1), per_n3),
                  pl.BlockSpec((sb, W_QB, 1), per_n3),
                  pl.BlockSpec((sb, 1, W_VB), per_n3), pl.BlockSpec((sb, 1, W_VB), per_n3),
                  pl.BlockSpec((W_QB, 1), fixed), pl.BlockSpec((1, DV_B), fixed),
                  pl.BlockSpec((sb, N_HEADS_B, DK_B, DV_B), lambda i: (i, 0, 0, 0))],
        out_specs=[pl.BlockSpec((sb, 1, W_VB), per_n3),
                   pl.BlockSpec((sb, N_HEADS_B, DK_B, DV_B), lambda i: (i, 0, 0, 0))],
        out_shape=[jax.ShapeDtypeStruct((n, 1, W_VB), BF16),
                   jax.ShapeDtypeStruct(s0.shape, F32)],
        compiler_params=_params("arbitrary"),
        name="gla_step",
    )(qc, kc, pre.reshape(n, W_QB, 1), v, rb, bg.reshape(W_QB, 1), nw.reshape(1, DV_B), s0)
    return o.reshape(n, W_VB), s


PAGES_PER_STEP = 16


def _page_scores(q_ref, k_refs, o_ref):
    q = q_ref[0].astype(F32)
    shape3 = (PAGE_SIZE, N_HEADS_A, LANES)
    diag = lax.broadcasted_iota(jnp.int32, shape3, 0) == lax.broadcasted_iota(jnp.int32, shape3, 2)
    for p, k_ref in enumerate(k_refs):
        r = jnp.sum(k_ref[0] * q[None], axis=-1, keepdims=True)
        o_ref[0, :, p * PAGE_SIZE:(p + 1) * PAGE_SIZE] = jnp.sum(jnp.where(diag, r, 0.0), axis=0)


def _moba_pick_kernel(s_ref, o_ref, *, nb):
    rows = s_ref.shape[0]
    lane = lax.broadcasted_iota(jnp.int32, (rows, LANES), 1)
    gate = jnp.zeros((rows, LANES), F32)
    for j in range(nb):
        gj = jnp.sum(s_ref[:, j * MOBA_BLOCK:(j + 1) * MOBA_BLOCK], axis=-1, keepdims=True) * (1.0 / MOBA_BLOCK)
        gate = jnp.where(lane == j, gj, gate)
    valid = lane < nb
    gm = jnp.where(valid, gate, NEG_INF)
    rank = jnp.zeros((rows, LANES), F32)
    for c in range(nb):
        gc = gm[:, c:c + 1]
        beats = (gc > gm) | ((gc == gm) & (lane > c))
        rank = rank + beats.astype(F32)
    out = jnp.zeros((rows, LANES), jnp.int32)
    for t in range(MOBA_TOPK):
        pick = jnp.sum(jnp.where(valid & (rank == float(t)), lane, 0), axis=-1, keepdims=True)
        out = jnp.where(lane == t, pick, out)
    o_ref[...] = out


def _moba_pick(scores, nb):
    rows = scores.shape[0]
    return pl.pallas_call(
        functools.partial(_moba_pick_kernel, nb=nb),
        grid=(1,),
        in_specs=[pl.BlockSpec(scores.shape, lambda i: (0, 0))],
        out_specs=pl.BlockSpec((rows, LANES), lambda i: (0, 0)),
        out_shape=jax.ShapeDtypeStruct((rows, LANES), jnp.int32),
        compiler_params=_params("arbitrary"),
        name="moba_pick",
    )(scores)


def _moba_gather_kernel(pt_ref, pick_ref, q_ref, kn_ref, vn_ref, s_ref, *refs):
    per_head = 2 * MOBA_TOPK
    hg = GATHER_HEADS_PER_STEP
    v_refs, o_ref = refs[:hg * per_head], refs[hg * per_head]
    for hh in range(hg):
        row = pl.program_id(0) * N_HEADS_A + pl.program_id(1) * hg + hh
        s_self = jnp.sum(q_ref[0, hh].astype(F32) * kn_ref[0, hh].astype(BF16).astype(F32), axis=-1,
                         keepdims=True)
        s = [s_ref[0, hh, pl.ds(pick_ref[row, t], 1), :] for t in range(MOBA_TOPK)]
        m = s_self
        for st in s:
            m = jnp.maximum(m, jnp.max(st, axis=-1, keepdims=True))
        p_self = jnp.exp(s_self - m)
        l = p_self
        acc = jnp.zeros((8, HEAD_DIM), F32)
        for t, st in enumerate(s):
            p = jnp.exp(st - m)
            l = l + jnp.sum(p, axis=-1, keepdims=True)
            pb = jnp.broadcast_to(p, (8, MOBA_BLOCK)).astype(BF16)
            for half in range(2):
                vpage = v_refs[hh * per_head + 2 * t + half].reshape(PAGE_SIZE, HEAD_DIM)[...].astype(BF16)
                acc = acc + jnp.dot(pb[:, half * PAGE_SIZE:(half + 1) * PAGE_SIZE], vpage,
                                    preferred_element_type=F32)
        out = acc[0:1] + p_self * vn_ref[0, hh].astype(BF16).astype(F32)
        o_ref[0, hh] = (out * (1.0 / l)).astype(o_ref.dtype)


def _moba_gather(cache_v, page_table, picks, scores, q, k_new, v_new, nb):
    n = page_table.shape[0]
    pages_per_block = MOBA_BLOCK // PAGE_SIZE
    assert pages_per_block == 2
    scores4 = scores.reshape(n, N_HEADS_A, nb, MOBA_BLOCK)
    cache_v5 = cache_v.reshape(cache_v.shape[0], PAGE_SIZE, N_HEADS_A, 1, HEAD_DIM)

    hg = GATHER_HEADS_PER_STEP

    def v_spec(hh, t, half):
        def index(i, g, pt, pk):
            h = g * hg + hh
            return (pt[i, pk[i * N_HEADS_A + h, t] * pages_per_block + half], 0, h, 0, 0)
        return pl.BlockSpec((1, PAGE_SIZE, 1, 1, HEAD_DIM), index)

    head_group = lambda i, g, pt, pk: (i, g, 0, 0)
    v_specs = [v_spec(hh, t, half) for hh in range(hg) for t in range(MOBA_TOPK) for half in range(2)]
    return pl.pallas_call(
        _moba_gather_kernel,
        grid_spec=pltpu.PrefetchScalarGridSpec(
            num_scalar_prefetch=2,
            grid=(n, N_HEADS_A // hg),
            in_specs=[pl.BlockSpec((1, hg, 1, HEAD_DIM), head_group)] * 3
                     + [pl.BlockSpec((1, hg, nb, MOBA_BLOCK), head_group)]
                     + v_specs,
            out_specs=pl.BlockSpec((1, hg, 1, HEAD_DIM), head_group),
        ),
        out_shape=jax.ShapeDtypeStruct((n, N_HEADS_A, 1, HEAD_DIM), BF16),
        compiler_params=_params("arbitrary", "arbitrary"),
        name="moba_gather",
    )(page_table, picks,
      q.reshape(n, N_HEADS_A, 1, HEAD_DIM), k_new.reshape(n, N_HEADS_A, 1, HEAD_DIM),
      v_new.reshape(n, N_HEADS_A, 1, HEAD_DIM), scores4,
      *([cache_v5] * len(v_specs)))


def _out_proj_kernel(oa_ref, ob_ref, w_ref, x_ref, g_ref, o_ref, wb_scr):
    @pl.when(pl.program_id(1) == 0)
    def _cast():
        wb_scr[...] = w_ref[...].astype(BF16)

    mix = jnp.dot(oa_ref[...], wb_scr[0:W_A, :], preferred_element_type=F32)
    mix = mix + jnp.dot(ob_ref[...], wb_scr[W_A:W_A + W_VB, :], preferred_element_type=F32)
    o_ref[...] = x_ref[...] + g_ref[...] * mix


def _out_proj(oa, ob, w, layer, x, gate, tm, tn, tiles_per_group):
    m, d = x.shape
    _, r, _ = gate.shape
    return pl.pallas_call(
        _out_proj_kernel,
        grid=(d // tn, m // tm),
        in_specs=[pl.BlockSpec((tm, W_A), lambda j, i: (i, 0)),
                  pl.BlockSpec((tm, W_VB), lambda j, i: (i, 0)),
                  pl.BlockSpec((None, W_A + W_VB, tn), lambda j, i: (layer, 0, j)),
                  pl.BlockSpec((tm, tn), lambda j, i: (i, j)),
                  pl.BlockSpec((None, r, tn), lambda j, i: (i // tiles_per_group, 0, j))],
        out_specs=pl.BlockSpec((tm, tn), lambda j, i: (i, j)),
        out_shape=jax.ShapeDtypeStruct((m, d), F32),
        scratch_shapes=[pltpu.VMEM((W_A + W_VB, tn), BF16)],
        compiler_params=_params("arbitrary", "arbitrary"),
        name="out_proj",
    )(oa, ob, w, x, gate)


def _mlp_scores_kernel(pt_ref, x_ref, sc_ref, sh_ref, g_ref, nw_ref, wu_ref, wd_ref, q_ref, *refs):
    k_refs = refs[:PAGES_PER_STEP]
    o_ref, s_ref, h_scr = refs[PAGES_PER_STEP:]
    _mlp_kernel(x_ref, sc_ref, sh_ref, g_ref, nw_ref, wu_ref, wd_ref, o_ref, h_scr,
                beside_matmuls=functools.partial(_page_scores, q_ref, k_refs, s_ref))


def _mlp_kernel(x_ref, sc_ref, sh_ref, g_ref, nw_ref, wu_ref, wd_ref, o_ref, h_scr, beside_matmuls=None):
    f = pl.program_id(1)
    d = o_ref.shape[1]
    tn = min(d, TN_DENSE)

    @pl.when(f == 0)
    def _first():
        y = _rms(x_ref[...], nw_ref[...])
        h_scr[...] = (y * (1.0 + sc_ref[...]) + sh_ref[...]).astype(BF16)
        o_ref[...] = jnp.zeros(o_ref.shape, F32)

    if beside_matmuls is not None:
        beside_matmuls()
    u = jnp.dot(h_scr[...], wu_ref[...], preferred_element_type=F32)
    r = jnp.maximum(u, 0.0)
    r2 = (r * r).astype(BF16)
    for c0 in range(0, d, tn):
        o_ref[:, c0:c0 + tn] += jnp.dot(r2, wd_ref[:, c0:c0 + tn], preferred_element_type=F32)

    @pl.when(f == pl.num_programs(1) - 1)
    def _last():
        o_ref[...] = x_ref[...] + g_ref[...] * o_ref[...]


def _mlp(x, sc, sh, gate, nw, wu, wd, tm, tf, tiles_per_group):
    m, d = x.shape
    ff = wu.shape[1]
    return pl.pallas_call(
        _mlp_kernel,
        grid=(m // tm, ff // tf),
        in_specs=[pl.BlockSpec((tm, d), lambda i, f: (i, 0)),
                  _mod_spec(sc, tiles_per_group), _mod_spec(sh, tiles_per_group), _mod_spec(gate, tiles_per_group),
                  pl.BlockSpec((1, d), lambda i, f: (0, 0)),
                  pl.BlockSpec((d, tf), lambda i, f: (0, f)),
                  pl.BlockSpec((tf, d), lambda i, f: (f, 0))],
        out_specs=pl.BlockSpec((tm, d), lambda i, f: (i, 0)),
        out_shape=jax.ShapeDtypeStruct((m, d), F32),
        scratch_shapes=[pltpu.VMEM((tm, d), BF16)],
        compiler_params=_params("arbitrary", "arbitrary"),
        name="mlp",
    )(x, sc, sh, gate, nw.reshape(1, d), wu, wd)


def _mlp_with_scores(x, sc, sh, gate, nw, wu, wd, tm, tf, tiles_per_group, cache_k, page_table, q):
    m, d = x.shape
    ff = wu.shape[1]
    n, n_pages = page_table.shape
    nf = ff // tf
    groups = n_pages // PAGES_PER_STEP
    assert (m // tm) * nf == n * groups and n_pages % PAGES_PER_STEP == 0

    def sample(i, f):
        return (i * nf + f) // groups

    def group(i, f):
        return (i * nf + f) % groups

    def page_spec(p):
        return pl.BlockSpec((1, PAGE_SIZE, N_HEADS_A, HEAD_DIM),
                            lambda i, f, pt: (pt[sample(i, f), group(i, f) * PAGES_PER_STEP + p], 0, 0, 0))

    def mod_spec(mod):
        _, r, w = mod.shape
        return pl.BlockSpec((None, r, w), lambda i, f, pt: (i // tiles_per_group, 0, 0))

    y, scores = pl.pallas_call(
        _mlp_scores_kernel,
        grid_spec=pltpu.PrefetchScalarGridSpec(
            num_scalar_prefetch=1,
            grid=(m // tm, nf),
            in_specs=[pl.BlockSpec((tm, d), lambda i, f, pt: (i, 0), pipeline_mode=pl.Buffered(1)),
                      mod_spec(sc), mod_spec(sh), mod_spec(gate),
                      pl.BlockSpec((1, d), lambda i, f, pt: (0, 0)),
                      pl.BlockSpec((d, tf), lambda i, f, pt: (0, f)),
                      pl.BlockSpec((tf, d), lambda i, f, pt: (f, 0)),
                      pl.BlockSpec((1, N_HEADS_A, HEAD_DIM), lambda i, f, pt: (sample(i, f), 0, 0))]
                     + [page_spec(p) for p in range(PAGES_PER_STEP)],
            out_specs=[pl.BlockSpec((tm, d), lambda i, f, pt: (i, 0), pipeline_mode=pl.Buffered(1)),
                       pl.BlockSpec((1, N_HEADS_A, PAGES_PER_STEP * PAGE_SIZE),
                                    lambda i, f, pt: (sample(i, f), 0, group(i, f)))],
            scratch_shapes=[pltpu.VMEM((tm, d), BF16)],
        ),
        out_shape=[jax.ShapeDtypeStruct((m, d), F32),
                   jax.ShapeDtypeStruct((n, N_HEADS_A, n_pages * PAGE_SIZE), F32)],
        compiler_params=_params("arbitrary", "arbitrary"),
        name="mlp_scores",
    )(page_table, x, sc, sh, gate, nw.reshape(1, d), wu, wd, q.reshape(n, N_HEADS_A, HEAD_DIM),
      *([cache_k] * PAGES_PER_STEP))
    return y, scores


def _dense_in(x2, mods, norm_w, w_in_t, layer, w_gate, rope, qk_norm_w, q_scale, tm, tiles_per_group):
    tn = TN_DENSE
    cos, sin = rope
    pos_tiles = cos.shape[0] // tm
    q, h = _norm_proj_rope(x2, mods["sc1"], mods["sh1"], norm_w, tiles_per_group, w_in_t, layer, 0, cos, sin,
                           qk_norm_w[0], tm, pos_tiles, q_scale, BF16)
    k = _proj_rope(h, w_in_t, layer, 1, cos, sin, qk_norm_w[1], tm, pos_tiles, 1.0, True, F32)
    v = _proj(h, w_in_t, layer, 2 * W_A // tn, W_A, tm, tn, transposed=True)
    g = _proj(h, w_in_t, layer, 3 * W_A // tn, W_GLA, tm, tn, transposed=True)
    pre = _proj(h, w_gate, 0, 0, W_QB, tm, W_QB)
    return q, k, v, g, pre


def kernel(x_prompt, x_sample, c_prompt, c_sample, cache_k, cache_v, state_gla, page_table, w_ada, b_ada, norm_mix_w, w_in, q_norm_w, k_norm_w, w_gate_up, b_gate, gla_norm_w, w_out, norm_ffn_w, w_up, w_down):
    n_p, s_p, d = x_prompt.shape
    n_s, s_s, _ = x_sample.shape
    depth = w_ada.shape[0]
    assert depth == 1 and s_s == 1 and d == D_MODEL
    n_pages = page_table.shape[1]
    past_len = n_pages * PAGE_SIZE
    nb_past = past_len // MOBA_BLOCK
    assert past_len % MOBA_BLOCK == 0 and nb_past >= MOBA_TOPK
    l = 0
    tm = min(TM_DENSE, s_p)

    rows = n_p + n_s
    rows_pad = -(-rows // 8) * 8
    c_all = jnp.concatenate([c_prompt, c_sample, jnp.zeros((rows_pad - rows, d), F32)], axis=0)
    ada = _ada(c_all, w_ada[l], b_ada[l])
    names = ("sh1", "sc1", "g1", "sh2", "sc2", "g2")
    mods_p = {nm: ada[0:n_p, i * d:(i + 1) * d].reshape(n_p, 1, d) for i, nm in enumerate(names)}
    mods_s = {nm: ada[n_p:rows, i * d:(i + 1) * d].reshape(1, n_s, d) for i, nm in enumerate(names)}

    w_in_t = jnp.swapaxes(w_in, 1, 2)
    w_gate = _gate_weight(w_in[l, :, 3 * W_A + W_GLA:], w_gate_up[l])[None]
    scale = HEAD_DIM ** -0.5
    qk_norm_w = (q_norm_w[l], k_norm_w[l])

    xp = x_prompt.reshape(n_p * s_p, d)
    rope_p = _rope_tables(jnp.arange(s_p, dtype=jnp.int32))
    q_p, k_p, v_p, g, pre = _dense_in(xp, mods_p, norm_mix_w[l], w_in_t, l, w_gate, rope_p, qk_norm_w,
                                      scale * LOG2E, tm, s_p // tm)
    oa_p, w_up_b, w_down_b = _moba_prompt(q_p, k_p, v_p, n_p, s_p, w_up, w_down, l)
    ob_p, gla_p = _gla_prompt(g, pre, b_gate[l], gla_norm_w[l], n_p, s_p)
    x1_p = _out_proj(oa_p, ob_p, w_out, l, xp, mods_p["g1"], tm, TN_DENSE, s_p // tm)

    xs = x_sample.reshape(n_s, d)
    rope_s = _rope_tables(jnp.full((n_s,), past_len, jnp.int32))
    q_s, k_s, v_s, g, pre = _dense_in(xs, mods_s, norm_mix_w[l], w_in_t, l, w_gate, rope_s, qk_norm_w,
                                      scale, n_s, 1)
    tm_mlp = min(TM_MLP, s_p)
    y_p, scores = _mlp_with_scores(x1_p, mods_p["sc2"], mods_p["sh2"], mods_p["g2"], norm_ffn_w[l],
                                   w_up_b, w_down_b, tm_mlp, TF_MLP_PROMPT, s_p // tm_mlp, cache_k[l], page_table, q_s)
    picks = _moba_pick(scores.reshape(n_s * N_HEADS_A, past_len), nb_past)
    oa_s = _moba_gather(cache_v[l], page_table, picks, scores, q_s, k_s, v_s, nb_past).reshape(n_s, W_A)
    ob_s, gla_s = _gla_step(g, pre, b_gate[l], gla_norm_w[l], state_gla[l])
    x1_s = _out_proj(oa_s, ob_s, w_out, l, xs, mods_s["g1"], n_s, TN_DENSE, 1)
    y_s = _mlp(x1_s, mods_s["sc2"], mods_s["sh2"], mods_s["g2"], norm_ffn_w[l], w_up_b, w_down_b,
               n_s, TF_MLP, 1)

    return (y_p.reshape(n_p, s_p, d), y_s.reshape(n_s, s_s, d),
            k_p.reshape(1, n_p, s_p, N_HEADS_A, HEAD_DIM), v_p.reshape(1, n_p, s_p, N_HEADS_A, HEAD_DIM),
            gla_p[None],
            k_s.reshape(1, n_s, s_s, N_HEADS_A, HEAD_DIM), v_s.reshape(1, n_s, s_s, N_HEADS_A, HEAD_DIM),
            gla_s[None])
```

```python
import functools

import numpy as np
import jax
import jax.numpy as jnp
from jax import lax
from jax.experimental import pallas as pl
from jax.experimental.pallas import tpu as pltpu

D_MODEL = 2048
PAGE_SIZE = 128
HEAD_DIM = 128
N_HEADS_A = 8
N_HEADS_B = 4
DK_B = 128
DV_B = 256
GATE_RANK = 16
GATE_TEMP = 16.0
MOBA_BLOCK = 256
MOBA_TOPK = 3
GLA_CHUNK = 128
D_FF = 4 * D_MODEL
ROPE_THETA = 10000.0
EPS = 1e-6

W_A = N_HEADS_A * HEAD_DIM
W_QB = N_HEADS_B * DK_B
W_VB = N_HEADS_B * DV_B
W_GLA = 2 * W_QB + 2 * W_VB
LANES = 128

F32 = jnp.float32
BF16 = jnp.bfloat16
NEG_INF = float("-inf")
VMEM_LIMIT = 56 * 1024 * 1024

TM_DENSE = 1024
TN_DENSE = 1024
TM_MLP = 1024
TF_MLP_PROMPT = 512
TF_MLP = 512
GATHER_HEADS_PER_STEP = 4
GLA_STEP_SAMPLES = 4
ROPE_SUB_ROWS = 256
MOBA_HEADS_PER_STEP = 2
MOBA_QBLOCK_GROUP = 2
ONES_ROWS = 16
LOG2E = 1.4426950408889634

_NT = (((1,), (1,)), ((), ()))


def _params(*sem):
    return pltpu.CompilerParams(dimension_semantics=sem, vmem_limit_bytes=VMEM_LIMIT)


def _rms(x, w):
    return x * lax.rsqrt(jnp.mean(x * x, axis=-1, keepdims=True) + EPS) * w


def _log_sigmoid(x):
    return jnp.minimum(x, 0.0) - jnp.log(1.0 + jnp.exp(-jnp.abs(x)))


def _silu(x):
    return x / (1.0 + jnp.exp(-x))


def _ada_kernel(c_ref, w_ref, b_ref, o_ref):
    a = _silu(c_ref[...]).astype(BF16)
    o_ref[...] = jnp.dot(a, w_ref[...].astype(BF16), preferred_element_type=F32) + b_ref[...]


def _ada(c, w, b, tn=TN_DENSE):
    m, d = c.shape
    n = w.shape[1]
    return pl.pallas_call(
        _ada_kernel,
        grid=(n // tn,),
        in_specs=[pl.BlockSpec((m, d), lambda j: (0, 0)),
                  pl.BlockSpec((d, tn), lambda j: (0, j)),
                  pl.BlockSpec((1, tn), lambda j: (0, j))],
        out_specs=pl.BlockSpec((m, tn), lambda j: (0, j)),
        out_shape=jax.ShapeDtypeStruct((m, n), F32),
        compiler_params=_params("arbitrary"),
        name="ada_proj",
    )(c, w, b.reshape(1, n))


def _mod_spec(mod, tiles_per_group):
    _, r, w = mod.shape
    return pl.BlockSpec((None, r, w), lambda i, *_: (i // tiles_per_group, 0, 0))


def _mod_rows(ref, rs):
    return ref[...] if ref.shape[0] == 1 else ref[rs, :]


def _weight_spec(k, tn, layer, col_block0, transposed):
    if transposed:
        return pl.BlockSpec((None, tn, k), lambda j, *_: (layer, j + col_block0, 0))
    return pl.BlockSpec((None, k, tn), lambda j, *_: (layer, 0, j + col_block0))


def _weight_dot(a, wb, transposed):
    if transposed:
        return lax.dot_general(a, wb, _NT, preferred_element_type=F32)
    return jnp.dot(a, wb, preferred_element_type=F32)


def _proj_kernel(a_ref, w_ref, o_ref, wb_scr, *, transposed):
    @pl.when(pl.program_id(1) == 0)
    def _cast():
        wb_scr[...] = w_ref[...].astype(BF16)

    o_ref[...] = _weight_dot(a_ref[...], wb_scr[...], transposed).astype(o_ref.dtype)


def _proj(a, w, layer, col_block0, n_out, tm, tn, transposed=False, out_dtype=F32):
    m, k = a.shape
    w_spec = _weight_spec(k, tn, layer, col_block0, transposed)
    return pl.pallas_call(
        functools.partial(_proj_kernel, transposed=transposed),
        grid=(n_out // tn, m // tm),
        in_specs=[pl.BlockSpec((tm, k), lambda j, i: (i, 0)), w_spec],
        out_specs=pl.BlockSpec((tm, tn), lambda j, i: (i, j)),
        out_shape=jax.ShapeDtypeStruct((m, n_out), out_dtype),
        scratch_shapes=[pltpu.VMEM(w_spec.block_shape[1:], BF16)],
        compiler_params=_params("arbitrary", "arbitrary"),
        name="proj",
    )(a, w)


def _rope_heads(y, cos, sin, w, scale):
    out = []
    for h in range(y.shape[1] // HEAD_DIM):
        x = _rms(y[:, h * HEAD_DIM:(h + 1) * HEAD_DIM], w)
        r = x * cos + pltpu.roll(x, HEAD_DIM // 2, 1) * sin
        out.append(r * scale if scale != 1.0 else r)
    return out


def _proj_rope_kernel(a_ref, w_ref, cos_ref, sin_ref, nw_ref, o_ref, wb_scr, *, scale, rows, transposed):
    @pl.when(pl.program_id(0) == 0)
    def _cast():
        wb_scr[...] = w_ref[...].astype(BF16)

    for r0 in range(0, a_ref.shape[0], rows):
        rs = slice(r0, r0 + rows)
        y = _weight_dot(a_ref[rs, :], wb_scr[...], transposed)
        for h, r in enumerate(_rope_heads(y, cos_ref[rs, :], sin_ref[rs, :], nw_ref[...], scale)):
            o_ref[rs, h * HEAD_DIM:(h + 1) * HEAD_DIM] = r.astype(o_ref.dtype)


def _proj_rope(a, w, layer, col_block, cos, sin, norm_w, tm, pos_tiles, scale, transposed, out_dtype):
    m, k = a.shape
    if transposed:
        w_spec = pl.BlockSpec((None, W_A, k), lambda i: (layer, col_block, 0))
    else:
        w_spec = pl.BlockSpec((None, k, W_A), lambda i: (layer, 0, col_block))
    return pl.pallas_call(
        functools.partial(_proj_rope_kernel, scale=scale, rows=min(tm, ROPE_SUB_ROWS), transposed=transposed),
        grid=(m // tm,),
        in_specs=[pl.BlockSpec((tm, k), lambda i: (i, 0)),
                  w_spec,
                  pl.BlockSpec((tm, HEAD_DIM), lambda i: (i % pos_tiles, 0)),
                  pl.BlockSpec((tm, HEAD_DIM), lambda i: (i % pos_tiles, 0)),
                  pl.BlockSpec((1, HEAD_DIM), lambda i: (0, 0))],
        out_specs=pl.BlockSpec((tm, W_A), lambda i: (i, 0)),
        out_shape=jax.ShapeDtypeStruct((m, W_A), out_dtype),
        scratch_shapes=[pltpu.VMEM(w_spec.block_shape[1:], BF16)],
        compiler_params=_params("arbitrary"),
        name="proj_rope",
    )(a, w, cos, sin, norm_w.reshape(1, HEAD_DIM))


def _norm_proj_rope_kernel(x_ref, sc_ref, sh_ref, nmw_ref, w_ref, cos_ref, sin_ref, nw_ref, o_ref, h_ref, wb_scr,
                           *, scale, rows, transposed):
    @pl.when(pl.program_id(0) == 0)
    def _cast():
        wb_scr[...] = w_ref[...].astype(BF16)

    for r0 in range(0, x_ref.shape[0], rows):
        rs = slice(r0, r0 + rows)
        y = _rms(x_ref[rs, :], nmw_ref[...])
        hb = (y * (1.0 + _mod_rows(sc_ref, rs)) + _mod_rows(sh_ref, rs)).astype(BF16)
        h_ref[rs, :] = hb
        y = _weight_dot(hb, wb_scr[...], transposed)
        for h, r in enumerate(_rope_heads(y, cos_ref[rs, :], sin_ref[rs, :], nw_ref[...], scale)):
            o_ref[rs, h * HEAD_DIM:(h + 1) * HEAD_DIM] = r.astype(o_ref.dtype)


def _norm_proj_rope(x, sc, sh, norm_w, tiles_per_group, w, layer, col_block, cos, sin, qk_w, tm, pos_tiles,
                    scale, out_dtype):
    m, k = x.shape
    w_spec = pl.BlockSpec((None, W_A, k), lambda i: (layer, col_block, 0))
    return pl.pallas_call(
        functools.partial(_norm_proj_rope_kernel, scale=scale, rows=min(tm, ROPE_SUB_ROWS), transposed=True),
        grid=(m // tm,),
        in_specs=[pl.BlockSpec((tm, k), lambda i: (i, 0)),
                  _mod_spec(sc, tiles_per_group), _mod_spec(sh, tiles_per_group),
                  pl.BlockSpec((1, k), lambda i: (0, 0)),
                  w_spec,
                  pl.BlockSpec((tm, HEAD_DIM), lambda i: (i % pos_tiles, 0)),
                  pl.BlockSpec((tm, HEAD_DIM), lambda i: (i % pos_tiles, 0)),
                  pl.BlockSpec((1, HEAD_DIM), lambda i: (0, 0))],
        out_specs=[pl.BlockSpec((tm, W_A), lambda i: (i, 0)), pl.BlockSpec((tm, k), lambda i: (i, 0))],
        out_shape=[jax.ShapeDtypeStruct((m, W_A), out_dtype), jax.ShapeDtypeStruct((m, k), BF16)],
        scratch_shapes=[pltpu.VMEM(w_spec.block_shape[1:], BF16)],
        compiler_params=_params("arbitrary"),
        name="norm_proj_rope",
    )(x, sc, sh, norm_w.reshape(1, k), w, cos, sin, qk_w.reshape(1, HEAD_DIM))


def _gate_weight_kernel(a_ref, b_ref, o_ref):
    o_ref[...] = jnp.dot(a_ref[...], b_ref[...], preferred_element_type=F32, precision=lax.Precision.HIGHEST)


def _gate_weight(w_ab, w_gate_up):
    d, r = w_ab.shape
    n = w_gate_up.shape[1]
    return pl.pallas_call(
        _gate_weight_kernel,
        grid=(1,),
        in_specs=[pl.BlockSpec((d, r), lambda i: (0, 0)), pl.BlockSpec((r, n), lambda i: (0, 0))],
        out_specs=pl.BlockSpec((d, n), lambda i: (0, 0)),
        out_shape=jax.ShapeDtypeStruct((d, n), F32),
        compiler_params=_params("arbitrary"),
        name="gate_weight",
    )(w_ab, w_gate_up)


def _rope_tables(pos):
    half = HEAD_DIM // 2
    inv_freq = ROPE_THETA ** (-jnp.arange(half, dtype=F32) / half)
    ang = pos.astype(F32)[:, None] * inv_freq[None, :]
    cos, sin = jnp.cos(ang), jnp.sin(ang)
    return jnp.concatenate([cos, cos], axis=-1), jnp.concatenate([-sin, sin], axis=-1)


def _topk_select(g, valid, axis):
    n = g.shape[axis]
    idx = lax.broadcasted_iota(jnp.int32, g.shape, axis)
    gm = jnp.where(valid, g, NEG_INF)
    rank = jnp.zeros(g.shape, F32)
    for c in range(n):
        gc = lax.slice_in_dim(gm, c, c + 1, axis=axis)
        beats = (gc > gm) | ((gc == gm) & (idx > c))
        rank = rank + beats.astype(F32)
    return valid & (rank < MOBA_TOPK)


def _moba_prompt_kernel(q_ref, k_ref, v_ref, wu_ref, wd_ref, o_ref, wub_ref, wdb_ref,
                        kb_scr, vt_scr, km_scr, sel_scr, s_scr, *, nb, hp, group):
    qb = pl.program_id(2)
    blk = MOBA_BLOCK
    heads = [slice(hh * HEAD_DIM, (hh + 1) * HEAD_DIM) for hh in range(hp)]

    @pl.when(qb == 0)
    def _prep():
        def per_block(j, carry):
            rows = pl.ds(pl.multiple_of(j * blk, blk), blk)
            for hh, cols in enumerate(heads):
                kj = k_ref[rows, cols]
                kb_scr[hh, j] = kj.astype(BF16)
                km_scr[hh, pl.ds(j, 1), :] = jnp.sum(kj, axis=0, keepdims=True) * (1.0 / blk)
                vt_scr[hh, j, 0:HEAD_DIM, :] = v_ref[rows, cols].T.astype(BF16)
                vt_scr[hh, j, HEAD_DIM:HEAD_DIM + ONES_ROWS, :] = jnp.ones((ONES_ROWS, blk), BF16)
            return carry

        lax.fori_loop(0, nb, per_block, 0)

    causal = (lax.broadcasted_iota(jnp.int32, (blk, blk), 0) <= lax.broadcasted_iota(jnp.int32, (blk, blk), 1))
    for hh, cols in enumerate(heads):
        gate = lax.dot_general(km_scr[hh].astype(BF16), q_ref[:, cols], _NT, preferred_element_type=F32)
        cand = lax.broadcasted_iota(jnp.int32, gate.shape, 0)
        sel_scr[hh] = (_topk_select(gate, cand < qb, 0) | (cand == qb)).astype(F32)

    def attend(n_keys):
        first_own = n_keys - group
        wub_ref[...] = wu_ref[...].astype(BF16)
        wdb_ref[...] = wd_ref[...].astype(BF16)
        for hh, cols in enumerate(heads):
            q = q_ref[:, cols]
            m = jnp.full((1, blk), NEG_INF, F32)
            for j in range(n_keys):
                s = lax.dot_general(kb_scr[hh, j], q, _NT, preferred_element_type=F32)
                s = jnp.where(sel_scr[hh, j:j + 1, :] > 0.0, s, NEG_INF)
                s_scr[hh, j] = s
                bmax = jnp.max(s, axis=0, keepdims=True)
                if j >= first_own:
                    bmax = jnp.where(qb == j, NEG_INF, bmax)
                m = jnp.maximum(m, bmax)
            s_own = jnp.where(causal, s_scr[hh, qb], NEG_INF)
            s_scr[hh, qb] = s_own
            m = jnp.maximum(m, jnp.max(s_own, axis=0, keepdims=True))
            acc = jnp.zeros((HEAD_DIM + ONES_ROWS, blk), F32)
            for j in range(n_keys):
                p = jnp.exp2(s_scr[hh, j] - m)
                acc = acc + jnp.dot(vt_scr[hh, j], p.astype(BF16), preferred_element_type=F32)
            l = acc[HEAD_DIM:HEAD_DIM + 1, :]
            o_ref[:, cols] = (acc[0:HEAD_DIM, :] * (1.0 / l)).T.astype(o_ref.dtype)

    for g in range(nb // group):
        pl.when(qb // group == g)(functools.partial(attend, (g + 1) * group))


def _moba_prompt(q, k, v, n_seq, seq, w_up, w_down, layer):
    nb = seq // MOBA_BLOCK
    hp = MOBA_HEADS_PER_STEP
    group = min(MOBA_QBLOCK_GROUP, nb)
    assert nb % group == 0
    w = hp * HEAD_DIM
    hgs = N_HEADS_A // hp
    steps = n_seq * hgs * nb
    _, d, ff = w_up.shape
    ru, rd = d // steps, ff // steps
    assert ru * steps == d and rd * steps == ff and ru % 16 == 0 and rd % 16 == 0
    step = lambda n, h, b: (n * hgs + h) * nb + b
    return pl.pallas_call(
        functools.partial(_moba_prompt_kernel, nb=nb, hp=hp, group=group),
        grid=(n_seq, hgs, nb),
        in_specs=[pl.BlockSpec((MOBA_BLOCK, w), lambda n, h, b: (n * nb + b, h)),
                  pl.BlockSpec((seq, w), lambda n, h, b: (n, h)),
                  pl.BlockSpec((seq, w), lambda n, h, b: (n, h)),
                  pl.BlockSpec((None, ru, ff), lambda n, h, b: (layer, step(n, h, b), 0)),
                  pl.BlockSpec((None, rd, d), lambda n, h, b: (layer, step(n, h, b), 0))],
        out_specs=[pl.BlockSpec((MOBA_BLOCK, w), lambda n, h, b: (n * nb + b, h)),
                   pl.BlockSpec((ru, ff), lambda n, h, b: (step(n, h, b), 0)),
                   pl.BlockSpec((rd, d), lambda n, h, b: (step(n, h, b), 0))],
        out_shape=[jax.ShapeDtypeStruct((n_seq * seq, W_A), BF16),
                   jax.ShapeDtypeStruct((d, ff), BF16),
                   jax.ShapeDtypeStruct((ff, d), BF16)],
        scratch_shapes=[pltpu.VMEM((hp, nb, MOBA_BLOCK, HEAD_DIM), BF16),
                        pltpu.VMEM((hp, nb, HEAD_DIM + ONES_ROWS, MOBA_BLOCK), BF16),
                        pltpu.VMEM((hp, nb, HEAD_DIM), F32),
                        pltpu.VMEM((hp, nb, MOBA_BLOCK), F32),
                        pltpu.VMEM((hp, nb, MOBA_BLOCK, MOBA_BLOCK), F32)],
        compiler_params=_params("arbitrary", "arbitrary", "arbitrary"),
        name="moba_prompt",
    )(q, k, v, w_up, w_down)


def _gla_levels():
    m, out = 1, []
    while m < GLA_CHUNK:
        out.append(m)
        m *= 2
    return out


def _gla_cum_matrices():
    c = GLA_CHUNK
    t = np.arange(c)
    tri = (t[None, :] <= t[:, None]).astype(np.float32)
    mats = [tri]
    for m in _gla_levels():
        mats.append(tri[(t // (2 * m)) * (2 * m) + m - 1])
    return np.concatenate(mats, axis=0)


def _split_bf16(x):
    hi = x.astype(BF16)
    return hi, (x - hi.astype(F32)).astype(BF16)


def _gla_prompt_kernel(g_ref, pre_ref, cm_ref, bg_ref, nw_ref, o_ref, s_ref):
    c = GLA_CHUNK

    @pl.when(pl.program_id(1) == 0)
    def _init():
        s_ref[...] = jnp.zeros(s_ref.shape, F32)

    tok = lax.broadcasted_iota(jnp.int32, (c, c), 0)
    col = lax.broadcasted_iota(jnp.int32, (c, c), 1)
    tok_d = lax.broadcasted_iota(jnp.int32, (c, DK_B), 0)
    la = _log_sigmoid(pre_ref[...] + bg_ref[...]) * (LOG2E / GATE_TEMP)
    cums_all = sum(jnp.dot(cm_ref[...], piece, preferred_element_type=F32) for piece in _split_bf16(la))
    for h in range(N_HEADS_B):
        dk = slice(h * DK_B, (h + 1) * DK_B)
        cums = cums_all[:, dk]
        b = cums[0:c]
        q = g_ref[:, dk] * (DK_B ** -0.5)
        k = g_ref[:, W_QB + h * DK_B:W_QB + (h + 1) * DK_B]
        v = g_ref[:, 2 * W_QB + h * DV_B:2 * W_QB + (h + 1) * DV_B]
        rb = g_ref[:, 2 * W_QB + W_VB + h * DV_B:2 * W_QB + W_VB + (h + 1) * DV_B]
        vb = v.astype(BF16)

        attn = lax.dot_general(q.astype(BF16), k.astype(BF16), _NT, preferred_element_type=F32)
        attn = jnp.where(col == tok, attn, 0.0)
        for lvl, m in enumerate(_gla_levels()):
            r = cums[(lvl + 1) * c:(lvl + 2) * c]
            upper = ((tok_d // m) % 2) == 1
            x = b - r
            e = jnp.exp2(jnp.where(upper, x, -x))
            e_up = jnp.where(upper, e, 0.0)
            a_m = lax.dot_general((q * e_up).astype(BF16), (k * (e - e_up)).astype(BF16), _NT,
                                  preferred_element_type=F32)
            attn = attn + jnp.where((tok // (2 * m)) == (col // (2 * m)), a_m, 0.0)

        state = s_ref[0, h]
        o = jnp.dot(attn.astype(BF16), vb, preferred_element_type=F32)
        o = o + jnp.dot((q * jnp.exp2(b)).astype(BF16), state.astype(BF16), preferred_element_type=F32)
        b_last = b[c - 1:c, :]
        kd = k * jnp.exp2(b_last - b)
        kv = jnp.dot(kd.T.astype(BF16), vb, preferred_element_type=F32)
        dcol = b.T[:, c - 1:c]
        s_ref[0, h] = jnp.exp2(dcol) * state + kv

        y = _rms(o, nw_ref[...]) * _silu(rb)
        o_ref[:, h * DV_B:(h + 1) * DV_B] = y.astype(o_ref.dtype)


def _gla_prompt(g, pre, bg, nw, n_seq, seq):
    nc = seq // GLA_CHUNK
    cm = jnp.asarray(_gla_cum_matrices(), BF16)
    return pl.pallas_call(
        _gla_prompt_kernel,
        grid=(n_seq, nc),
        in_specs=[pl.BlockSpec((GLA_CHUNK, W_GLA), lambda n, t: (n * nc + t, 0)),
                  pl.BlockSpec((GLA_CHUNK, W_QB), lambda n, t: (n * nc + t, 0)),
                  pl.BlockSpec(cm.shape, lambda n, t: (0, 0)),
                  pl.BlockSpec((1, W_QB), lambda n, t: (0, 0)),
                  pl.BlockSpec((1, DV_B), lambda n, t: (0, 0))],
        out_specs=[pl.BlockSpec((GLA_CHUNK, W_VB), lambda n, t: (n * nc + t, 0)),
                   pl.BlockSpec((1, N_HEADS_B, DK_B, DV_B), lambda n, t: (n, 0, 0, 0))],
        out_shape=[jax.ShapeDtypeStruct((n_seq * seq, W_VB), BF16),
                   jax.ShapeDtypeStruct((n_seq, N_HEADS_B, DK_B, DV_B), F32)],
        compiler_params=_params("arbitrary", "arbitrary"),
        name="gla_prompt",
    )(g, pre, cm, bg.reshape(1, W_QB), nw.reshape(1, DV_B))


def _gla_step_kernel(qc_ref, kc_ref, prec_ref, v_ref, rb_ref, bgc_ref, nw_ref, s0_ref, o_ref, s_ref):
    for i in range(s0_ref.shape[0]):
        for h in range(N_HEADS_B):
            rows = slice(h * DK_B, (h + 1) * DK_B)
            a = jnp.exp(_log_sigmoid(prec_ref[i, rows, :] + bgc_ref[rows, :]) * (1.0 / GATE_TEMP))
            v = v_ref[i, :, h * DV_B:(h + 1) * DV_B]
            s_new = a * s0_ref[i, h] + kc_ref[i, rows, :] * v
            s_ref[i, h] = s_new
            o = jnp.sum((qc_ref[i, rows, :] * (DK_B ** -0.5)) * s_new, axis=0, keepdims=True)
            y = _rms(o, nw_ref[...]) * _silu(rb_ref[i, :, h * DV_B:(h + 1) * DV_B])
            o_ref[i, :, h * DV_B:(h + 1) * DV_B] = y.astype(o_ref.dtype)


def _gla_step(g, pre, bg, nw, s0):
    n = g.shape[0]
    sb = GLA_STEP_SAMPLES if n % GLA_STEP_SAMPLES == 0 else 1
    qc = g[:, 0:W_QB].reshape(n, W_QB, 1)
    kc = g[:, W_QB:2 * W_QB].reshape(n, W_QB, 1)
    v = g[:, 2 * W_QB:2 * W_QB + W_VB].reshape(n, 1, W_VB)
    rb = g[:, 2 * W_QB + W_VB:W_GLA].reshape(n, 1, W_VB)
    per_n3 = lambda i: (i, 0, 0)
    fixed = lambda i: (0, 0)
    o, s = pl.pallas_call(
        _gla_step_kernel,
        grid=(n // sb,),
        in_specs=[pl.BlockSpec((sb, W_QB, 1), per_n3), pl.BlockSpec((sb, W_QB, 1), per_n3),
                  pl.BlockSpec((sb, W_QB, 1), per_n3),
                  pl.BlockSpec((sb, 1, W_VB), per_n3), pl.BlockSpec((sb, 1, W_VB), per_n3),
                  pl.BlockSpec((W_QB, 1), fixed), pl.BlockSpec((1, DV_B), fixed),
                  pl.BlockSpec((sb, N_HEADS_B, DK_B, DV_B), lambda i: (i, 0, 0, 0))],
        out_specs=[pl.BlockSpec((sb, 1, W_VB), per_n3),
                   pl.BlockSpec((sb, N_HEADS_B, DK_B, DV_B), lambda i: (i, 0, 0, 0))],
        out_shape=[jax.ShapeDtypeStruct((n, 1, W_VB), BF16),
                   jax.ShapeDtypeStruct(s0.shape, F32)],
        compiler_params=_params("arbitrary"),
        name="gla_step",
    )(qc, kc, pre.reshape(n, W_QB, 1), v, rb, bg.reshape(W_QB, 1), nw.reshape(1, DV_B), s0)
    return o.reshape(n, W_VB), s


PAGES_PER_STEP = 16


def _page_scores(q_ref, k_refs, o_ref):
    q = q_ref[0].astype(F32)
    shape3 = (PAGE_SIZE, N_HEADS_A, LANES)
    diag = lax.broadcasted_iota(jnp.int32, shape3, 0) == lax.broadcasted_iota(jnp.int32, shape3, 2)
    for p, k_ref in enumerate(k_refs):
        r = jnp.sum(k_ref[0] * q[None], axis=-1, keepdims=True)
        o_ref[0, :, p * PAGE_SIZE:(p + 1) * PAGE_SIZE] = jnp.sum(jnp.where(diag, r, 0.0), axis=0)


def _moba_pick_kernel(s_ref, o_ref, *, nb):
    rows = s_ref.shape[0]
    lane = lax.broadcasted_iota(jnp.int32, (rows, LANES), 1)
    gate = jnp.zeros((rows, LANES), F32)
    for j in range(nb):
        gj = jnp.sum(s_ref[:, j * MOBA_BLOCK:(j + 1) * MOBA_BLOCK], axis=-1, keepdims=True) * (1.0 / MOBA_BLOCK)
        gate = jnp.where(lane == j, gj, gate)
    valid = lane < nb
    gm = jnp.where(valid, gate, NEG_INF)
    rank = jnp.zeros((rows, LANES), F32)
    for c in range(nb):
        gc = gm[:, c:c + 1]
        beats = (gc > gm) | ((gc == gm) & (lane > c))
        rank = rank + beats.astype(F32)
    out = jnp.zeros((rows, LANES), jnp.int32)
    for t in range(MOBA_TOPK):
        pick = jnp.sum(jnp.where(valid & (rank == float(t)), lane, 0), axis=-1, keepdims=True)
        out = jnp.where(lane == t, pick, out)
    o_ref[...] = out


def _moba_pick(scores, nb):
    rows = scores.shape[0]
    return pl.pallas_call(
        functools.partial(_moba_pick_kernel, nb=nb),
        grid=(1,),
        in_specs=[pl.BlockSpec(scores.shape, lambda i: (0, 0))],
        out_specs=pl.BlockSpec((rows, LANES), lambda i: (0, 0)),
        out_shape=jax.ShapeDtypeStruct((rows, LANES), jnp.int32),
        compiler_params=_params("arbitrary"),
        name="moba_pick",
    )(scores)


def _moba_gather_kernel(pt_ref, pick_ref, q_ref, kn_ref, vn_ref, s_ref, *refs):
    per_head = 2 * MOBA_TOPK
    hg = GATHER_HEADS_PER_STEP
    v_refs, o_ref = refs[:hg * per_head], refs[hg * per_head]
    for hh in range(hg):
        row = pl.program_id(0) * N_HEADS_A + pl.program_id(1) * hg + hh
        s_self = jnp.sum(q_ref[0, hh].astype(F32) * kn_ref[0, hh].astype(BF16).astype(F32), axis=-1,
                         keepdims=True)
        s = [s_ref[0, hh, pl.ds(pick_ref[row, t], 1), :] for t in range(MOBA_TOPK)]
        m = s_self
        for st in s:
            m = jnp.maximum(m, jnp.max(st, axis=-1, keepdims=True))
        p_self = jnp.exp(s_self - m)
        l = p_self
        acc = jnp.zeros((8, HEAD_DIM), F32)
        for t, st in enumerate(s):
            p = jnp.exp(st - m)
            l = l + jnp.sum(p, axis=-1, keepdims=True)
            pb = jnp.broadcast_to(p, (8, MOBA_BLOCK)).astype(BF16)
            for half in range(2):
                vpage = v_refs[hh * per_head + 2 * t + half].reshape(PAGE_SIZE, HEAD_DIM)[...].astype(BF16)
                acc = acc + jnp.dot(pb[:, half * PAGE_SIZE:(half + 1) * PAGE_SIZE], vpage,
                                    preferred_element_type=F32)
        out = acc[0:1] + p_self * vn_ref[0, hh].astype(BF16).astype(F32)
        o_ref[0, hh] = (out * (1.0 / l)).astype(o_ref.dtype)


def _moba_gather(cache_v, page_table, picks, scores, q, k_new, v_new, nb):
    n = page_table.shape[0]
    pages_per_block = MOBA_BLOCK // PAGE_SIZE
    assert pages_per_block == 2
    scores4 = scores.reshape(n, N_HEADS_A, nb, MOBA_BLOCK)
    cache_v5 = cache_v.reshape(cache_v.shape[0], PAGE_SIZE, N_HEADS_A, 1, HEAD_DIM)

    hg = GATHER_HEADS_PER_STEP

    def v_spec(hh, t, half):
        def index(i, g, pt, pk):
            h = g * hg + hh
            return (pt[i, pk[i * N_HEADS_A + h, t] * pages_per_block + half], 0, h, 0, 0)
        return pl.BlockSpec((1, PAGE_SIZE, 1, 1, HEAD_DIM), index)

    head_group = lambda i, g, pt, pk: (i, g, 0, 0)
    v_specs = [v_spec(hh, t, half) for hh in range(hg) for t in range(MOBA_TOPK) for half in range(2)]
    return pl.pallas_call(
        _moba_gather_kernel,
        grid_spec=pltpu.PrefetchScalarGridSpec(
            num_scalar_prefetch=2,
            grid=(n, N_HEADS_A // hg),
            in_specs=[pl.BlockSpec((1, hg, 1, HEAD_DIM), head_group)] * 3
                     + [pl.BlockSpec((1, hg, nb, MOBA_BLOCK), head_group)]
                     + v_specs,
            out_specs=pl.BlockSpec((1, hg, 1, HEAD_DIM), head_group),
        ),
        out_shape=jax.ShapeDtypeStruct((n, N_HEADS_A, 1, HEAD_DIM), BF16),
        compiler_params=_params("arbitrary", "arbitrary"),
        name="moba_gather",
    )(page_table, picks,
      q.reshape(n, N_HEADS_A, 1, HEAD_DIM), k_new.reshape(n, N_HEADS_A, 1, HEAD_DIM),
      v_new.reshape(n, N_HEADS_A, 1, HEAD_DIM), scores4,
      *([cache_v5] * len(v_specs)))


def _out_proj_kernel(oa_ref, ob_ref, w_ref, x_ref, g_ref, o_ref, wb_scr):
    @pl.when(pl.program_id(1) == 0)
    def _cast():
        wb_scr[...] = w_ref[...].astype(BF16)

    mix = jnp.dot(oa_ref[...], wb_scr[0:W_A, :], preferred_element_type=F32)
    mix = mix + jnp.dot(ob_ref[...], wb_scr[W_A:W_A + W_VB, :], preferred_element_type=F32)
    o_ref[...] = x_ref[...] + g_ref[...] * mix


def _out_proj(oa, ob, w, layer, x, gate, tm, tn, tiles_per_group):
    m, d = x.shape
    _, r, _ = gate.shape
    return pl.pallas_call(
        _out_proj_kernel,
        grid=(d // tn, m // tm),
        in_specs=[pl.BlockSpec((tm, W_A), lambda j, i: (i, 0)),
                  pl.BlockSpec((tm, W_VB), lambda j, i: (i, 0)),
                  pl.BlockSpec((None, W_A + W_VB, tn), lambda j, i: (layer, 0, j)),
                  pl.BlockSpec((tm, tn), lambda j, i: (i, j)),
                  pl.BlockSpec((None, r, tn), lambda j, i: (i // tiles_per_group, 0, j))],
        out_specs=pl.BlockSpec((tm, tn), lambda j, i: (i, j)),
        out_shape=jax.ShapeDtypeStruct((m, d), F32),
        scratch_shapes=[pltpu.VMEM((W_A + W_VB, tn), BF16)],
        compiler_params=_params("arbitrary", "arbitrary"),
        name="out_proj",
    )(oa, ob, w, x, gate)


def _mlp_scores_kernel(pt_ref, x_ref, sc_ref, sh_ref, g_ref, nw_ref, wu_ref, wd_ref, q_ref, *refs):
    k_refs = refs[:PAGES_PER_STEP]
    o_ref, s_ref, h_scr = refs[PAGES_PER_STEP:]
    _mlp_kernel(x_ref, sc_ref, sh_ref, g_ref, nw_ref, wu_ref, wd_ref, o_ref, h_scr,
                beside_matmuls=functools.partial(_page_scores, q_ref, k_refs, s_ref))


def _mlp_kernel(x_ref, sc_ref, sh_ref, g_ref, nw_ref, wu_ref, wd_ref, o_ref, h_scr, beside_matmuls=None):
    f = pl.program_id(1)
    d = o_ref.shape[1]
    tn = min(d, TN_DENSE)

    @pl.when(f == 0)
    def _first():
        y = _rms(x_ref[...], nw_ref[...])
        h_scr[...] = (y * (1.0 + sc_ref[...]) + sh_ref[...]).astype(BF16)
        o_ref[...] = jnp.zeros(o_ref.shape, F32)

    if beside_matmuls is not None:
        beside_matmuls()
    u = jnp.dot(h_scr[...], wu_ref[...], preferred_element_type=F32)
    r = jnp.maximum(u, 0.0)
    r2 = (r * r).astype(BF16)
    for c0 in range(0, d, tn):
        o_ref[:, c0:c0 + tn] += jnp.dot(r2, wd_ref[:, c0:c0 + tn], preferred_element_type=F32)

    @pl.when(f == pl.num_programs(1) - 1)
    def _last():
        o_ref[...] = x_ref[...] + g_ref[...] * o_ref[...]


def _mlp(x, sc, sh, gate, nw, wu, wd, tm, tf, tiles_per_group):
    m, d = x.shape
    ff = wu.shape[1]
    return pl.pallas_call(
        _mlp_kernel,
        grid=(m // tm, ff // tf),
        in_specs=[pl.BlockSpec((tm, d), lambda i, f: (i, 0)),
                  _mod_spec(sc, tiles_per_group), _mod_spec(sh, tiles_per_group), _mod_spec(gate, tiles_per_group),
                  pl.BlockSpec((1, d), lambda i, f: (0, 0)),
                  pl.BlockSpec((d, tf), lambda i, f: (0, f)),
                  pl.BlockSpec((tf, d), lambda i, f: (f, 0))],
        out_specs=pl.BlockSpec((tm, d), lambda i, f: (i, 0)),
        out_shape=jax.ShapeDtypeStruct((m, d), F32),
        scratch_shapes=[pltpu.VMEM((tm, d), BF16)],
        compiler_params=_params("arbitrary", "arbitrary"),
        name="mlp",
    )(x, sc, sh, gate, nw.reshape(1, d), wu, wd)


def _mlp_with_scores(x, sc, sh, gate, nw, wu, wd, tm, tf, tiles_per_group, cache_k, page_table, q):
    m, d = x.shape
    ff = wu.shape[1]
    n, n_pages = page_table.shape
    nf = ff // tf
    groups = n_pages // PAGES_PER_STEP
    assert (m // tm) * nf == n * groups and n_pages % PAGES_PER_STEP == 0

    def sample(i, f):
        return (i * nf + f) // groups

    def group(i, f):
        return (i * nf + f) % groups

    def page_spec(p):
        return pl.BlockSpec((1, PAGE_SIZE, N_HEADS_A, HEAD_DIM),
                            lambda i, f, pt: (pt[sample(i, f), group(i, f) * PAGES_PER_STEP + p], 0, 0, 0))

    def mod_spec(mod):
        _, r, w = mod.shape
        return pl.BlockSpec((None, r, w), lambda i, f, pt: (i // tiles_per_group, 0, 0))

    y, scores = pl.pallas_call(
        _mlp_scores_kernel,
        grid_spec=pltpu.PrefetchScalarGridSpec(
            num_scalar_prefetch=1,
            grid=(m // tm, nf),
            in_specs=[pl.BlockSpec((tm, d), lambda i, f, pt: (i, 0), pipeline_mode=pl.Buffered(1)),
                      mod_spec(sc), mod_spec(sh), mod_spec(gate),
                      pl.BlockSpec((1, d), lambda i, f, pt: (0, 0)),
                      pl.BlockSpec((d, tf), lambda i, f, pt: (0, f)),
                      pl.BlockSpec((tf, d), lambda i, f, pt: (f, 0)),
                      pl.BlockSpec((1, N_HEADS_A, HEAD_DIM), lambda i, f, pt: (sample(i, f), 0, 0))]
                     + [page_spec(p) for p in range(PAGES_PER_STEP)],
            out_specs=[pl.BlockSpec((tm, d), lambda i, f, pt: (i, 0), pipeline_mode=pl.Buffered(1)),
                       pl.BlockSpec((1, N_HEADS_A, PAGES_PER_STEP * PAGE_SIZE),
                                    lambda i, f, pt: (sample(i, f), 0, group(i, f)))],
            scratch_shapes=[pltpu.VMEM((tm, d), BF16)],
        ),
        out_shape=[jax.ShapeDtypeStruct((m, d), F32),
                   jax.ShapeDtypeStruct((n, N_HEADS_A, n_pages * PAGE_SIZE), F32)],
        compiler_params=_params("arbitrary", "arbitrary"),
        name="mlp_scores",
    )(page_table, x, sc, sh, gate, nw.reshape(1, d), wu, wd, q.reshape(n, N_HEADS_A, HEAD_DIM),
      *([cache_k] * PAGES_PER_STEP))
    return y, scores


def _dense_in(x2, mods, norm_w, w_in_t, layer, w_gate, rope, qk_norm_w, q_scale, tm, tiles_per_group):
    tn = TN_DENSE
    cos, sin = rope
    pos_tiles = cos.shape[0] // tm
    q, h = _norm_proj_rope(x2, mods["sc1"], mods["sh1"], norm_w, tiles_per_group, w_in_t, layer, 0, cos, sin,
                           qk_norm_w[0], tm, pos_tiles, q_scale, BF16)
    k = _proj_rope(h, w_in_t, layer, 1, cos, sin, qk_norm_w[1], tm, pos_tiles, 1.0, True, F32)
    v = _proj(h, w_in_t, layer, 2 * W_A // tn, W_A, tm, tn, transposed=True)
    g = _proj(h, w_in_t, layer, 3 * W_A // tn, W_GLA, tm, tn, transposed=True)
    pre = _proj(h, w_gate, 0, 0, W_QB, tm, W_QB)
    return q, k, v, g, pre


def kernel(x_prompt, x_sample, c_prompt, c_sample, cache_k, cache_v, state_gla, page_table, w_ada, b_ada, norm_mix_w, w_in, q_norm_w, k_norm_w, w_gate_up, b_gate, gla_norm_w, w_out, norm_ffn_w, w_up, w_down):
    n_p, s_p, d = x_prompt.shape
    n_s, s_s, _ = x_sample.shape
    depth = w_ada.shape[0]
    assert depth == 1 and s_s == 1 and d == D_MODEL
    n_pages = page_table.shape[1]
    past_len = n_pages * PAGE_SIZE
    nb_past = past_len // MOBA_BLOCK
    assert past_len % MOBA_BLOCK == 0 and nb_past >= MOBA_TOPK
    l = 0
    tm = min(TM_DENSE, s_p)

    rows = n_p + n_s
    rows_pad = -(-rows // 8) * 8
    c_all = jnp.concatenate([c_prompt, c_sample, jnp.zeros((rows_pad - rows, d), F32)], axis=0)
    ada = _ada(c_all, w_ada[l], b_ada[l])
    names = ("sh1", "sc1", "g1", "sh2", "sc2", "g2")
    mods_p = {nm: ada[0:n_p, i * d:(i + 1) * d].reshape(n_p, 1, d) for i, nm in enumerate(names)}
    mods_s = {nm: ada[n_p:rows, i * d:(i + 1) * d].reshape(1, n_s, d) for i, nm in enumerate(names)}

    w_in_t = jnp.swapaxes(w_in, 1, 2)
    w_gate = _gate_weight(w_in[l, :, 3 * W_A + W_GLA:], w_gate_up[l])[None]
    scale = HEAD_DIM ** -0.5
    qk_norm_w = (q_norm_w[l], k_norm_w[l])

    xp = x_prompt.reshape(n_p * s_p, d)
    rope_p = _rope_tables(jnp.arange(s_p, dtype=jnp.int32))
    q_p, k_p, v_p, g, pre = _dense_in(xp, mods_p, norm_mix_w[l], w_in_t, l, w_gate, rope_p, qk_norm_w,
                                      scale * LOG2E, tm, s_p // tm)
    oa_p, w_up_b, w_down_b = _moba_prompt(q_p, k_p, v_p, n_p, s_p, w_up, w_down, l)
    ob_p, gla_p = _gla_prompt(g, pre, b_gate[l], gla_norm_w[l], n_p, s_p)
    x1_p = _out_proj(oa_p, ob_p, w_out, l, xp, mods_p["g1"], tm, TN_DENSE, s_p // tm)

    xs = x_sample.reshape(n_s, d)
    rope_s = _rope_tables(jnp.full((n_s,), past_len, jnp.int32))
    q_s, k_s, v_s, g, pre = _dense_in(xs, mods_s, norm_mix_w[l], w_in_t, l, w_gate, rope_s, qk_norm_w,
                                      scale, n_s, 1)
    tm_mlp = min(TM_MLP, s_p)
    y_p, scores = _mlp_with_scores(x1_p, mods_p["sc2"], mods_p["sh2"], mods_p["g2"], norm_ffn_w[l],
                                   w_up_b, w_down_b, tm_mlp, TF_MLP_PROMPT, s_p // tm_mlp, cache_k[l], page_table, q_s)
    picks = _moba_pick(scores.reshape(n_s * N_HEADS_A, past_len), nb_past)
    oa_s = _moba_gather(cache_v[l], page_table, picks, scores, q_s, k_s, v_s, nb_past).reshape(n_s, W_A)
    ob_s, gla_s = _gla_step(g, pre, b_gate[l], gla_norm_w[l], state_gla[l])
    x1_s = _out_proj(oa_s, ob_s, w_out, l, xs, mods_s["g1"], n_s, TN_DENSE, 1)
    y_s = _mlp(x1_s, mods_s["sc2"], mods_s["sh2"], mods_s["g2"], norm_ffn_w[l], w_up_b, w_down_b,
               n_s, TF_MLP, 1)

    return (y_p.reshape(n_p, s_p, d), y_s.reshape(n_s, s_s, d),
            k_p.reshape(1, n_p, s_p, N_HEADS_A, HEAD_DIM), v_p.reshape(1, n_p, s_p, N_HEADS_A, HEAD_DIM),
            gla_p[None],
            k_s.reshape(1, n_s, s_s, N_HEADS_A, HEAD_DIM), v_s.reshape(1, n_s, s_s, N_HEADS_A, HEAD_DIM),
            gla_s[None])
```

```python
import functools

import numpy as np
import jax
import jax.numpy as jnp
from jax import lax
from jax.experimental import pallas as pl
from jax.experimental.pallas import tpu as pltpu

D_MODEL = 2048
PAGE_SIZE = 128
HEAD_DIM = 128
N_HEADS_A = 8
N_HEADS_B = 4
DK_B = 128
DV_B = 256
GATE_RANK = 16
GATE_TEMP = 16.0
MOBA_BLOCK = 256
MOBA_TOPK = 3
GLA_CHUNK = 128
D_FF = 4 * D_MODEL
ROPE_THETA = 10000.0
EPS = 1e-6

W_A = N_HEADS_A * HEAD_DIM
W_QB = N_HEADS_B * DK_B
W_VB = N_HEADS_B * DV_B
W_GLA = 2 * W_QB + 2 * W_VB
LANES = 128

F32 = jnp.float32
BF16 = jnp.bfloat16
NEG_INF = float("-inf")
VMEM_LIMIT = 56 * 1024 * 1024

TM_DENSE = 1024
TN_DENSE = 1024
TM_MLP = 1024
TF_MLP_PROMPT = 512
TF_MLP = 512
GATHER_HEADS_PER_STEP = 4
GLA_STEP_SAMPLES = 4
ROPE_SUB_ROWS = 256
MOBA_HEADS_PER_STEP = 2
MOBA_QBLOCK_GROUP = 1
ONES_ROWS = 16
LOG2E = 1.4426950408889634

_NT = (((1,), (1,)), ((), ()))


def _params(*sem):
    return pltpu.CompilerParams(dimension_semantics=sem, vmem_limit_bytes=VMEM_LIMIT)


def _rms(x, w):
    return x * lax.rsqrt(jnp.mean(x * x, axis=-1, keepdims=True) + EPS) * w


def _log_sigmoid(x):
    return jnp.minimum(x, 0.0) - jnp.log(1.0 + jnp.exp(-jnp.abs(x)))


def _silu(x):
    return x / (1.0 + jnp.exp(-x))


def _ada_kernel(c_ref, w_ref, b_ref, o_ref):
    a = _silu(c_ref[...]).astype(BF16)
    o_ref[...] = jnp.dot(a, w_ref[...].astype(BF16), preferred_element_type=F32) + b_ref[...]


def _ada(c, w, b, tn=TN_DENSE):
    m, d = c.shape
    n = w.shape[1]
    return pl.pallas_call(
        _ada_kernel,
        grid=(n // tn,),
        in_specs=[pl.BlockSpec((m, d), lambda j: (0, 0)),
                  pl.BlockSpec((d, tn), lambda j: (0, j)),
                  pl.BlockSpec((1, tn), lambda j: (0, j))],
        out_specs=pl.BlockSpec((m, tn), lambda j: (0, j)),
        out_shape=jax.ShapeDtypeStruct((m, n), F32),
        compiler_params=_params("arbitrary"),
        name="ada_proj",
    )(c, w, b.reshape(1, n))


def _mod_spec(mod, tiles_per_group):
    _, r, w = mod.shape
    return pl.BlockSpec((None, r, w), lambda i, *_: (i // tiles_per_group, 0, 0))


def _mod_rows(ref, rs):
    return ref[...] if ref.shape[0] == 1 else ref[rs, :]


def _weight_spec(k, tn, layer, col_block0, transposed):
    if transposed:
        return pl.BlockSpec((None, tn, k), lambda j, *_: (layer, j + col_block0, 0))
    return pl.BlockSpec((None, k, tn), lambda j, *_: (layer, 0, j + col_block0))


def _weight_dot(a, wb, transposed):
    if transposed:
        return lax.dot_general(a, wb, _NT, preferred_element_type=F32)
    return jnp.dot(a, wb, preferred_element_type=F32)


def _proj_kernel(a_ref, w_ref, o_ref, wb_scr, *, transposed):
    @pl.when(pl.program_id(1) == 0)
    def _cast():
        wb_scr[...] = w_ref[...].astype(BF16)

    o_ref[...] = _weight_dot(a_ref[...], wb_scr[...], transposed).astype(o_ref.dtype)


def _proj(a, w, layer, col_block0, n_out, tm, tn, transposed=False, out_dtype=F32):
    m, k = a.shape
    w_spec = _weight_spec(k, tn, layer, col_block0, transposed)
    return pl.pallas_call(
        functools.partial(_proj_kernel, transposed=transposed),
        grid=(n_out // tn, m // tm),
        in_specs=[pl.BlockSpec((tm, k), lambda j, i: (i, 0)), w_spec],
        out_specs=pl.BlockSpec((tm, tn), lambda j, i: (i, j)),
        out_shape=jax.ShapeDtypeStruct((m, n_out), out_dtype),
        scratch_shapes=[pltpu.VMEM(w_spec.block_shape[1:], BF16)],
        compiler_params=_params("arbitrary", "arbitrary"),
        name="proj",
    )(a, w)


def _rope_heads(y, cos, sin, w, scale):
    out = []
    for h in range(y.shape[1] // HEAD_DIM):
        x = _rms(y[:, h * HEAD_DIM:(h + 1) * HEAD_DIM], w)
        r = x * cos + pltpu.roll(x, HEAD_DIM // 2, 1) * sin
        out.append(r * scale if scale != 1.0 else r)
    return out


def _proj_rope_kernel(a_ref, w_ref, cos_ref, sin_ref, nw_ref, o_ref, wb_scr, *, scale, rows, transposed):
    @pl.when(pl.program_id(0) == 0)
    def _cast():
        wb_scr[...] = w_ref[...].astype(BF16)

    for r0 in range(0, a_ref.shape[0], rows):
        rs = slice(r0, r0 + rows)
        y = _weight_dot(a_ref[rs, :], wb_scr[...], transposed)
        for h, r in enumerate(_rope_heads(y, cos_ref[rs, :], sin_ref[rs, :], nw_ref[...], scale)):
            o_ref[rs, h * HEAD_DIM:(h + 1) * HEAD_DIM] = r.astype(o_ref.dtype)


def _proj_rope(a, w, layer, col_block, cos, sin, norm_w, tm, pos_tiles, scale, transposed, out_dtype):
    m, k = a.shape
    if transposed:
        w_spec = pl.BlockSpec((None, W_A, k), lambda i: (layer, col_block, 0))
    else:
        w_spec = pl.BlockSpec((None, k, W_A), lambda i: (layer, 0, col_block))
    return pl.pallas_call(
        functools.partial(_proj_rope_kernel, scale=scale, rows=min(tm, ROPE_SUB_ROWS), transposed=transposed),
        grid=(m // tm,),
        in_specs=[pl.BlockSpec((tm, k), lambda i: (i, 0)),
                  w_spec,
                  pl.BlockSpec((tm, HEAD_DIM), lambda i: (i % pos_tiles, 0)),
                  pl.BlockSpec((tm, HEAD_DIM), lambda i: (i % pos_tiles, 0)),
                  pl.BlockSpec((1, HEAD_DIM), lambda i: (0, 0))],
        out_specs=pl.BlockSpec((tm, W_A), lambda i: (i, 0)),
        out_shape=jax.ShapeDtypeStruct((m, W_A), out_dtype),
        scratch_shapes=[pltpu.VMEM(w_spec.block_shape[1:], BF16)],
        compiler_params=_params("arbitrary"),
        name="proj_rope",
    )(a, w, cos, sin, norm_w.reshape(1, HEAD_DIM))


def _norm_proj_rope_kernel(x_ref, sc_ref, sh_ref, nmw_ref, w_ref, cos_ref, sin_ref, nw_ref, o_ref, h_ref, wb_scr,
                           *, scale, rows, transposed):
    @pl.when(pl.program_id(0) == 0)
    def _cast():
        wb_scr[...] = w_ref[...].astype(BF16)

    for r0 in range(0, x_ref.shape[0], rows):
        rs = slice(r0, r0 + rows)
        y = _rms(x_ref[rs, :], nmw_ref[...])
        hb = (y * (1.0 + _mod_rows(sc_ref, rs)) + _mod_rows(sh_ref, rs)).astype(BF16)
        h_ref[rs, :] = hb
        y = _weight_dot(hb, wb_scr[...], transposed)
        for h, r in enumerate(_rope_heads(y, cos_ref[rs, :], sin_ref[rs, :], nw_ref[...], scale)):
            o_ref[rs, h * HEAD_DIM:(h + 1) * HEAD_DIM] = r.astype(o_ref.dtype)


def _norm_proj_rope(x, sc, sh, norm_w, tiles_per_group, w, layer, col_block, cos, sin, qk_w, tm, pos_tiles,
                    scale, out_dtype):
    m, k = x.shape
    w_spec = pl.BlockSpec((None, W_A, k), lambda i: (layer, col_block, 0))
    return pl.pallas_call(
        functools.partial(_norm_proj_rope_kernel, scale=scale, rows=min(tm, ROPE_SUB_ROWS), transposed=True),
        grid=(m // tm,),
        in_specs=[pl.BlockSpec((tm, k), lambda i: (i, 0)),
                  _mod_spec(sc, tiles_per_group), _mod_spec(sh, tiles_per_group),
                  pl.BlockSpec((1, k), lambda i: (0, 0)),
                  w_spec,
                  pl.BlockSpec((tm, HEAD_DIM), lambda i: (i % pos_tiles, 0)),
                  pl.BlockSpec((tm, HEAD_DIM), lambda i: (i % pos_tiles, 0)),
                  pl.BlockSpec((1, HEAD_DIM), lambda i: (0, 0))],
        out_specs=[pl.BlockSpec((tm, W_A), lambda i: (i, 0)), pl.BlockSpec((tm, k), lambda i: (i, 0))],
        out_shape=[jax.ShapeDtypeStruct((m, W_A), out_dtype), jax.ShapeDtypeStruct((m, k), BF16)],
        scratch_shapes=[pltpu.VMEM(w_spec.block_shape[1:], BF16)],
        compiler_params=_params("arbitrary"),
        name="norm_proj_rope",
    )(x, sc, sh, norm_w.reshape(1, k), w, cos, sin, qk_w.reshape(1, HEAD_DIM))


def _gate_weight_kernel(a_ref, b_ref, o_ref):
    o_ref[...] = jnp.dot(a_ref[...], b_ref[...], preferred_element_type=F32, precision=lax.Precision.HIGHEST)


def _gate_weight(w_ab, w_gate_up):
    d, r = w_ab.shape
    n = w_gate_up.shape[1]
    return pl.pallas_call(
        _gate_weight_kernel,
        grid=(1,),
        in_specs=[pl.BlockSpec((d, r), lambda i: (0, 0)), pl.BlockSpec((r, n), lambda i: (0, 0))],
        out_specs=pl.BlockSpec((d, n), lambda i: (0, 0)),
        out_shape=jax.ShapeDtypeStruct((d, n), F32),
        compiler_params=_params("arbitrary"),
        name="gate_weight",
    )(w_ab, w_gate_up)


def _rope_tables(pos):
    half = HEAD_DIM // 2
    inv_freq = ROPE_THETA ** (-jnp.arange(half, dtype=F32) / half)
    ang = pos.astype(F32)[:, None] * inv_freq[None, :]
    cos, sin = jnp.cos(ang), jnp.sin(ang)
    return jnp.concatenate([cos, cos], axis=-1), jnp.concatenate([-sin, sin], axis=-1)


def _topk_select(g, valid, axis):
    n = g.shape[axis]
    idx = lax.broadcasted_iota(jnp.int32, g.shape, axis)
    gm = jnp.where(valid, g, NEG_INF)
    rank = jnp.zeros(g.shape, F32)
    for c in range(n):
        gc = lax.slice_in_dim(gm, c, c + 1, axis=axis)
        beats = (gc > gm) | ((gc == gm) & (idx > c))
        rank = rank + beats.astype(F32)
    return valid & (rank < MOBA_TOPK)


def _moba_prompt_kernel(q_ref, k_ref, v_ref, wu_ref, wd_ref, o_ref, wub_ref, wdb_ref,
                        kb_scr, vt_scr, km_scr, sel_scr, s_scr, *, nb, hp, group):
    qb = pl.program_id(2)
    blk = MOBA_BLOCK
    wub_ref[...] = wu_ref[...].astype(BF16)
    wdb_ref[...] = wd_ref[...].astype(BF16)
    heads = [slice(hh * HEAD_DIM, (hh + 1) * HEAD_DIM) for hh in range(hp)]

    @pl.when(qb == 0)
    def _prep():
        def per_block(j, carry):
            rows = pl.ds(pl.multiple_of(j * blk, blk), blk)
            for hh, cols in enumerate(heads):
                kj = k_ref[rows, cols]
                kb_scr[hh, j] = kj.astype(BF16)
                km_scr[hh, pl.ds(j, 1), :] = jnp.sum(kj, axis=0, keepdims=True) * (1.0 / blk)
                vt_scr[hh, j, 0:HEAD_DIM, :] = v_ref[rows, cols].T.astype(BF16)
                vt_scr[hh, j, HEAD_DIM:HEAD_DIM + ONES_ROWS, :] = jnp.ones((ONES_ROWS, blk), BF16)
            return carry

        lax.fori_loop(0, nb, per_block, 0)

    causal = (lax.broadcasted_iota(jnp.int32, (blk, blk), 0) <= lax.broadcasted_iota(jnp.int32, (blk, blk), 1))
    for hh, cols in enumerate(heads):
        gate = lax.dot_general(km_scr[hh].astype(BF16), q_ref[:, cols], _NT, preferred_element_type=F32)
        cand = lax.broadcasted_iota(jnp.int32, gate.shape, 0)
        sel_scr[hh] = (_topk_select(gate, cand < qb, 0) | (cand == qb)).astype(F32)

    def attend(n_keys):
        first_own = n_keys - group
        for hh, cols in enumerate(heads):
            q = q_ref[:, cols]
            m = jnp.full((1, blk), NEG_INF, F32)
            for j in range(n_keys):
                s = lax.dot_general(kb_scr[hh, j], q, _NT, preferred_element_type=F32)
                s = jnp.where(sel_scr[hh, j:j + 1, :] > 0.0, s, NEG_INF)
                s_scr[hh, j] = s
                bmax = jnp.max(s, axis=0, keepdims=True)
                if j >= first_own:
                    bmax = jnp.where(qb == j, NEG_INF, bmax)
                m = jnp.maximum(m, bmax)
            s_own = jnp.where(causal, s_scr[hh, qb], NEG_INF)
            s_scr[hh, qb] = s_own
            m = jnp.maximum(m, jnp.max(s_own, axis=0, keepdims=True))
            acc = jnp.zeros((HEAD_DIM + ONES_ROWS, blk), F32)
            for j in range(n_keys):
                p = jnp.exp2(s_scr[hh, j] - m)
                acc = acc + jnp.dot(vt_scr[hh, j], p.astype(BF16), preferred_element_type=F32)
            l = acc[HEAD_DIM:HEAD_DIM + 1, :]
            o_ref[:, cols] = (acc[0:HEAD_DIM, :] * (1.0 / l)).T.astype(o_ref.dtype)

    for g in range(nb // group):
        pl.when(qb // group == g)(functools.partial(attend, (g + 1) * group))


def _moba_prompt(q, k, v, n_seq, seq, w_up, w_down, layer):
    nb = seq // MOBA_BLOCK
    hp = MOBA_HEADS_PER_STEP
    group = min(MOBA_QBLOCK_GROUP, nb)
    assert nb % group == 0
    w = hp * HEAD_DIM
    hgs = N_HEADS_A // hp
    steps = n_seq * hgs * nb
    _, d, ff = w_up.shape
    ru, rd = d // steps, ff // steps
    assert ru * steps == d and rd * steps == ff and ru % 16 == 0 and rd % 16 == 0
    step = lambda n, h, b: (n * hgs + h) * nb + b
    return pl.pallas_call(
        functools.partial(_moba_prompt_kernel, nb=nb, hp=hp, group=group),
        grid=(n_seq, hgs, nb),
        in_specs=[pl.BlockSpec((MOBA_BLOCK, w), lambda n, h, b: (n * nb + b, h)),
                  pl.BlockSpec((seq, w), lambda n, h, b: (n, h)),
                  pl.BlockSpec((seq, w), lambda n, h, b: (n, h)),
                  pl.BlockSpec((None, ru, ff), lambda n, h, b: (layer, step(n, h, b), 0)),
                  pl.BlockSpec((None, rd, d), lambda n, h, b: (layer, step(n, h, b), 0))],
        out_specs=[pl.BlockSpec((MOBA_BLOCK, w), lambda n, h, b: (n * nb + b, h)),
                   pl.BlockSpec((ru, ff), lambda n, h, b: (step(n, h, b), 0)),
                   pl.BlockSpec((rd, d), lambda n, h, b: (step(n, h, b), 0))],
        out_shape=[jax.ShapeDtypeStruct((n_seq * seq, W_A), BF16),
                   jax.ShapeDtypeStruct((d, ff), BF16),
                   jax.ShapeDtypeStruct((ff, d), BF16)],
        scratch_shapes=[pltpu.VMEM((hp, nb, MOBA_BLOCK, HEAD_DIM), BF16),
                        pltpu.VMEM((hp, nb, HEAD_DIM + ONES_ROWS, MOBA_BLOCK), BF16),
                        pltpu.VMEM((hp, nb, HEAD_DIM), F32),
                        pltpu.VMEM((hp, nb, MOBA_BLOCK), F32),
                        pltpu.VMEM((hp, nb, MOBA_BLOCK, MOBA_BLOCK), F32)],
        compiler_params=_params("arbitrary", "arbitrary", "arbitrary"),
        name="moba_prompt",
    )(q, k, v, w_up, w_down)


def _gla_levels():
    m, out = 1, []
    while m < GLA_CHUNK:
        out.append(m)
        m *= 2
    return out


def _gla_cum_matrices():
    c = GLA_CHUNK
    t = np.arange(c)
    tri = (t[None, :] <= t[:, None]).astype(np.float32)
    mats = [tri]
    for m in _gla_levels():
        mats.append(tri[(t // (2 * m)) * (2 * m) + m - 1])
    return np.concatenate(mats, axis=0)


def _split_bf16(x):
    hi = x.astype(BF16)
    return hi, (x - hi.astype(F32)).astype(BF16)


def _gla_prompt_kernel(g_ref, pre_ref, cm_ref, bg_ref, nw_ref, o_ref, s_ref):
    c = GLA_CHUNK

    @pl.when(pl.program_id(1) == 0)
    def _init():
        s_ref[...] = jnp.zeros(s_ref.shape, F32)

    tok = lax.broadcasted_iota(jnp.int32, (c, c), 0)
    col = lax.broadcasted_iota(jnp.int32, (c, c), 1)
    tok_d = lax.broadcasted_iota(jnp.int32, (c, DK_B), 0)
    la = _log_sigmoid(pre_ref[...] + bg_ref[...]) * (LOG2E / GATE_TEMP)
    cums_all = sum(jnp.dot(cm_ref[...], piece, preferred_element_type=F32) for piece in _split_bf16(la))
    for h in range(N_HEADS_B):
        dk = slice(h * DK_B, (h + 1) * DK_B)
        cums = cums_all[:, dk]
        b = cums[0:c]
        q = g_ref[:, dk] * (DK_B ** -0.5)
        k = g_ref[:, W_QB + h * DK_B:W_QB + (h + 1) * DK_B]
        v = g_ref[:, 2 * W_QB + h * DV_B:2 * W_QB + (h + 1) * DV_B]
        rb = g_ref[:, 2 * W_QB + W_VB + h * DV_B:2 * W_QB + W_VB + (h + 1) * DV_B]
        vb = v.astype(BF16)

        attn = lax.dot_general(q.astype(BF16), k.astype(BF16), _NT, preferred_element_type=F32)
        attn = jnp.where(col == tok, attn, 0.0)
        for lvl, m in enumerate(_gla_levels()):
            r = cums[(lvl + 1) * c:(lvl + 2) * c]
            upper = ((tok_d // m) % 2) == 1
            x = b - r
            e = jnp.exp2(jnp.where(upper, x, -x))
            e_up = jnp.where(upper, e, 0.0)
            a_m = lax.dot_general((q * e_up).astype(BF16), (k * (e - e_up)).astype(BF16), _NT,
                                  preferred_element_type=F32)
            attn = attn + jnp.where((tok // (2 * m)) == (col // (2 * m)), a_m, 0.0)

        state = s_ref[0, h]
        o = jnp.dot(attn.astype(BF16), vb, preferred_element_type=F32)
        o = o + jnp.dot((q * jnp.exp2(b)).astype(BF16), state.astype(BF16), preferred_element_type=F32)
        b_last = b[c - 1:c, :]
        kd = k * jnp.exp2(b_last - b)
        kv = jnp.dot(kd.T.astype(BF16), vb, preferred_element_type=F32)
        dcol = b.T[:, c - 1:c]
        s_ref[0, h] = jnp.exp2(dcol) * state + kv

        y = _rms(o, nw_ref[...]) * _silu(rb)
        o_ref[:, h * DV_B:(h + 1) * DV_B] = y.astype(o_ref.dtype)


def _gla_prompt(g, pre, bg, nw, n_seq, seq):
    nc = seq // GLA_CHUNK
    cm = jnp.asarray(_gla_cum_matrices(), BF16)
    return pl.pallas_call(
        _gla_prompt_kernel,
        grid=(n_seq, nc),
        in_specs=[pl.BlockSpec((GLA_CHUNK, W_GLA), lambda n, t: (n * nc + t, 0)),
                  pl.BlockSpec((GLA_CHUNK, W_QB), lambda n, t: (n * nc + t, 0)),
                  pl.BlockSpec(cm.shape, lambda n, t: (0, 0)),
                  pl.BlockSpec((1, W_QB), lambda n, t: (0, 0)),
                  pl.BlockSpec((1, DV_B), lambda n, t: (0, 0))],
        out_specs=[pl.BlockSpec((GLA_CHUNK, W_VB), lambda n, t: (n * nc + t, 0)),
                   pl.BlockSpec((1, N_HEADS_B, DK_B, DV_B), lambda n, t: (n, 0, 0, 0))],
        out_shape=[jax.ShapeDtypeStruct((n_seq * seq, W_VB), BF16),
                   jax.ShapeDtypeStruct((n_seq, N_HEADS_B, DK_B, DV_B), F32)],
        compiler_params=_params("arbitrary", "arbitrary"),
        name="gla_prompt",
    )(g, pre, cm, bg.reshape(1, W_QB), nw.reshape(1, DV_B))


def _gla_step_kernel(qc_ref, kc_ref, prec_ref, v_ref, rb_ref, bgc_ref, nw_ref, s0_ref, o_ref, s_ref):
    for i in range(s0_ref.shape[0]):
        for h in range(N_HEADS_B):
            rows = slice(h * DK_B, (h + 1) * DK_B)
            a = jnp.exp(_log_sigmoid(prec_ref[i, rows, :] + bgc_ref[rows, :]) * (1.0 / GATE_TEMP))
            v = v_ref[i, :, h * DV_B:(h + 1) * DV_B]
            s_new = a * s0_ref[i, h] + kc_ref[i, rows, :] * v
            s_ref[i, h] = s_new
            o = jnp.sum((qc_ref[i, rows, :] * (DK_B ** -0.5)) * s_new, axis=0, keepdims=True)
            y = _rms(o, nw_ref[...]) * _silu(rb_ref[i, :, h * DV_B:(h + 1) * DV_B])
            o_ref[i, :, h * DV_B:(h + 1) * DV_B] = y.astype(o_ref.dtype)


def _gla_step(g, pre, bg, nw, s0):
    n = g.shape[0]
    sb = GLA_STEP_SAMPLES if n % GLA_STEP_SAMPLES == 0 else 1
    qc = g[:, 0:W_QB].reshape(n, W_QB, 1)
    kc = g[:, W_QB:2 * W_QB].reshape(n, W_QB, 1)
    v = g[:, 2 * W_QB:2 * W_QB + W_VB].reshape(n, 1, W_VB)
    rb = g[:, 2 * W_QB + W_VB:W_GLA].reshape(n, 1, W_VB)
    per_n3 = lambda i: (i, 0, 0)
    fixed = lambda i: (0, 0)
    o, s = pl.pallas_call(
        _gla_step_kernel,
        grid=(n // sb,),
        in_specs=[pl.BlockSpec((sb, W_QB, 1), per_n3), pl.BlockSpec((sb, W_QB, 1), per_n3),
                  pl.BlockSpec((sb, W_QB, 1), per_n3),
                  pl.BlockSpec((sb, 1, W_VB), per_n3), pl.BlockSpec((sb, 1, W_VB), per_n3),
                  pl.BlockSpec((W_QB, 1), fixed), pl.BlockSpec((1, DV_B), fixed),
                  pl.BlockSpec((sb, N_HEADS_B, DK_B, DV_B), lambda i: (i, 0, 0, 0))],
        out_specs=[pl.BlockSpec((sb, 1, W_VB), per_n3),
                   pl.BlockSpec((sb, N_HEADS_B, DK_B, DV_B), lambda i: (i, 0, 0, 0))],
        out_shape=[jax.ShapeDtypeStruct((n, 1, W_VB), BF16),
                   jax.ShapeDtypeStruct(s0.shape, F32)],
        compiler_params=_params("arbitrary"),
        name="gla_step",
    )(qc, kc, pre.reshape(n, W_QB, 1), v, rb, bg.reshape(W_QB, 1), nw.reshape(1, DV_B), s0)
    return o.reshape(n, W_VB), s


PAGES_PER_STEP = 16


def _page_scores(q_ref, k_refs, o_ref):
    q = q_ref[0].astype(F32)
    shape3 = (PAGE_SIZE, N_HEADS_A, LANES)
    diag = lax.broadcasted_iota(jnp.int32, shape3, 0) == lax.broadcasted_iota(jnp.int32, shape3, 2)
    for p, k_ref in enumerate(k_refs):
        r = jnp.sum(k_ref[0] * q[None], axis=-1, keepdims=True)
        o_ref[0, :, p * PAGE_SIZE:(p + 1) * PAGE_SIZE] = jnp.sum(jnp.where(diag, r, 0.0), axis=0)


def _moba_pick_kernel(s_ref, o_ref, *, nb):
    rows = s_ref.shape[0]
    lane = lax.broadcasted_iota(jnp.int32, (rows, LANES), 1)
    gate = jnp.zeros((rows, LANES), F32)
    for j in range(nb):
        gj = jnp.sum(s_ref[:, j * MOBA_BLOCK:(j + 1) * MOBA_BLOCK], axis=-1, keepdims=True) * (1.0 / MOBA_BLOCK)
        gate = jnp.where(lane == j, gj, gate)
    valid = lane < nb
    gm = jnp.where(valid, gate, NEG_INF)
    rank = jnp.zeros((rows, LANES), F32)
    for c in range(nb):
        gc = gm[:, c:c + 1]
        beats = (gc > gm) | ((gc == gm) & (lane > c))
        rank = rank + beats.astype(F32)
    out = jnp.zeros((rows, LANES), jnp.int32)
    for t in range(MOBA_TOPK):
        pick = jnp.sum(jnp.where(valid & (rank == float(t)), lane, 0), axis=-1, keepdims=True)
        out = jnp.where(lane == t, pick, out)
    o_ref[...] = out


def _moba_pick(scores, nb):
    rows = scores.shape[0]
    return pl.pallas_call(
        functools.partial(_moba_pick_kernel, nb=nb),
        grid=(1,),
        in_specs=[pl.BlockSpec(scores.shape, lambda i: (0, 0))],
        out_specs=pl.BlockSpec((rows, LANES), lambda i: (0, 0)),
        out_shape=jax.ShapeDtypeStruct((rows, LANES), jnp.int32),
        compiler_params=_params("arbitrary"),
        name="moba_pick",
    )(scores)


def _moba_gather_kernel(pt_ref, pick_ref, q_ref, kn_ref, vn_ref, s_ref, *refs):
    per_head = 2 * MOBA_TOPK
    hg = GATHER_HEADS_PER_STEP
    v_refs, o_ref = refs[:hg * per_head], refs[hg * per_head]
    for hh in range(hg):
        row = pl.program_id(0) * N_HEADS_A + pl.program_id(1) * hg + hh
        s_self = jnp.sum(q_ref[0, hh].astype(F32) * kn_ref[0, hh].astype(BF16).astype(F32), axis=-1,
                         keepdims=True)
        s = [s_ref[0, hh, pl.ds(pick_ref[row, t], 1), :] for t in range(MOBA_TOPK)]
        m = s_self
        for st in s:
            m = jnp.maximum(m, jnp.max(st, axis=-1, keepdims=True))
        p_self = jnp.exp(s_self - m)
        l = p_self
        acc = jnp.zeros((8, HEAD_DIM), F32)
        for t, st in enumerate(s):
            p = jnp.exp(st - m)
            l = l + jnp.sum(p, axis=-1, keepdims=True)
            pb = jnp.broadcast_to(p, (8, MOBA_BLOCK)).astype(BF16)
            for half in range(2):
                vpage = v_refs[hh * per_head + 2 * t + half].reshape(PAGE_SIZE, HEAD_DIM)[...].astype(BF16)
                acc = acc + jnp.dot(pb[:, half * PAGE_SIZE:(half + 1) * PAGE_SIZE], vpage,
                                    preferred_element_type=F32)
        out = acc[0:1] + p_self * vn_ref[0, hh].astype(BF16).astype(F32)
        o_ref[0, hh] = (out * (1.0 / l)).astype(o_ref.dtype)


def _moba_gather(cache_v, page_table, picks, scores, q, k_new, v_new, nb):
    n = page_table.shape[0]
    pages_per_block = MOBA_BLOCK // PAGE_SIZE
    assert pages_per_block == 2
    scores4 = scores.reshape(n, N_HEADS_A, nb, MOBA_BLOCK)
    cache_v5 = cache_v.reshape(cache_v.shape[0], PAGE_SIZE, N_HEADS_A, 1, HEAD_DIM)

    hg = GATHER_HEADS_PER_STEP

    def v_spec(hh, t, half):
        def index(i, g, pt, pk):
            h = g * hg + hh
            return (pt[i, pk[i * N_HEADS_A + h, t] * pages_per_block + half], 0, h, 0, 0)
        return pl.BlockSpec((1, PAGE_SIZE, 1, 1, HEAD_DIM), index)

    head_group = lambda i, g, pt, pk: (i, g, 0, 0)
    v_specs = [v_spec(hh, t, half) for hh in range(hg) for t in range(MOBA_TOPK) for half in range(2)]
    return pl.pallas_call(
        _moba_gather_kernel,
        grid_spec=pltpu.PrefetchScalarGridSpec(
            num_scalar_prefetch=2,
            grid=(n, N_HEADS_A // hg),
            in_specs=[pl.BlockSpec((1, hg, 1, HEAD_DIM), head_group)] * 3
                     + [pl.BlockSpec((1, hg, nb, MOBA_BLOCK), head_group)]
                     + v_specs,
            out_specs=pl.BlockSpec((1, hg, 1, HEAD_DIM), head_group),
        ),
        out_shape=jax.ShapeDtypeStruct((n, N_HEADS_A, 1, HEAD_DIM), BF16),
        compiler_params=_params("arbitrary", "arbitrary"),
        name="moba_gather",
    )(page_table, picks,
      q.reshape(n, N_HEADS_A, 1, HEAD_DIM), k_new.reshape(n, N_HEADS_A, 1, HEAD_DIM),
      v_new.reshape(n, N_HEADS_A, 1, HEAD_DIM), scores4,
      *([cache_v5] * len(v_specs)))


def _out_proj_kernel(oa_ref, ob_ref, w_ref, x_ref, g_ref, o_ref, wb_scr):
    @pl.when(pl.program_id(1) == 0)
    def _cast():
        wb_scr[...] = w_ref[...].astype(BF16)

    mix = jnp.dot(oa_ref[...], wb_scr[0:W_A, :], preferred_element_type=F32)
    mix = mix + jnp.dot(ob_ref[...], wb_scr[W_A:W_A + W_VB, :], preferred_element_type=F32)
    o_ref[...] = x_ref[...] + g_ref[...] * mix


def _out_proj(oa, ob, w, layer, x, gate, tm, tn, tiles_per_group):
    m, d = x.shape
    _, r, _ = gate.shape
    return pl.pallas_call(
        _out_proj_kernel,
        grid=(d // tn, m // tm),
        in_specs=[pl.BlockSpec((tm, W_A), lambda j, i: (i, 0)),
                  pl.BlockSpec((tm, W_VB), lambda j, i: (i, 0)),
                  pl.BlockSpec((None, W_A + W_VB, tn), lambda j, i: (layer, 0, j)),
                  pl.BlockSpec((tm, tn), lambda j, i: (i, j)),
                  pl.BlockSpec((None, r, tn), lambda j, i: (i // tiles_per_group, 0, j))],
        out_specs=pl.BlockSpec((tm, tn), lambda j, i: (i, j)),
        out_shape=jax.ShapeDtypeStruct((m, d), F32),
        scratch_shapes=[pltpu.VMEM((W_A + W_VB, tn), BF16)],
        compiler_params=_params("arbitrary", "arbitrary"),
        name="out_proj",
    )(oa, ob, w, x, gate)


def _mlp_scores_kernel(pt_ref, x_ref, sc_ref, sh_ref, g_ref, nw_ref, wu_ref, wd_ref, q_ref, *refs):
    k_refs = refs[:PAGES_PER_STEP]
    o_ref, s_ref, h_scr = refs[PAGES_PER_STEP:]
    _mlp_kernel(x_ref, sc_ref, sh_ref, g_ref, nw_ref, wu_ref, wd_ref, o_ref, h_scr,
                beside_matmuls=functools.partial(_page_scores, q_ref, k_refs, s_ref))


def _mlp_kernel(x_ref, sc_ref, sh_ref, g_ref, nw_ref, wu_ref, wd_ref, o_ref, h_scr, beside_matmuls=None):
    f = pl.program_id(1)
    d = o_ref.shape[1]
    tn = min(d, TN_DENSE)

    @pl.when(f == 0)
    def _first():
        y = _rms(x_ref[...], nw_ref[...])
        h_scr[...] = (y * (1.0 + sc_ref[...]) + sh_ref[...]).astype(BF16)
        o_ref[...] = jnp.zeros(o_ref.shape, F32)

    if beside_matmuls is not None:
        beside_matmuls()
    u = jnp.dot(h_scr[...], wu_ref[...], preferred_element_type=F32)
    r = jnp.maximum(u, 0.0)
    r2 = (r * r).astype(BF16)
    for c0 in range(0, d, tn):
        o_ref[:, c0:c0 + tn] += jnp.dot(r2, wd_ref[:, c0:c0 + tn], preferred_element_type=F32)

    @pl.when(f == pl.num_programs(1) - 1)
    def _last():
        o_ref[...] = x_ref[...] + g_ref[...] * o_ref[...]


def _mlp(x, sc, sh, gate, nw, wu, wd, tm, tf, tiles_per_group):
    m, d = x.shape
    ff = wu.shape[1]
    return pl.pallas_call(
        _mlp_kernel,
        grid=(m // tm, ff // tf),
        in_specs=[pl.BlockSpec((tm, d), lambda i, f: (i, 0)),
                  _mod_spec(sc, tiles_per_group), _mod_spec(sh, tiles_per_group), _mod_spec(gate, tiles_per_group),
                  pl.BlockSpec((1, d), lambda i, f: (0, 0)),
                  pl.BlockSpec((d, tf), lambda i, f: (0, f)),
                  pl.BlockSpec((tf, d), lambda i, f: (f, 0))],
        out_specs=pl.BlockSpec((tm, d), lambda i, f: (i, 0)),
        out_shape=jax.ShapeDtypeStruct((m, d), F32),
        scratch_shapes=[pltpu.VMEM((tm, d), BF16)],
        compiler_params=_params("arbitrary", "arbitrary"),
        name="mlp",
    )(x, sc, sh, gate, nw.reshape(1, d), wu, wd)


def _mlp_with_scores(x, sc, sh, gate, nw, wu, wd, tm, tf, tiles_per_group, cache_k, page_table, q):
    m, d = x.shape
    ff = wu.shape[1]
    n, n_pages = page_table.shape
    nf = ff // tf
    groups = n_pages // PAGES_PER_STEP
    assert (m // tm) * nf == n * groups and n_pages % PAGES_PER_STEP == 0

    def sample(i, f):
        return (i * nf + f) // groups

    def group(i, f):
        return (i * nf + f) % groups

    def page_spec(p):
        return pl.BlockSpec((1, PAGE_SIZE, N_HEADS_A, HEAD_DIM),
                            lambda i, f, pt: (pt[sample(i, f), group(i, f) * PAGES_PER_STEP + p], 0, 0, 0))

    def mod_spec(mod):
        _, r, w = mod.shape
        return pl.BlockSpec((None, r, w), lambda i, f, pt: (i // tiles_per_group, 0, 0))

    y, scores = pl.pallas_call(
        _mlp_scores_kernel,
        grid_spec=pltpu.PrefetchScalarGridSpec(
            num_scalar_prefetch=1,
            grid=(m // tm, nf),
            in_specs=[pl.BlockSpec((tm, d), lambda i, f, pt: (i, 0), pipeline_mode=pl.Buffered(1)),
                      mod_spec(sc), mod_spec(sh), mod_spec(gate),
                      pl.BlockSpec((1, d), lambda i, f, pt: (0, 0)),
                      pl.BlockSpec((d, tf), lambda i, f, pt: (0, f)),
                      pl.BlockSpec((tf, d), lambda i, f, pt: (f, 0)),
                      pl.BlockSpec((1, N_HEADS_A, HEAD_DIM), lambda i, f, pt: (sample(i, f), 0, 0))]
                     + [page_spec(p) for p in range(PAGES_PER_STEP)],
            out_specs=[pl.BlockSpec((tm, d), lambda i, f, pt: (i, 0), pipeline_mode=pl.Buffered(1)),
                       pl.BlockSpec((1, N_HEADS_A, PAGES_PER_STEP * PAGE_SIZE),
                                    lambda i, f, pt: (sample(i, f), 0, group(i, f)))],
            scratch_shapes=[pltpu.VMEM((tm, d), BF16)],
        ),
        out_shape=[jax.ShapeDtypeStruct((m, d), F32),
                   jax.ShapeDtypeStruct((n, N_HEADS_A, n_pages * PAGE_SIZE), F32)],
        compiler_params=_params("arbitrary", "arbitrary"),
        name="mlp_scores",
    )(page_table, x, sc, sh, gate, nw.reshape(1, d), wu, wd, q.reshape(n, N_HEADS_A, HEAD_DIM),
      *([cache_k] * PAGES_PER_STEP))
    return y, scores


def _dense_in(x2, mods, norm_w, w_in_t, layer, w_gate, rope, qk_norm_w, q_scale, tm, tiles_per_group):
    tn = TN_DENSE
    cos, sin = rope
    pos_tiles = cos.shape[0] // tm
    q, h = _norm_proj_rope(x2, mods["sc1"], mods["sh1"], norm_w, tiles_per_group, w_in_t, layer, 0, cos, sin,
                           qk_norm_w[0], tm, pos_tiles, q_scale, BF16)
    k = _proj_rope(h, w_in_t, layer, 1, cos, sin, qk_norm_w[1], tm, pos_tiles, 1.0, True, F32)
    v = _proj(h, w_in_t, layer, 2 * W_A // tn, W_A, tm, tn, transposed=True)
    g = _proj(h, w_in_t, layer, 3 * W_A // tn, W_GLA, tm, tn, transposed=True)
    pre = _proj(h, w_gate, 0, 0, W_QB, tm, W_QB)
    return q, k, v, g, pre


def kernel(x_prompt, x_sample, c_prompt, c_sample, cache_k, cache_v, state_gla, page_table, w_ada, b_ada, norm_mix_w, w_in, q_norm_w, k_norm_w, w_gate_up, b_gate, gla_norm_w, w_out, norm_ffn_w, w_up, w_down):
    n_p, s_p, d = x_prompt.shape
    n_s, s_s, _ = x_sample.shape
    depth = w_ada.shape[0]
    assert depth == 1 and s_s == 1 and d == D_MODEL
    n_pages = page_table.shape[1]
    past_len = n_pages * PAGE_SIZE
    nb_past = past_len // MOBA_BLOCK
    assert past_len % MOBA_BLOCK == 0 and nb_past >= MOBA_TOPK
    l = 0
    tm = min(TM_DENSE, s_p)

    rows = n_p + n_s
    rows_pad = -(-rows // 8) * 8
    c_all = jnp.concatenate([c_prompt, c_sample, jnp.zeros((rows_pad - rows, d), F32)], axis=0)
    ada = _ada(c_all, w_ada[l], b_ada[l])
    names = ("sh1", "sc1", "g1", "sh2", "sc2", "g2")
    mods_p = {nm: ada[0:n_p, i * d:(i + 1) * d].reshape(n_p, 1, d) for i, nm in enumerate(names)}
    mods_s = {nm: ada[n_p:rows, i * d:(i + 1) * d].reshape(1, n_s, d) for i, nm in enumerate(names)}

    w_in_t = jnp.swapaxes(w_in, 1, 2)
    w_gate = _gate_weight(w_in[l, :, 3 * W_A + W_GLA:], w_gate_up[l])[None]
    scale = HEAD_DIM ** -0.5
    qk_norm_w = (q_norm_w[l], k_norm_w[l])

    xp = x_prompt.reshape(n_p * s_p, d)
    rope_p = _rope_tables(jnp.arange(s_p, dtype=jnp.int32))
    q_p, k_p, v_p, g, pre = _dense_in(xp, mods_p, norm_mix_w[l], w_in_t, l, w_gate, rope_p, qk_norm_w,
                                      scale * LOG2E, tm, s_p // tm)
    oa_p, w_up_b, w_down_b = _moba_prompt(q_p, k_p, v_p, n_p, s_p, w_up, w_down, l)
    ob_p, gla_p = _gla_prompt(g, pre, b_gate[l], gla_norm_w[l], n_p, s_p)
    x1_p = _out_proj(oa_p, ob_p, w_out, l, xp, mods_p["g1"], tm, TN_DENSE, s_p // tm)

    xs = x_sample.reshape(n_s, d)
    rope_s = _rope_tables(jnp.full((n_s,), past_len, jnp.int32))
    q_s, k_s, v_s, g, pre = _dense_in(xs, mods_s, norm_mix_w[l], w_in_t, l, w_gate, rope_s, qk_norm_w,
                                      scale, n_s, 1)
    tm_mlp = min(TM_MLP, s_p)
    y_p, scores = _mlp_with_scores(x1_p, mods_p["sc2"], mods_p["sh2"], mods_p["g2"], norm_ffn_w[l],
                                   w_up_b, w_down_b, tm_mlp, TF_MLP_PROMPT, s_p // tm_mlp, cache_k[l], page_table, q_s)
    picks = _moba_pick(scores.reshape(n_s * N_HEADS_A, past_len), nb_past)
    oa_s = _moba_gather(cache_v[l], page_table, picks, scores, q_s, k_s, v_s, nb_past).reshape(n_s, W_A)
    ob_s, gla_s = _gla_step(g, pre, b_gate[l], gla_norm_w[l], state_gla[l])
    x1_s = _out_proj(oa_s, ob_s, w_out, l, xs, mods_s["g1"], n_s, TN_DENSE, 1)
    y_s = _mlp(x1_s, mods_s["sc2"], mods_s["sh2"], mods_s["g2"], norm_ffn_w[l], w_up_b, w_down_b,
               n_s, TF_MLP, 1)

    return (y_p.reshape(n_p, s_p, d), y_s.reshape(n_s, s_s, d),
            k_p.reshape(1, n_p, s_p, N_HEADS_A, HEAD_DIM), v_p.reshape(1, n_p, s_p, N_HEADS_A, HEAD_DIM),
            gla_p[None],
            k_s.reshape(1, n_s, s_s, N_HEADS_A, HEAD_DIM), v_s.reshape(1, n_s, s_s, N_HEADS_A, HEAD_DIM),
            gla_s[None])
```
